```python
import jax, jax.numpy as jnp
from jax import lax
import numpy as np

D_MODEL = 2048
BATCH = 8
SEQ = 4096
DEPTH = 4

N_MIXERS = 3
N_MLSTM_LAYERS = (DEPTH + 2) // 3
N_ATTN_LAYERS = (DEPTH + 1) // 3
N_RWKV_LAYERS = DEPTH // 3

D_FF = 5632
NORM_EPS = 1e-6

MLSTM_HEADS = 8
MLSTM_DQK = D_MODEL // (2 * MLSTM_HEADS)
MLSTM_DV = D_MODEL // MLSTM_HEADS
MLSTM_CHUNK = 64
MLSTM_SPLITS = (MLSTM_HEADS * MLSTM_DQK, 2 * MLSTM_HEADS * MLSTM_DQK,
                2 * MLSTM_HEADS * MLSTM_DQK + MLSTM_HEADS * MLSTM_DV,
                2 * MLSTM_HEADS * MLSTM_DQK + 2 * MLSTM_HEADS * MLSTM_DV,
                2 * MLSTM_HEADS * MLSTM_DQK + 2 * MLSTM_HEADS * MLSTM_DV + MLSTM_HEADS)
MLSTM_IN_COLS = 2 * MLSTM_HEADS * MLSTM_DQK + 2 * MLSTM_HEADS * MLSTM_DV + 2 * MLSTM_HEADS

ATTN_HEAD_DIM = 64
ATTN_Q_HEADS = D_MODEL // ATTN_HEAD_DIM
ATTN_GROUP = 8
ATTN_KV_HEADS = ATTN_Q_HEADS // ATTN_GROUP
ATTN_WINDOW = 128
ATTN_BLOCK = 128
ROPE_DIM = ATTN_HEAD_DIM // 4
ROPE_THETA = 500000.0
ATTN_QKV_COLS = (ATTN_Q_HEADS + 2 * ATTN_KV_HEADS) * ATTN_HEAD_DIM

RWKV_HEAD = 64
RWKV_HEADS = D_MODEL // RWKV_HEAD
RWKV_DECAY_LORA = 96
RWKV_AAA_LORA = 96
RWKV_GATE_LORA = 256
RWKV_LN_EPS = 64e-5

kernel_name = "hybrid_mlstm_swa_rwkv7_macaron"


def rmsnorm(x, gain):
    xf = x.astype(jnp.float32)
    y = xf * lax.rsqrt(jnp.mean(xf * xf, axis=-1, keepdims=True) + NORM_EPS)
    return (y * gain.astype(jnp.float32)).astype(x.dtype)


def swiglu(h, w_gu, w_down):
    gate, up = jnp.split(h @ w_gu, 2, axis=-1)
    return (jax.nn.silu(gate) * up) @ w_down


def rope_partial(x, cos, sin):
    half = ROPE_DIM // 2
    xf = x[..., :ROPE_DIM].astype(jnp.float32)
    x1, x2 = xf[..., :half], xf[..., half:]
    rot = jnp.concatenate([x1 * cos - x2 * sin, x2 * cos + x1 * sin], axis=-1).astype(x.dtype)
    return jnp.concatenate([rot, x[..., ROPE_DIM:]], axis=-1)


def mlstm_chunk_step(carry, xs):
    c_state, n_state, m_state = carry
    q, k, v, log_i, log_f = xs
    L = q.shape[2]
    causal = jnp.tril(jnp.ones((L, L), dtype=bool))
    b = jnp.cumsum(log_f, axis=-1)
    g = b[..., -1]
    log_d = jnp.where(causal, b[..., :, None] - b[..., None, :] + log_i[..., None, :], -jnp.inf)
    log_inter = b + m_state[..., None]
    m_out = jnp.maximum(log_inter, jnp.max(log_d, axis=-1))
    d_mat = jnp.exp(log_d - m_out[..., None])
    w_inter = jnp.exp(log_inter - m_out)
    s = jnp.einsum('bhtd,bhsd->bhts', q, k) * d_mat
    num = jnp.einsum('bhts,bhsv->bhtv', s, v) + w_inter[..., None] * jnp.einsum('bhvd,bhtd->bhtv', c_state, q)
    den = jnp.sum(s, axis=-1) + w_inter * jnp.einsum('bhd,bhtd->bht', n_state, q)
    h = num / jnp.maximum(jnp.abs(den), jnp.exp(-m_out))[..., None]
    log_w = g[..., None] - b + log_i
    m_new = jnp.maximum(g + m_state, jnp.max(log_w, axis=-1))
    w = jnp.exp(log_w - m_new[..., None])
    decay = jnp.exp(g + m_state - m_new)
    c_new = decay[..., None, None] * c_state + jnp.einsum('bhsv,bhsd->bhvd', v * w[..., None], k)
    n_new = decay[..., None] * n_state + jnp.einsum('bhs,bhsd->bhd', w, k)
    return (c_new, n_new, m_new), h


def mlstm_mixer(h, w_in, b_gate, head_gain, w_out):
    B, S, _ = h.shape
    H, DK, DV, L = MLSTM_HEADS, MLSTM_DQK, MLSTM_DV, MLSTM_CHUNK
    nc = S // L
    q, k, v, o, ig, fg = jnp.split(h @ w_in, MLSTM_SPLITS, axis=-1)

    def to_chunks(t, dh):
        return t.astype(jnp.float32).reshape(B, nc, L, H, dh).transpose(1, 0, 3, 2, 4)

    qc = to_chunks(q, DK)
    kc = to_chunks(k, DK) * (DK ** -0.5)
    vc = to_chunks(v, DV)
    bg = b_gate.astype(jnp.float32)
    log_i = (ig.astype(jnp.float32) + bg[0]).reshape(B, nc, L, H).transpose(1, 0, 3, 2)
    log_f = jax.nn.log_sigmoid(fg.astype(jnp.float32) + bg[1]).reshape(B, nc, L, H).transpose(1, 0, 3, 2)
    init = (jnp.zeros((B, H, DV, DK), jnp.float32), jnp.zeros((B, H, DK), jnp.float32),
            jnp.zeros((B, H), jnp.float32))
    _, hc = lax.scan(mlstm_chunk_step, init, (qc, kc, vc, log_i, log_f))
    hs = hc.transpose(1, 0, 3, 2, 4).reshape(B, S, H, DV)
    hs = rmsnorm(hs, head_gain).astype(h.dtype)
    return (hs.reshape(B, S, H * DV) * jax.nn.sigmoid(o)) @ w_out


def swa_sink_attention(h, positions, w_qkv, q_gain, k_gain, sinks, w_o):
    B, S, _ = h.shape
    HQ, HKV, G, DH, BLK = ATTN_Q_HEADS, ATTN_KV_HEADS, ATTN_GROUP, ATTN_HEAD_DIM, ATTN_BLOCK
    nb = S // BLK
    q, k, v = jnp.split(h @ w_qkv, (HQ * DH, (HQ + HKV) * DH), axis=-1)
    q = q.reshape(B, S, HQ, DH)
    k = k.reshape(B, S, HKV, DH)
    v = v.reshape(B, S, HKV, DH)
    inv_freq = ROPE_THETA ** (-jnp.arange(0, ROPE_DIM, 2, dtype=jnp.float32) / ROPE_DIM)
    ang = positions.astype(jnp.float32)[..., None] * inv_freq
    cos, sin = jnp.cos(ang)[:, :, None, :], jnp.sin(ang)[:, :, None, :]
    q = rope_partial(rmsnorm(q, q_gain), cos, sin)
    k = rope_partial(rmsnorm(k, k_gain), cos, sin)
    qb = q.reshape(B, nb, BLK, HKV, G, DH).transpose(1, 0, 2, 3, 4, 5)

    def banded(t):
        tp = jnp.pad(t, ((0, 0), (BLK, 0), (0, 0), (0, 0))).reshape(B, nb + 1, BLK, HKV, DH)
        return jnp.concatenate([tp[:, :-1], tp[:, 1:]], axis=2).transpose(1, 0, 2, 3, 4)

    kb, vb = banded(k), banded(v)
    qi = jnp.arange(BLK)[:, None] + BLK
    kj = jnp.arange(2 * BLK)[None, :]
    local_ok = (qi >= kj) & (qi - kj < ATTN_WINDOW)
    key_ok = local_ok[None] & ((jnp.arange(nb) > 0)[:, None, None] | (kj >= BLK)[None])
    sink_logit = sinks.astype(jnp.float32).reshape(HKV, G)[None, :, :, None, None]
    scale = DH ** -0.5

    def block(args):
        q_blk, k_blk, v_blk, ok = args
        s = jnp.einsum('bqhgd,bkhd->bhgqk', q_blk, k_blk).astype(jnp.float32) * scale
        s = jnp.where(ok, s, -jnp.inf)
        sink_col = jnp.broadcast_to(sink_logit, s.shape[:-1] + (1,))
        p = jax.nn.softmax(jnp.concatenate([s, sink_col], axis=-1), axis=-1)[..., :-1]
        return jnp.einsum('bhgqk,bkhd->bqhgd', p.astype(v_blk.dtype), v_blk)

    ob = lax.map(block, (qb, kb, vb, key_ok))
    o = ob.transpose(1, 0, 2, 3, 4, 5).reshape(B, S, HQ * DH)
    return o @ w_o


def rwkv7_step(state, xs):
    r, w, k, v, a_vec, b_vec = xs
    sa = jnp.einsum('bhvk,bhk->bhv', state, a_vec)
    state = state * w[:, :, None, :] + sa[..., None] * b_vec[:, :, None, :] + v[..., None] * k[:, :, None, :]
    y = jnp.einsum('bhvk,bhk->bhv', state, r)
    return state, y


def rwkv7_mixer(h, mix, w_rkv, w0, w_la, w_lb, a0, a_la, a_lb, g_la, g_lb, k_k, k_a, r_k, ln_w, ln_b, w_o):
    B, S, D = h.shape
    H, N = RWKV_HEADS, RWKV_HEAD
    dx = jnp.pad(h, ((0, 0), (1, 0), (0, 0)))[:, :-1] - h
    xr, xw, xk, xv, xa, xg = [h + dx * mix[i] for i in range(6)]
    r = xr @ w_rkv[0]
    k = xk @ w_rkv[1]
    v = xv @ w_rkv[2]
    log_w = -jax.nn.softplus(-(w0 + jnp.tanh(xw @ w_la) @ w_lb)) - 0.5
    a = jax.nn.sigmoid(a0 + (xa @ a_la) @ a_lb)
    g = jax.nn.sigmoid(xg @ g_la) @ g_lb

    def heads(t):
        return t.astype(jnp.float32).reshape(B, S, H, N)

    kk = heads(k * k_k)
    kk = kk / jnp.maximum(jnp.sqrt(jnp.sum(kk * kk, axis=-1, keepdims=True)), 1e-12)
    k = k * (1.0 + (a - 1.0) * k_a)
    r_h, k_h, v_h, a_h = heads(r), heads(k), heads(v), heads(a)
    decay = jnp.exp(-jnp.exp(heads(log_w)))

    def to_time(t):
        return t.transpose(1, 0, 2, 3)

    xs = tuple(map(to_time, (r_h, decay, k_h, v_h, -kk, kk * a_h)))
    _, y = lax.scan(rwkv7_step, jnp.zeros((B, H, N, N), jnp.float32), xs)
    y = to_time(y)
    mu = jnp.mean(y, axis=-1, keepdims=True)
    var = jnp.mean(jnp.square(y - mu), axis=-1, keepdims=True)
    y = (y - mu) * lax.rsqrt(var + RWKV_LN_EPS) * ln_w.astype(jnp.float32).reshape(H, N) \
        + ln_b.astype(jnp.float32).reshape(H, N)
    y = y + jnp.sum(r_h * k_h * r_k.astype(jnp.float32), axis=-1, keepdims=True) * v_h
    return (y.reshape(B, S, D).astype(h.dtype) * g) @ w_o


def setup_inputs(seed: int = 0) -> dict:
    key = jax.random.key(seed)
    ks = iter(jax.random.split(key, 64))
    D, F = D_MODEL, D_FF
    NA, NB_, NC = N_MLSTM_LAYERS, N_ATTN_LAYERS, N_RWKV_LAYERS

    def nrm(shape, scale):
        return jax.random.normal(next(ks), shape, jnp.float32) * scale

    def gain(shape):
        return 1.0 + nrm(shape, 0.02)

    x = nrm((BATCH, SEQ, D), 1.0)
    positions = jnp.arange(SEQ, dtype=jnp.int32)[None, :] + jax.random.randint(
        next(ks), (BATCH, 1), 0, 4096, dtype=jnp.int32)
    inp = {
        "x": x,
        "positions": positions,
        "ffn1_norm": gain((DEPTH, D)),
        "ffn1_w_gu": nrm((DEPTH, D, 2 * F), D ** -0.5),
        "ffn1_w_down": nrm((DEPTH, F, D), F ** -0.5),
        "mixer_norm": gain((DEPTH, D)),
        "ffn2_norm": gain((DEPTH, D)),
        "ffn2_w_gu": nrm((DEPTH, D, 2 * F), D ** -0.5),
        "ffn2_w_down": nrm((DEPTH, F, D), F ** -0.5),
        "mlstm_w_in": nrm((NA, D, MLSTM_IN_COLS), D ** -0.5),
        "mlstm_b_gate": jnp.stack([nrm((NA, MLSTM_HEADS), 0.1),
                                   jnp.linspace(3.0, 6.0, MLSTM_HEADS)[None, :] + nrm((NA, MLSTM_HEADS), 0.1)], axis=1),
        "mlstm_head_gain": gain((NA, MLSTM_HEADS, MLSTM_DV)),
        "mlstm_w_out": nrm((NA, MLSTM_HEADS * MLSTM_DV, D), (MLSTM_HEADS * MLSTM_DV) ** -0.5),
        "attn_w_qkv": nrm((NB_, D, ATTN_QKV_COLS), D ** -0.5),
        "attn_q_gain": gain((NB_, ATTN_HEAD_DIM)),
        "attn_k_gain": gain((NB_, ATTN_HEAD_DIM)),
        "attn_sinks": nrm((NB_, ATTN_Q_HEADS), 0.5),
        "attn_w_o": nrm((NB_, ATTN_Q_HEADS * ATTN_HEAD_DIM, D), (ATTN_Q_HEADS * ATTN_HEAD_DIM) ** -0.5),
        "rwkv_mix": jax.random.uniform(next(ks), (NC, 6, D), jnp.float32),
        "rwkv_w_rkv": nrm((NC, 3, D, D), D ** -0.5),
        "rwkv_w0": jnp.linspace(-6.5, -1.5, D)[None, :] + nrm((NC, D), 0.1),
        "rwkv_w_lora_a": nrm((NC, D, RWKV_DECAY_LORA), D ** -0.5),
        "rwkv_w_lora_b": nrm((NC, RWKV_DECAY_LORA, D), 0.1 * RWKV_DECAY_LORA ** -0.5),
        "rwkv_a0": nrm((NC, D), 0.1),
        "rwkv_a_lora_a": nrm((NC, D, RWKV_AAA_LORA), D ** -0.5),
        "rwkv_a_lora_b": nrm((NC, RWKV_AAA_LORA, D), RWKV_AAA_LORA ** -0.5),
        "rwkv_g_lora_a": nrm((NC, D, RWKV_GATE_LORA), D ** -0.5),
        "rwkv_g_lora_b": nrm((NC, RWKV_GATE_LORA, D), RWKV_GATE_LORA ** -0.5),
        "rwkv_k_k": 0.85 + nrm((NC, D), 0.02),
        "rwkv_k_a": 1.0 + nrm((NC, D), 0.02),
        "rwkv_r_k": nrm((NC, RWKV_HEADS, RWKV_HEAD), 0.1),
        "rwkv_ln_w": gain((NC, D)),
        "rwkv_ln_b": nrm((NC, D), 0.02),
        "rwkv_w_o": nrm((NC, D, D), D ** -0.5),
    }
    return inp


def reference(x, positions, ffn1_norm, ffn1_w_gu, ffn1_w_down, mixer_norm, ffn2_norm, ffn2_w_gu, ffn2_w_down,
              mlstm_w_in, mlstm_b_gate, mlstm_head_gain, mlstm_w_out,
              attn_w_qkv, attn_q_gain, attn_k_gain, attn_sinks, attn_w_o,
              rwkv_mix, rwkv_w_rkv, rwkv_w0, rwkv_w_lora_a, rwkv_w_lora_b, rwkv_a0, rwkv_a_lora_a, rwkv_a_lora_b,
              rwkv_g_lora_a, rwkv_g_lora_b, rwkv_k_k, rwkv_k_a, rwkv_r_k, rwkv_ln_w, rwkv_ln_b, rwkv_w_o):
    for i in range(DEPTH):
        x = x + 0.5 * swiglu(rmsnorm(x, ffn1_norm[i]), ffn1_w_gu[i], ffn1_w_down[i])
        hn = rmsnorm(x, mixer_norm[i])
        kind, j = i % N_MIXERS, i // N_MIXERS
        if kind == 0:
            y = mlstm_mixer(hn, mlstm_w_in[j], mlstm_b_gate[j], mlstm_head_gain[j], mlstm_w_out[j])
        elif kind == 1:
            y = swa_sink_attention(hn, positions, attn_w_qkv[j], attn_q_gain[j], attn_k_gain[j],
                                   attn_sinks[j], attn_w_o[j])
        else:
            y = rwkv7_mixer(hn, rwkv_mix[j], rwkv_w_rkv[j], rwkv_w0[j], rwkv_w_lora_a[j], rwkv_w_lora_b[j],
                            rwkv_a0[j], rwkv_a_lora_a[j], rwkv_a_lora_b[j], rwkv_g_lora_a[j], rwkv_g_lora_b[j],
                            rwkv_k_k[j], rwkv_k_a[j], rwkv_r_k[j], rwkv_ln_w[j], rwkv_ln_b[j], rwkv_w_o[j])
        x = x + y
        x = x + 0.5 * swiglu(rmsnorm(x, ffn2_norm[i]), ffn2_w_gu[i], ffn2_w_down[i])
    return x
```

```python
import functools

import jax
import jax.numpy as jnp
from jax import lax
from jax.experimental import pallas as pl
from jax.experimental.pallas import tpu as pltpu

F32 = jnp.float32
BF16 = jnp.bfloat16

NORM_EPS = 1e-6

MLSTM_HEADS = 8
MLSTM_DQK = 128
MLSTM_DV = 256
MLSTM_CHUNK = 64

ATTN_HEAD_DIM = 64
ATTN_GROUP = 8
ATTN_WINDOW = 128
ROPE_DIM = 16
ROPE_THETA = 500000.0

RWKV_HEAD = 64
RWKV_CHUNK = 64
RWKV_LN_EPS = 64e-5
RWKV_LORA_PAD = 128

LANES = 128
VMEM_LIMIT_BYTES = 56 * 1024 * 1024


def _compiler_params(semantics):
    return pltpu.CompilerParams(dimension_semantics=semantics, vmem_limit_bytes=VMEM_LIMIT_BYTES)


def _row_tile(seq, preferred):
    tile = preferred
    while seq % tile:
        tile //= 2
    return tile


def _dot(a, b):
    return jnp.dot(a, b, preferred_element_type=F32)


def _dot_nt(a, b):
    return lax.dot_general(a, b, (((1,), (1,)), ((), ())), preferred_element_type=F32)


def _dot_tn(a, b):
    return lax.dot_general(a, b, (((0,), (0,)), ((), ())), preferred_element_type=F32)


def _sigmoid(x):
    return 1.0 / (1.0 + jnp.exp(-x))


def _rmsnorm_rows(x, gain):
    ms = jnp.mean(x * x, axis=-1, keepdims=True)
    return x * lax.rsqrt(ms + NORM_EPS) * gain


def _split3(x):
    hi = x.astype(BF16)
    r1 = x - hi.astype(F32)
    mid = r1.astype(BF16)
    lo = (r1 - mid.astype(F32)).astype(BF16)
    return hi, mid, lo


def _cumsum_time(tril_bf, x):
    hi, mid, lo = _split3(x)
    return _dot(tril_bf, hi) + _dot(tril_bf, mid) + _dot(tril_bf, lo)


def _ffn_kernel(x_ref, gain_ref, wg_ref, wu_ref, wd_ref, o_ref, xn_ref):
    j = pl.program_id(1)

    @pl.when(j == 0)
    def _():
        xn_ref[...] = _rmsnorm_rows(x_ref[...], gain_ref[...]).astype(BF16)
        o_ref[...] = jnp.zeros_like(o_ref)

    xn = xn_ref[...]
    gate = _dot(xn, wg_ref[...])
    up = _dot(xn, wu_ref[...])
    act = (gate * _sigmoid(gate) * up).astype(BF16)
    o_ref[...] += _dot(act, wd_ref[...])

    @pl.when(j == pl.num_programs(1) - 1)
    def _():
        o_ref[...] = x_ref[...] + 0.5 * o_ref[...]


def _ffn(x, gain, w_gu, w_down, *, seq):
    tokens, d = x.shape
    f = w_down.shape[0]
    tm = _row_tile(seq, 512)
    tf = 512
    assert f % tf == 0
    nf = f // tf
    return pl.pallas_call(
        _ffn_kernel,
        grid=(tokens // tm, nf),
        in_specs=[
            pl.BlockSpec((tm, d), lambda i, j: (i, 0)),
            pl.BlockSpec((1, d), lambda i, j: (0, 0)),
            pl.BlockSpec((d, tf), lambda i, j: (0, j)),
            pl.BlockSpec((d, tf), lambda i, j: (0, j + nf)),
            pl.BlockSpec((tf, d), lambda i, j: (j, 0)),
        ],
        out_specs=pl.BlockSpec((tm, d), lambda i, j: (i, 0)),
        out_shape=jax.ShapeDtypeStruct((tokens, d), F32),
        scratch_shapes=[pltpu.VMEM((tm, d), BF16)],
        compiler_params=_compiler_params(("parallel", "arbitrary")),
        name="ffn",
    )(x, gain.reshape(1, d), w_gu, w_gu, w_down)


def _proj_kernel(*refs, norm, residual):
    if residual:
        x_ref, gain_ref, w_ref, res_ref, o_ref, xn_ref = refs
    else:
        x_ref, gain_ref, w_ref, o_ref, xn_ref = refs
        res_ref = None

    @pl.when(pl.program_id(1) == 0)
    def _():
        x = x_ref[...]
        if norm:
            x = _rmsnorm_rows(x, gain_ref[...])
        xn_ref[...] = x.astype(BF16)

    acc = _dot(xn_ref[...], w_ref[...])
    if residual:
        acc = acc + res_ref[...]
    o_ref[...] = acc


def _proj(x, w, *, seq, gain=None, residual=None, tn=512):
    tokens, k = x.shape
    n = w.shape[1]
    tm = _row_tile(seq, 1024)
    assert n % tn == 0
    norm = gain is not None
    gain_arr = (gain if norm else jnp.ones((k,), F32)).reshape(1, k)
    in_specs = [
        pl.BlockSpec((tm, k), lambda i, j: (i, 0)),
        pl.BlockSpec((1, k), lambda i, j: (0, 0)),
        pl.BlockSpec((k, tn), lambda i, j: (0, j)),
    ]
    args = [x, gain_arr, w]
    if residual is not None:
        in_specs.append(pl.BlockSpec((tm, tn), lambda i, j: (i, j)))
        args.append(residual)
    return pl.pallas_call(
        functools.partial(_proj_kernel, norm=norm, residual=residual is not None),
        grid=(tokens // tm, n // tn),
        in_specs=in_specs,
        out_specs=pl.BlockSpec((tm, tn), lambda i, j: (i, j)),
        out_shape=jax.ShapeDtypeStruct((tokens, n), F32),
        scratch_shapes=[pltpu.VMEM((tm, k), BF16)],
        compiler_params=_compiler_params(("parallel", "arbitrary")),
        name="proj",
    )(*args)


def _pad_cols(w, width):
    return jnp.pad(w, ((0, 0), (0, width - w.shape[1])))


def _mlstm_kernel(q_ref, k_ref, v_ref, og_ref, g_ref, bias_ref, hg_ref, o_ref, ct_ref, n_ref, m_ref):
    L, H, DK, DV = MLSTM_CHUNK, MLSTM_HEADS, MLSTM_DQK, MLSTM_DV

    @pl.when(pl.program_id(1) == 0)
    def _():
        ct_ref[...] = jnp.zeros_like(ct_ref)
        n_ref[...] = jnp.zeros_like(n_ref)
        m_ref[...] = jnp.zeros_like(m_ref)

    row = lax.broadcasted_iota(jnp.int32, (L, L), 0)
    col = lax.broadcasted_iota(jnp.int32, (L, L), 1)
    causal = row >= col
    tril_bf = causal.astype(BF16)

    z = g_ref[...] + bias_ref[...]
    log_f = jnp.minimum(z, 0.0) - jnp.log(1.0 + jnp.exp(-jnp.abs(z)))
    bcum = _cumsum_time(tril_bf, log_f)
    z_t = z.T
    bcum_t = bcum.T

    for h in range(H):
        q = q_ref[:, h * DK:(h + 1) * DK]
        k = k_ref[:, h * DK:(h + 1) * DK] * (DK ** -0.5)
        v = v_ref[:, h * DV:(h + 1) * DV]
        li_col = z[:, h:h + 1]
        li_row = z_t[h:h + 1, :]
        b_col = bcum[:, H + h:H + h + 1]
        b_row = bcum_t[H + h:H + h + 1, :]
        g = bcum[L - 1:L, H + h:H + h + 1]
        m_prev = m_ref[h]

        log_d = jnp.where(causal, b_col - b_row + li_row, -jnp.inf)
        log_inter = b_col + m_prev
        m_out = jnp.maximum(log_inter, jnp.max(log_d, axis=-1, keepdims=True))
        d_mat = jnp.exp(log_d - m_out)
        w_inter = jnp.exp(log_inter - m_out)

        q_bf = q.astype(BF16)
        k_bf = k.astype(BF16)
        v_bf = v.astype(BF16)
        s = _dot_nt(q_bf, k_bf) * d_mat
        num = _dot(s.astype(BF16), v_bf) + w_inter * _dot(q_bf, ct_ref[h].astype(BF16))
        den = jnp.sum(s, axis=-1, keepdims=True) + w_inter * jnp.sum(q * n_ref[h], axis=-1, keepdims=True)
        hout = num / jnp.maximum(jnp.abs(den), jnp.exp(-m_out))

        hn = _rmsnorm_rows(hout, hg_ref[:, h * DV:(h + 1) * DV])
        o_ref[:, h * DV:(h + 1) * DV] = hn * _sigmoid(og_ref[:, h * DV:(h + 1) * DV])

        log_w = g - b_col + li_col
        m_new = jnp.maximum(g + m_prev, jnp.max(log_w, axis=0, keepdims=True))
        w_col = jnp.exp(log_w - m_new)
        decay = jnp.exp(g + m_prev - m_new)
        kw = k * w_col
        ct_ref[h] = decay * ct_ref[h] + _dot_tn(kw.astype(BF16), v_bf)
        n_ref[h] = decay * n_ref[h] + jnp.sum(kw, axis=0, keepdims=True)
        m_ref[h] = m_new


def _mlstm_mixer(x, norm_gain, w_in, b_gate, head_gain, w_out, *, batch, seq):
    tokens, d = x.shape
    L, H, DK, DV = MLSTM_CHUNK, MLSTM_HEADS, MLSTM_DQK, MLSTM_DV
    nc = seq // L
    main = 2 * H * DK + 2 * H * DV
    w_pad = _pad_cols(w_in, main + 512).astype(BF16)
    zin = _proj(x, w_pad, seq=seq, gain=norm_gain)
    bias = jnp.pad(b_gate.reshape(1, 2 * H), ((0, 0), (0, LANES - 2 * H)))
    qk_w, v_w = H * DK, H * DV
    gated = pl.pallas_call(
        _mlstm_kernel,
        grid=(batch, nc),
        in_specs=[
            pl.BlockSpec((L, qk_w), lambda b, c: (b * nc + c, 0)),
            pl.BlockSpec((L, qk_w), lambda b, c: (b * nc + c, 1)),
            pl.BlockSpec((L, v_w), lambda b, c: (b * nc + c, 2 * qk_w // v_w)),
            pl.BlockSpec((L, v_w), lambda b, c: (b * nc + c, 2 * qk_w // v_w + 1)),
            pl.BlockSpec((L, LANES), lambda b, c: (b * nc + c, main // LANES)),
            pl.BlockSpec((1, LANES), lambda b, c: (0, 0)),
            pl.BlockSpec((1, v_w), lambda b, c: (0, 0)),
        ],
        out_specs=pl.BlockSpec((L, v_w), lambda b, c: (b * nc + c, 0)),
        out_shape=jax.ShapeDtypeStruct((tokens, v_w), F32),
        scratch_shapes=[
            pltpu.VMEM((H, DK, DV), F32),
            pltpu.VMEM((H, 1, DK), F32),
            pltpu.VMEM((H, 1, 1), F32),
        ],
        compiler_params=_compiler_params(("parallel", "arbitrary")),
        name="mlstm_scan",
    )(zin, zin, zin, zin, zin, bias, head_gain.reshape(1, v_w))
    return _proj(gated, w_out.astype(BF16), seq=seq, residual=x)


def _qk_norm_rope(t, gain2, cos_t, sin_t, lane):
    dh = ATTN_HEAD_DIM
    half = ROPE_DIM // 2
    first = lane < dh
    sq = t * t
    s0 = jnp.sum(jnp.where(first, sq, 0.0), axis=-1, keepdims=True)
    s1 = jnp.sum(jnp.where(first, 0.0, sq), axis=-1, keepdims=True)
    ms = jnp.where(first, s0, s1) * (1.0 / dh)
    tn = t * lax.rsqrt(ms + NORM_EPS) * gain2
    swapped = jnp.where(lane % dh < half, pltpu.roll(tn, LANES - half, 1), pltpu.roll(tn, half, 1))
    return tn * cos_t + swapped * sin_t


def _swa_kernel(q_ref, kc_ref, kp_ref, vc_ref, vp_ref, pc_ref, pp_ref, freq_ref, qg_ref, kg_ref, sink_ref, o_ref):
    blk, dh, grp = ATTN_WINDOW, ATTN_HEAD_DIM, ATTN_GROUP
    half = ROPE_DIM // 2
    n_kv = kc_ref.shape[1] // dh
    has_prev = pl.program_id(1) > 0

    lane = lax.broadcasted_iota(jnp.int32, (blk, LANES), 1)
    in_rot = lane % dh < ROPE_DIM

    def rope_tables(pos_ref):
        ang = pos_ref[...].astype(F32) * freq_ref[...]
        cos_t = jnp.where(in_rot, jnp.cos(ang), 1.0)
        sin = jnp.sin(ang)
        sin_t = jnp.where(in_rot, jnp.where(lane % dh < half, -sin, sin), 0.0)
        return cos_t, sin_t

    cos_c, sin_c = rope_tables(pc_ref)
    cos_p, sin_p = rope_tables(pp_ref)

    qi = lax.broadcasted_iota(jnp.int32, (blk, 2 * blk), 0) + blk
    kj = lax.broadcasted_iota(jnp.int32, (blk, 2 * blk), 1)
    first_key = jnp.where(has_prev, 0, blk)
    ok = (qi >= kj) & (qi - kj < ATTN_WINDOW) & (kj >= first_key)
    scale = dh ** -0.5

    k_heads, v_heads = [], []
    for c in range(n_kv * dh // LANES):
        sl = slice(c * LANES, (c + 1) * LANES)
        kc = _qk_norm_rope(kc_ref[:, sl], kg_ref[...], cos_c, sin_c, lane)
        kp = _qk_norm_rope(kp_ref[:, sl], kg_ref[...], cos_p, sin_p, lane)
        kcat = jnp.concatenate([kp, kc], axis=0).astype(BF16)
        vcat = jnp.concatenate([vp_ref[:, sl], vc_ref[:, sl]], axis=0).astype(BF16)
        for e in range(LANES // dh):
            k_heads.append(kcat[:, e * dh:(e + 1) * dh])
            v_heads.append(vcat[:, e * dh:(e + 1) * dh])

    for c in range(q_ref.shape[1] // LANES):
        sl = slice(c * LANES, (c + 1) * LANES)
        q2 = _qk_norm_rope(q_ref[:, sl], qg_ref[...], cos_c, sin_c, lane).astype(BF16)
        outs = []
        for e in range(LANES // dh):
            head = c * (LANES // dh) + e
            kv = head // grp
            s = _dot_nt(q2[:, e * dh:(e + 1) * dh], k_heads[kv]) * scale
            s = jnp.where(ok, s, -jnp.inf)
            sink = sink_ref[:, head:head + 1]
            m = jnp.maximum(jnp.max(s, axis=-1, keepdims=True), sink)
            p = jnp.exp(s - m)
            denom = jnp.sum(p, axis=-1, keepdims=True) + jnp.exp(sink - m)
            outs.append(_dot((p / denom).astype(BF16), v_heads[kv]))
        o_ref[:, sl] = jnp.concatenate(outs, axis=1)


def _swa_mixer(x, positions, norm_gain, w_qkv, q_gain, k_gain, sinks, w_o, *, batch, seq):
    tokens, d = x.shape
    blk, dh = ATTN_WINDOW, ATTN_HEAD_DIM
    nb = seq // blk
    n_q = sinks.shape[0]
    q_w = n_q * dh
    kv_w = (w_qkv.shape[1] - q_w) // 2
    qkv = _proj(x, w_qkv.astype(BF16), seq=seq, gain=norm_gain)
    pos = positions.reshape(tokens, 1)
    inv_freq = ROPE_THETA ** (-jnp.arange(0, ROPE_DIM, 2, dtype=F32) / ROPE_DIM)
    freq_head = jnp.concatenate([inv_freq, inv_freq, jnp.zeros((dh - ROPE_DIM,), F32)])
    freq = jnp.tile(freq_head, LANES // dh).reshape(1, LANES)
    tile2 = lambda g: jnp.tile(g, LANES // dh).reshape(1, LANES)
    sink_row = jnp.pad(sinks.reshape(1, n_q), ((0, 0), (0, LANES - n_q)))
    cur = lambda b, i: b * nb + i
    prev = lambda b, i: b * nb + jnp.maximum(i - 1, 0)
    k_blk, v_blk = q_w // kv_w, q_w // kv_w + 1
    small = lambda b, i: (0, 0)
    attn = pl.pallas_call(
        _swa_kernel,
        grid=(batch, nb),
        in_specs=[
            pl.BlockSpec((blk, q_w), lambda b, i: (cur(b, i), 0)),
            pl.BlockSpec((blk, kv_w), lambda b, i: (cur(b, i), k_blk)),
            pl.BlockSpec((blk, kv_w), lambda b, i: (prev(b, i), k_blk)),
            pl.BlockSpec((blk, kv_w), lambda b, i: (cur(b, i), v_blk)),
            pl.BlockSpec((blk, kv_w), lambda b, i: (prev(b, i), v_blk)),
            pl.BlockSpec((blk, 1), lambda b, i: (cur(b, i), 0)),
            pl.BlockSpec((blk, 1), lambda b, i: (prev(b, i), 0)),
            pl.BlockSpec((1, LANES), small),
            pl.BlockSpec((1, LANES), small),
            pl.BlockSpec((1, LANES), small),
            pl.BlockSpec((1, LANES), small),
        ],
        out_specs=pl.BlockSpec((blk, q_w), lambda b, i: (cur(b, i), 0)),
        out_shape=jax.ShapeDtypeStruct((tokens, q_w), F32),
        compiler_params=_compiler_params(("parallel", "arbitrary")),
        name="swa",
    )(qkv, qkv, qkv, qkv, qkv, pos, pos, freq, tile2(q_gain), tile2(k_gain), sink_row)
    return _proj(attn, w_o.astype(BF16), seq=seq, residual=x)


def _rwkv_proj_kernel(x_ref, xb_ref, gain_ref, mix_ref, w_ref, o_ref, xm_ref, *, n_big):
    j = pl.program_id(1)
    tm = x_ref.shape[0]
    tn = o_ref.shape[1]

    @pl.when(j == 0)
    def _():
        hn = _rmsnorm_rows(x_ref[...], gain_ref[...])
        hb = _rmsnorm_rows(xb_ref[0], gain_ref[...])
        row = lax.broadcasted_iota(jnp.int32, hn.shape, 0)
        dx = jnp.where(row == 0, hb, pltpu.roll(hn, 1, 0)) - hn
        for i in range(6):
            xm_ref[i] = (hn + dx * mix_ref[i:i + 1, :]).astype(BF16)

    per_proj = n_big // 3

    @pl.when(j < n_big)
    def _():
        src = jnp.where(j < per_proj, 0, jnp.where(j < 2 * per_proj, 2, 3))
        o_ref[...] = _dot(xm_ref[src], w_ref[...])

    @pl.when(j == n_big)
    def _():
        pad = RWKV_LORA_PAD
        o_ref[:, 0:pad] = _dot(xm_ref[1], w_ref[:, 0:pad])
        o_ref[:, pad:2 * pad] = _dot(xm_ref[4], w_ref[:, pad:2 * pad])
        o_ref[:, 2 * pad:tn] = _dot(xm_ref[5], w_ref[:, 2 * pad:tn])


def _rwkv_scan_kernel(r_ref, k_ref, v_ref, lo_ref, w0_ref, wlb_ref, a0_ref, alb_ref, glb_ref,
                      kk_ref, ka_ref, rk_ref, lnw_ref, lnb_ref, o_ref, st_ref):
    C, N = RWKV_CHUNK, RWKV_HEAD
    pad = RWKV_LORA_PAD
    n_pairs = r_ref.shape[1] // LANES

    @pl.when(pl.program_id(1) == 0)
    def _():
        st_ref[...] = jnp.zeros_like(st_ref)

    lane = lax.broadcasted_iota(jnp.int32, (C, LANES), 1)
    first = lane < N
    row_c = lax.broadcasted_iota(jnp.int32, (C, C), 0)
    col_c = lax.broadcasted_iota(jnp.int32, (C, C), 1)
    tril_bf = (row_c >= col_c).astype(BF16)
    row2 = lax.broadcasted_iota(jnp.int32, (2 * C, 2 * C), 0)
    col2 = lax.broadcasted_iota(jnp.int32, (2 * C, 2 * C), 1)
    same_head = (row2 // C) == (col2 // C)
    strict_lower = same_head & (row2 % C > col2 % C)
    lower = same_head & (row2 % C >= col2 % C)
    eye2 = row2 == col2

    def head_sum(t):
        s0 = jnp.sum(jnp.where(first, t, 0.0), axis=-1, keepdims=True)
        s1 = jnp.sum(jnp.where(first, 0.0, t), axis=-1, keepdims=True)
        return jnp.where(first, s0, s1)

    def stack(t):
        return jnp.concatenate([jnp.where(first, t, 0.0), jnp.where(first, 0.0, t)], axis=0)

    lw1 = jnp.tanh(lo_ref[:, 0:pad]).astype(BF16)
    la1 = lo_ref[:, pad:2 * pad].astype(BF16)
    lg1 = _sigmoid(lo_ref[:, 2 * pad:]).astype(BF16)

    for p in range(n_pairs):
        sl = slice(p * LANES, (p + 1) * LANES)
        r = r_ref[:, sl]
        k = k_ref[:, sl]
        v = v_ref[:, sl]
        zw = w0_ref[:, sl] + _dot(lw1, wlb_ref[:, sl])
        softplus = jnp.maximum(-zw, 0.0) + jnp.log(1.0 + jnp.exp(-jnp.abs(zw)))
        log_decay = -jnp.exp(-softplus - 0.5)
        a = _sigmoid(a0_ref[:, sl] + _dot(la1, alb_ref[:, sl]))
        g = _dot(lg1, glb_ref[:, sl])
        kk = k * kk_ref[:, sl]
        kk = kk / jnp.maximum(jnp.sqrt(head_sum(kk * kk)), 1e-12)
        k2 = k * (1.0 + (a - 1.0) * ka_ref[:, sl])

        cum = _cumsum_time(tril_bf, log_decay)
        total = cum[C - 1:C, :]
        inv_p = jnp.exp(-cum)
        a_t = -kk * jnp.exp(cum - log_decay)
        b_t = kk * a * inv_p
        k_t = k2 * inv_p
        r_t = r * jnp.exp(cum)
        to_end = jnp.exp(total - cum)
        b_e = kk * a * to_end
        k_e = k2 * to_end

        la = stack(a_t).astype(BF16)
        lr = stack(r_t).astype(BF16)
        sb = stack(b_t).astype(BF16)
        sk = stack(k_t).astype(BF16)
        v_st = stack(v).astype(BF16)
        a_ab = jnp.where(strict_lower, _dot_nt(la, sb), 0.0)
        a_ak = jnp.where(strict_lower, _dot_nt(la, sk), 0.0)
        a_rb = jnp.where(lower, _dot_nt(lr, sb), 0.0)
        a_rk = jnp.where(lower, _dot_nt(lr, sk), 0.0)

        pw = a_ab
        t_inv = jnp.where(eye2, 1.0, 0.0) + a_ab
        steps = C.bit_length() - 2
        for _ in range(steps):
            pw_bf = pw.astype(BF16)
            pw = _dot(pw_bf, pw_bf)
            t_inv = t_inv + _dot(t_inv.astype(BF16), pw.astype(BF16))

        m0 = st_ref[p]
        m0_bf = m0.astype(BF16)
        x_rhs = _dot(la, m0_bf) + _dot(a_ak.astype(BF16), v_st)
        u = _dot(t_inv.astype(BF16), x_rhs.astype(BF16))
        u_bf = u.astype(BF16)
        y_st = _dot(lr, m0_bf) + _dot(a_rb.astype(BF16), u_bf) + _dot(a_rk.astype(BF16), v_st)
        y = y_st[:C] + y_st[C:]

        total_col = jnp.sum(jnp.where(eye2, total, 0.0), axis=-1, keepdims=True)
        st_ref[p] = (jnp.exp(total_col) * m0
                     + _dot_tn(stack(b_e).astype(BF16), u_bf)
                     + _dot_tn(stack(k_e).astype(BF16), v_st))

        mu = head_sum(y) * (1.0 / N)
        yc = y - mu
        var = head_sum(yc * yc) * (1.0 / N)
        yn = yc * lax.rsqrt(var + RWKV_LN_EPS) * lnw_ref[:, sl] + lnb_ref[:, sl]
        bonus = head_sum(r * k2 * rk_ref[:, sl])
        o_ref[:, sl] = (yn + bonus * v) * g


def _rwkv_mixer(x, norm_gain, mix, w_rkv, w0, w_la, w_lb, a0, a_la, a_lb, g_la, g_lb, k_k, k_a, r_k,
                ln_w, ln_b, w_o, *, batch, seq):
    tokens, d = x.shape
    C = RWKV_CHUNK
    pad = RWKV_LORA_PAD
    tm = _row_tile(seq, 512)
    tn = 512
    n_big = 3 * d // tn
    n_tiles = tokens // tm
    lora_w = w_la.shape[1], a_la.shape[1], g_la.shape[1]
    assert lora_w[0] <= pad and lora_w[1] <= pad and 2 * pad + lora_w[2] == tn and 2 * C == LANES
    w_cat = jnp.concatenate([w_rkv[0], w_rkv[1], w_rkv[2], _pad_cols(w_la, pad), _pad_cols(a_la, pad), g_la],
                            axis=1).astype(BF16)
    starts = (jnp.arange(n_tiles) * tm) % seq == 0
    xb = jnp.where(starts[:, None], 0.0, jnp.roll(x[tm - 1::tm], 1, axis=0)).reshape(n_tiles, 1, d)
    proj = pl.pallas_call(
        functools.partial(_rwkv_proj_kernel, n_big=n_big),
        grid=(n_tiles, n_big + 1),
        in_specs=[
            pl.BlockSpec((tm, d), lambda i, j: (i, 0)),
            pl.BlockSpec((1, 1, d), lambda i, j: (i, 0, 0)),
            pl.BlockSpec((1, d), lambda i, j: (0, 0)),
            pl.BlockSpec((6, d), lambda i, j: (0, 0)),
            pl.BlockSpec((d, tn), lambda i, j: (0, j)),
        ],
        out_specs=pl.BlockSpec((tm, tn), lambda i, j: (i, j)),
        out_shape=jax.ShapeDtypeStruct((tokens, 3 * d + tn), F32),
        scratch_shapes=[pltpu.VMEM((6, tm, d), BF16)],
        compiler_params=_compiler_params(("parallel", "arbitrary")),
        name="rwkv_proj",
    )(x, xb, norm_gain.reshape(1, d), mix, w_cat)

    nc = seq // C
    row1 = lambda t: t.reshape(1, d)
    pad_rows = lambda w: jnp.pad(w, ((0, pad - w.shape[0]), (0, 0))).astype(BF16)
    chunk = lambda blk: (lambda b, c: (b * nc + c, blk))
    const = lambda b, c: (0, 0)
    vec = pl.BlockSpec((1, d), const)
    mixed = pl.pallas_call(
        _rwkv_scan_kernel,
        grid=(batch, nc),
        in_specs=[
            pl.BlockSpec((C, d), chunk(0)),
            pl.BlockSpec((C, d), chunk(1)),
            pl.BlockSpec((C, d), chunk(2)),
            pl.BlockSpec((C, tn), chunk(3 * d // tn)),
            vec,
            pl.BlockSpec((pad, d), const),
            vec,
            pl.BlockSpec((pad, d), const),
            pl.BlockSpec((lora_w[2], d), const),
            vec, vec, vec, vec, vec,
        ],
        out_specs=pl.BlockSpec((C, d), chunk(0)),
        out_shape=jax.ShapeDtypeStruct((tokens, d), F32),
        scratch_shapes=[pltpu.VMEM((d // LANES, LANES, LANES), F32)],
        compiler_params=_compiler_params(("parallel", "arbitrary")),
        name="rwkv_scan",
    )(proj, proj, proj, proj, row1(w0), pad_rows(w_lb), row1(a0), pad_rows(a_lb), g_lb.astype(BF16),
      row1(k_k), row1(k_a), row1(r_k), row1(ln_w), row1(ln_b))
    return _proj(mixed, w_o.astype(BF16), seq=seq, residual=x)


def kernel(x, positions, ffn1_norm, ffn1_w_gu, ffn1_w_down, mixer_norm, ffn2_norm, ffn2_w_gu, ffn2_w_down,
           mlstm_w_in, mlstm_b_gate, mlstm_head_gain, mlstm_w_out,
           attn_w_qkv, attn_q_gain, attn_k_gain, attn_sinks, attn_w_o,
           rwkv_mix, rwkv_w_rkv, rwkv_w0, rwkv_w_lora_a, rwkv_w_lora_b, rwkv_a0, rwkv_a_lora_a, rwkv_a_lora_b,
           rwkv_g_lora_a, rwkv_g_lora_b, rwkv_k_k, rwkv_k_a, rwkv_r_k, rwkv_ln_w, rwkv_ln_b, rwkv_w_o):
    batch, seq, d = x.shape
    depth = ffn1_norm.shape[0]
    h = x.reshape(batch * seq, d)
    for i in range(depth):
        h = _ffn(h, ffn1_norm[i], ffn1_w_gu[i].astype(BF16), ffn1_w_down[i].astype(BF16), seq=seq)
        kind, j = i % 3, i // 3
        if kind == 0:
            h = _mlstm_mixer(h, mixer_norm[i], mlstm_w_in[j], mlstm_b_gate[j], mlstm_head_gain[j],
                             mlstm_w_out[j], batch=batch, seq=seq)
        elif kind == 1:
            h = _swa_mixer(h, positions, mixer_norm[i], attn_w_qkv[j], attn_q_gain[j], attn_k_gain[j],
                           attn_sinks[j], attn_w_o[j], batch=batch, seq=seq)
        else:
            h = _rwkv_mixer(h, mixer_norm[i], rwkv_mix[j], rwkv_w_rkv[j], rwkv_w0[j], rwkv_w_lora_a[j],
                            rwkv_w_lora_b[j], rwkv_a0[j], rwkv_a_lora_a[j], rwkv_a_lora_b[j], rwkv_g_lora_a[j],
                            rwkv_g_lora_b[j], rwkv_k_k[j], rwkv_k_a[j], rwkv_r_k[j], rwkv_ln_w[j], rwkv_ln_b[j],
                            rwkv_w_o[j], batch=batch, seq=seq)
        h = _ffn(h, ffn2_norm[i], ffn2_w_gu[i].astype(BF16), ffn2_w_down[i].astype(BF16), seq=seq)
    return h.reshape(batch, seq, d)
```

```python
import functools

import jax
import jax.numpy as jnp
from jax import lax
from jax.experimental import pallas as pl
from jax.experimental.pallas import tpu as pltpu

F32 = jnp.float32
BF16 = jnp.bfloat16

NORM_EPS = 1e-6

MLSTM_HEADS = 8
MLSTM_DQK = 128
MLSTM_DV = 256
MLSTM_CHUNK = 64

ATTN_HEAD_DIM = 64
ATTN_GROUP = 8
ATTN_WINDOW = 128
ROPE_DIM = 16
ROPE_THETA = 500000.0

RWKV_HEAD = 64
RWKV_CHUNK = 64
RWKV_LN_EPS = 64e-5
RWKV_LORA_PAD = 128

LANES = 128
VMEM_LIMIT_BYTES = 56 * 1024 * 1024


def _compiler_params(semantics):
    return pltpu.CompilerParams(dimension_semantics=semantics, vmem_limit_bytes=VMEM_LIMIT_BYTES)


def _row_tile(seq, preferred):
    tile = preferred
    while seq % tile:
        tile //= 2
    return tile


def _dot(a, b):
    return jnp.dot(a, b, preferred_element_type=F32)


def _dot_nt(a, b):
    return lax.dot_general(a, b, (((1,), (1,)), ((), ())), preferred_element_type=F32)


def _dot_tn(a, b):
    return lax.dot_general(a, b, (((0,), (0,)), ((), ())), preferred_element_type=F32)


def _sigmoid(x):
    return 1.0 / (1.0 + jnp.exp(-x))


def _rmsnorm_rows(x, gain):
    ms = jnp.mean(x * x, axis=-1, keepdims=True)
    return x * lax.rsqrt(ms + NORM_EPS) * gain


def _split3(x):
    hi = x.astype(BF16)
    r1 = x - hi.astype(F32)
    mid = r1.astype(BF16)
    lo = (r1 - mid.astype(F32)).astype(BF16)
    return hi, mid, lo


def _cumsum_time(tril_bf, x):
    hi, mid, lo = _split3(x)
    return _dot(tril_bf, hi) + _dot(tril_bf, mid) + _dot(tril_bf, lo)


def _ffn_kernel(x_ref, gain_ref, wg_ref, wu_ref, wd_ref, o_ref, xn_ref):
    j = pl.program_id(1)

    @pl.when(j == 0)
    def _():
        xn_ref[...] = _rmsnorm_rows(x_ref[...], gain_ref[...]).astype(BF16)
        o_ref[...] = jnp.zeros_like(o_ref)

    xn = xn_ref[...]
    gate = _dot(xn, wg_ref[...])
    up = _dot(xn, wu_ref[...])
    act = (gate * _sigmoid(gate) * up).astype(BF16)
    o_ref[...] += _dot(act, wd_ref[...])

    @pl.when(j == pl.num_programs(1) - 1)
    def _():
        o_ref[...] = x_ref[...] + 0.5 * o_ref[...]


def _ffn(x, gain, w_gu, w_down, *, seq):
    tokens, d = x.shape
    f = w_down.shape[0]
    tm = _row_tile(seq, 512)
    tf = 512
    assert f % tf == 0
    nf = f // tf
    return pl.pallas_call(
        _ffn_kernel,
        grid=(tokens // tm, nf),
        in_specs=[
            pl.BlockSpec((tm, d), lambda i, j: (i, 0)),
            pl.BlockSpec((1, d), lambda i, j: (0, 0)),
            pl.BlockSpec((d, tf), lambda i, j: (0, j)),
            pl.BlockSpec((d, tf), lambda i, j: (0, j + nf)),
            pl.BlockSpec((tf, d), lambda i, j: (j, 0)),
        ],
        out_specs=pl.BlockSpec((tm, d), lambda i, j: (i, 0)),
        out_shape=jax.ShapeDtypeStruct((tokens, d), F32),
        scratch_shapes=[pltpu.VMEM((tm, d), BF16)],
        compiler_params=_compiler_params(("parallel", "arbitrary")),
        name="ffn",
    )(x, gain.reshape(1, d), w_gu, w_gu, w_down)


def _proj_kernel(*refs, norm, residual):
    if residual:
        x_ref, gain_ref, w_ref, res_ref, o_ref, xn_ref = refs
    else:
        x_ref, gain_ref, w_ref, o_ref, xn_ref = refs
        res_ref = None

    @pl.when(pl.program_id(1) == 0)
    def _():
        x = x_ref[...]
        if norm:
            x = _rmsnorm_rows(x, gain_ref[...])
        xn_ref[...] = x.astype(BF16)

    acc = _dot(xn_ref[...], w_ref[...])
    if residual:
        acc = acc + res_ref[...]
    o_ref[...] = acc


def _proj(x, w, *, seq, gain=None, residual=None, tn=512):
    tokens, k = x.shape
    n = w.shape[1]
    tm = _row_tile(seq, 1024)
    assert n % tn == 0
    norm = gain is not None
    gain_arr = (gain if norm else jnp.ones((k,), F32)).reshape(1, k)
    in_specs = [
        pl.BlockSpec((tm, k), lambda i, j: (i, 0)),
        pl.BlockSpec((1, k), lambda i, j: (0, 0)),
        pl.BlockSpec((k, tn), lambda i, j: (0, j)),
    ]
    args = [x, gain_arr, w]
    if residual is not None:
        in_specs.append(pl.BlockSpec((tm, tn), lambda i, j: (i, j)))
        args.append(residual)
    return pl.pallas_call(
        functools.partial(_proj_kernel, norm=norm, residual=residual is not None),
        grid=(tokens // tm, n // tn),
        in_specs=in_specs,
        out_specs=pl.BlockSpec((tm, tn), lambda i, j: (i, j)),
        out_shape=jax.ShapeDtypeStruct((tokens, n), F32),
        scratch_shapes=[pltpu.VMEM((tm, k), BF16)],
        compiler_params=_compiler_params(("parallel", "arbitrary")),
        name="proj",
    )(*args)


def _pad_cols(w, width):
    return jnp.pad(w, ((0, 0), (0, width - w.shape[1])))


def _mlstm_kernel(q_ref, k_ref, v_ref, og_ref, g_ref, bias_ref, hg_ref, o_ref, ct_ref, n_ref, m_ref):
    L, H, DK, DV = MLSTM_CHUNK, MLSTM_HEADS, MLSTM_DQK, MLSTM_DV

    @pl.when(pl.program_id(1) == 0)
    def _():
        ct_ref[...] = jnp.zeros_like(ct_ref)
        n_ref[...] = jnp.zeros_like(n_ref)
        m_ref[...] = jnp.zeros_like(m_ref)

    row = lax.broadcasted_iota(jnp.int32, (L, L), 0)
    col = lax.broadcasted_iota(jnp.int32, (L, L), 1)
    causal = row >= col
    tril_bf = causal.astype(BF16)

    z = g_ref[...] + bias_ref[...]
    log_f = jnp.minimum(z, 0.0) - jnp.log(1.0 + jnp.exp(-jnp.abs(z)))
    bcum = _cumsum_time(tril_bf, log_f)
    z_t = z.T
    bcum_t = bcum.T

    for h in range(H):
        q = q_ref[:, h * DK:(h + 1) * DK]
        k = k_ref[:, h * DK:(h + 1) * DK] * (DK ** -0.5)
        v = v_ref[:, h * DV:(h + 1) * DV]
        li_col = z[:, h:h + 1]
        li_row = z_t[h:h + 1, :]
        b_col = bcum[:, H + h:H + h + 1]
        b_row = bcum_t[H + h:H + h + 1, :]
        g = bcum[L - 1:L, H + h:H + h + 1]
        m_prev = m_ref[h]

        log_d = jnp.where(causal, b_col - b_row + li_row, -jnp.inf)
        log_inter = b_col + m_prev
        m_out = jnp.maximum(log_inter, jnp.max(log_d, axis=-1, keepdims=True))
        d_mat = jnp.exp(log_d - m_out)
        w_inter = jnp.exp(log_inter - m_out)

        q_bf = q.astype(BF16)
        k_bf = k.astype(BF16)
        v_bf = v.astype(BF16)
        s = _dot_nt(q_bf, k_bf) * d_mat
        num = _dot(s.astype(BF16), v_bf) + w_inter * _dot(q_bf, ct_ref[h].astype(BF16))
        den = jnp.sum(s, axis=-1, keepdims=True) + w_inter * jnp.sum(q * n_ref[h], axis=-1, keepdims=True)
        hout = num / jnp.maximum(jnp.abs(den), jnp.exp(-m_out))

        hn = _rmsnorm_rows(hout, hg_ref[:, h * DV:(h + 1) * DV])
        o_ref[:, h * DV:(h + 1) * DV] = hn * _sigmoid(og_ref[:, h * DV:(h + 1) * DV])

        log_w = g - b_col + li_col
        m_new = jnp.maximum(g + m_prev, jnp.max(log_w, axis=0, keepdims=True))
        w_col = jnp.exp(log_w - m_new)
        decay = jnp.exp(g + m_prev - m_new)
        kw = k * w_col
        ct_ref[h] = decay * ct_ref[h] + _dot_tn(kw.astype(BF16), v_bf)
        n_ref[h] = decay * n_ref[h] + jnp.sum(kw, axis=0, keepdims=True)
        m_ref[h] = m_new


def _mlstm_mixer(x, norm_gain, w_in, b_gate, head_gain, w_out, *, batch, seq):
    tokens, d = x.shape
    L, H, DK, DV = MLSTM_CHUNK, MLSTM_HEADS, MLSTM_DQK, MLSTM_DV
    nc = seq // L
    main = 2 * H * DK + 2 * H * DV
    w_pad = _pad_cols(w_in, main + 512).astype(BF16)
    zin = _proj(x, w_pad, seq=seq, gain=norm_gain)
    bias = jnp.pad(b_gate.reshape(1, 2 * H), ((0, 0), (0, LANES - 2 * H)))
    qk_w, v_w = H * DK, H * DV
    gated = pl.pallas_call(
        _mlstm_kernel,
        grid=(batch, nc),
        in_specs=[
            pl.BlockSpec((L, qk_w), lambda b, c: (b * nc + c, 0)),
            pl.BlockSpec((L, qk_w), lambda b, c: (b * nc + c, 1)),
            pl.BlockSpec((L, v_w), lambda b, c: (b * nc + c, 2 * qk_w // v_w)),
            pl.BlockSpec((L, v_w), lambda b, c: (b * nc + c, 2 * qk_w // v_w + 1)),
            pl.BlockSpec((L, LANES), lambda b, c: (b * nc + c, main // LANES)),
            pl.BlockSpec((1, LANES), lambda b, c: (0, 0)),
            pl.BlockSpec((1, v_w), lambda b, c: (0, 0)),
        ],
        out_specs=pl.BlockSpec((L, v_w), lambda b, c: (b * nc + c, 0)),
        out_shape=jax.ShapeDtypeStruct((tokens, v_w), F32),
        scratch_shapes=[
            pltpu.VMEM((H, DK, DV), F32),
            pltpu.VMEM((H, 1, DK), F32),
            pltpu.VMEM((H, 1, 1), F32),
        ],
        compiler_params=_compiler_params(("parallel", "arbitrary")),
        name="mlstm_scan",
    )(zin, zin, zin, zin, zin, bias, head_gain.reshape(1, v_w))
    return _proj(gated, w_out.astype(BF16), seq=seq, residual=x)


def _qk_norm_rope(t, gain2, cos_t, sin_t, lane):
    dh = ATTN_HEAD_DIM
    half = ROPE_DIM // 2
    first = lane < dh
    sq = t * t
    s0 = jnp.sum(jnp.where(first, sq, 0.0), axis=-1, keepdims=True)
    s1 = jnp.sum(jnp.where(first, 0.0, sq), axis=-1, keepdims=True)
    ms = jnp.where(first, s0, s1) * (1.0 / dh)
    tn = t * lax.rsqrt(ms + NORM_EPS) * gain2
    swapped = jnp.where(lane % dh < half, pltpu.roll(tn, LANES - half, 1), pltpu.roll(tn, half, 1))
    return tn * cos_t + swapped * sin_t


def _swa_kernel(q_ref, kc_ref, kp_ref, vc_ref, vp_ref, pc_ref, pp_ref, freq_ref, qg_ref, kg_ref, sink_ref, o_ref):
    blk, dh, grp = ATTN_WINDOW, ATTN_HEAD_DIM, ATTN_GROUP
    half = ROPE_DIM // 2
    n_kv = kc_ref.shape[1] // dh
    has_prev = pl.program_id(1) > 0

    lane = lax.broadcasted_iota(jnp.int32, (blk, LANES), 1)
    in_rot = lane % dh < ROPE_DIM

    def rope_tables(pos_ref):
        ang = pos_ref[...].astype(F32) * freq_ref[...]
        cos_t = jnp.where(in_rot, jnp.cos(ang), 1.0)
        sin = jnp.sin(ang)
        sin_t = jnp.where(in_rot, jnp.where(lane % dh < half, -sin, sin), 0.0)
        return cos_t, sin_t

    cos_c, sin_c = rope_tables(pc_ref)
    cos_p, sin_p = rope_tables(pp_ref)

    qi = lax.broadcasted_iota(jnp.int32, (blk, 2 * blk), 0) + blk
    kj = lax.broadcasted_iota(jnp.int32, (blk, 2 * blk), 1)
    first_key = jnp.where(has_prev, 0, blk)
    ok = (qi >= kj) & (qi - kj < ATTN_WINDOW) & (kj >= first_key)
    scale = dh ** -0.5
    assert LANES == 2 * dh and scale == 0.125
    first = lane < dh
    first2 = lax.broadcasted_iota(jnp.int32, (2 * blk, LANES), 1) < dh
    slabs_per_kv = grp * dh // LANES

    k_dup, v_split = [], []
    for c in range(n_kv * dh // LANES):
        sl = slice(c * LANES, (c + 1) * LANES)
        kc = _qk_norm_rope(kc_ref[:, sl], kg_ref[...], cos_c, sin_c, lane)
        kp = _qk_norm_rope(kp_ref[:, sl], kg_ref[...], cos_p, sin_p, lane)
        kcat = jnp.concatenate([kp, kc], axis=0)
        vcat = jnp.concatenate([vp_ref[:, sl], vc_ref[:, sl]], axis=0)
        k_rot = pltpu.roll(kcat, dh, 1)
        v_rot = pltpu.roll(vcat, dh, 1)
        k_dup.append(jnp.where(first2, kcat, k_rot).astype(BF16))
        k_dup.append(jnp.where(first2, k_rot, kcat).astype(BF16))
        v_split.append(jnp.concatenate([jnp.where(first2, vcat, 0.0), jnp.where(first2, 0.0, v_rot)],
                                       axis=0).astype(BF16))
        v_split.append(jnp.concatenate([jnp.where(first2, v_rot, 0.0), jnp.where(first2, 0.0, vcat)],
                                       axis=0).astype(BF16))

    for kv in range(n_kv):
        parts = []
        for c in range(kv * slabs_per_kv, (kv + 1) * slabs_per_kv):
            q2 = _qk_norm_rope(q_ref[:, c * LANES:(c + 1) * LANES], qg_ref[...], cos_c, sin_c, lane) * scale
            parts += [jnp.where(first, q2, 0.0), jnp.where(first, 0.0, q2)]
        q_st = jnp.concatenate(parts, axis=0).astype(BF16)
        s_all = _dot_nt(q_st, k_dup[kv])
        probs, rinv = [], []
        for hb in range(grp):
            sink = sink_ref[:, kv * grp + hb:kv * grp + hb + 1]
            s = jnp.where(ok, s_all[hb * blk:(hb + 1) * blk], -jnp.inf)
            m = jnp.maximum(jnp.max(s, axis=-1, keepdims=True), sink)
            p = jnp.exp(s - m)
            rinv.append(1.0 / (jnp.sum(p, axis=-1, keepdims=True) + jnp.exp(sink - m)))
            probs.append(p.astype(BF16))
        for ci in range(slabs_per_kv):
            c = kv * slabs_per_kv + ci
            p_cat = jnp.concatenate([probs[2 * ci], probs[2 * ci + 1]], axis=1)
            out = _dot(p_cat, v_split[kv])
            o_ref[:, c * LANES:(c + 1) * LANES] = out * jnp.where(first, rinv[2 * ci], rinv[2 * ci + 1])


def _swa_mixer(x, positions, norm_gain, w_qkv, q_gain, k_gain, sinks, w_o, *, batch, seq):
    tokens, d = x.shape
    blk, dh = ATTN_WINDOW, ATTN_HEAD_DIM
    nb = seq // blk
    n_q = sinks.shape[0]
    q_w = n_q * dh
    kv_w = (w_qkv.shape[1] - q_w) // 2
    qkv = _proj(x, w_qkv.astype(BF16), seq=seq, gain=norm_gain)
    pos = positions.reshape(tokens, 1)
    inv_freq = ROPE_THETA ** (-jnp.arange(0, ROPE_DIM, 2, dtype=F32) / ROPE_DIM)
    freq_head = jnp.concatenate([inv_freq, inv_freq, jnp.zeros((dh - ROPE_DIM,), F32)])
    freq = jnp.tile(freq_head, LANES // dh).reshape(1, LANES)
    tile2 = lambda g: jnp.tile(g, LANES // dh).reshape(1, LANES)
    sink_row = jnp.pad(sinks.reshape(1, n_q), ((0, 0), (0, LANES - n_q)))
    cur = lambda b, i: b * nb + i
    prev = lambda b, i: b * nb + jnp.maximum(i - 1, 0)
    k_blk, v_blk = q_w // kv_w, q_w // kv_w + 1
    small = lambda b, i: (0, 0)
    attn = pl.pallas_call(
        _swa_kernel,
        grid=(batch, nb),
        in_specs=[
            pl.BlockSpec((blk, q_w), lambda b, i: (cur(b, i), 0)),
            pl.BlockSpec((blk, kv_w), lambda b, i: (cur(b, i), k_blk)),
            pl.BlockSpec((blk, kv_w), lambda b, i: (prev(b, i), k_blk)),
            pl.BlockSpec((blk, kv_w), lambda b, i: (cur(b, i), v_blk)),
            pl.BlockSpec((blk, kv_w), lambda b, i: (prev(b, i), v_blk)),
            pl.BlockSpec((blk, 1), lambda b, i: (cur(b, i), 0)),
            pl.BlockSpec((blk, 1), lambda b, i: (prev(b, i), 0)),
            pl.BlockSpec((1, LANES), small),
            pl.BlockSpec((1, LANES), small),
            pl.BlockSpec((1, LANES), small),
            pl.BlockSpec((1, LANES), small),
        ],
        out_specs=pl.BlockSpec((blk, q_w), lambda b, i: (cur(b, i), 0)),
        out_shape=jax.ShapeDtypeStruct((tokens, q_w), F32),
        compiler_params=_compiler_params(("parallel", "arbitrary")),
        name="swa",
    )(qkv, qkv, qkv, qkv, qkv, pos, pos, freq, tile2(q_gain), tile2(k_gain), sink_row)
    return _proj(attn, w_o.astype(BF16), seq=seq, residual=x)


def _rwkv_proj_kernel(x_ref, xb_ref, gain_ref, mix_ref, w_ref, o_ref, xm_ref, *, n_big):
    j = pl.program_id(1)
    tm = x_ref.shape[0]
    tn = o_ref.shape[1]

    @pl.when(j == 0)
    def _():
        hn = _rmsnorm_rows(x_ref[...], gain_ref[...])
        hb = _rmsnorm_rows(xb_ref[0], gain_ref[...])
        row = lax.broadcasted_iota(jnp.int32, hn.shape, 0)
        dx = jnp.where(row == 0, hb, pltpu.roll(hn, 1, 0)) - hn
        for i in range(6):
            xm_ref[i] = (hn + dx * mix_ref[i:i + 1, :]).astype(BF16)

    per_proj = n_big // 3

    @pl.when(j < n_big)
    def _():
        src = jnp.where(j < per_proj, 0, jnp.where(j < 2 * per_proj, 2, 3))
        o_ref[...] = _dot(xm_ref[src], w_ref[...])

    @pl.when(j == n_big)
    def _():
        pad = RWKV_LORA_PAD
        o_ref[:, 0:pad] = _dot(xm_ref[1], w_ref[:, 0:pad])
        o_ref[:, pad:2 * pad] = _dot(xm_ref[4], w_ref[:, pad:2 * pad])
        o_ref[:, 2 * pad:tn] = _dot(xm_ref[5], w_ref[:, 2 * pad:tn])


def _rwkv_scan_kernel(r_ref, k_ref, v_ref, lo_ref, w0_ref, wlb_ref, a0_ref, alb_ref, glb_ref,
                      kk_ref, ka_ref, rk_ref, lnw_ref, lnb_ref, o_ref, st_ref):
    C, N = RWKV_CHUNK, RWKV_HEAD
    pad = RWKV_LORA_PAD
    n_pairs = r_ref.shape[1] // LANES

    @pl.when(pl.program_id(1) == 0)
    def _():
        st_ref[...] = jnp.zeros_like(st_ref)

    lane = lax.broadcasted_iota(jnp.int32, (C, LANES), 1)
    first = lane < N
    row_c = lax.broadcasted_iota(jnp.int32, (C, C), 0)
    col_c = lax.broadcasted_iota(jnp.int32, (C, C), 1)
    tril_bf = (row_c >= col_c).astype(BF16)
    row2 = lax.broadcasted_iota(jnp.int32, (2 * C, 2 * C), 0)
    col2 = lax.broadcasted_iota(jnp.int32, (2 * C, 2 * C), 1)
    same_head = (row2 // C) == (col2 // C)
    strict_lower = same_head & (row2 % C > col2 % C)
    lower = same_head & (row2 % C >= col2 % C)
    eye2 = row2 == col2

    def head_sum(t):
        s0 = jnp.sum(jnp.where(first, t, 0.0), axis=-1, keepdims=True)
        s1 = jnp.sum(jnp.where(first, 0.0, t), axis=-1, keepdims=True)
        return jnp.where(first, s0, s1)

    def stack(t):
        return jnp.concatenate([jnp.where(first, t, 0.0), jnp.where(first, 0.0, t)], axis=0)

    lw1 = jnp.tanh(lo_ref[:, 0:pad]).astype(BF16)
    la1 = lo_ref[:, pad:2 * pad].astype(BF16)
    lg1 = _sigmoid(lo_ref[:, 2 * pad:]).astype(BF16)

    r_all = r_ref[...]
    k_all = k_ref[...]
    zw = w0_ref[...] + _dot(lw1, wlb_ref[...])
    softplus = jnp.maximum(-zw, 0.0) + jnp.log(1.0 + jnp.exp(-jnp.abs(zw)))
    log_decay = -jnp.exp(-softplus - 0.5)
    a_all = _sigmoid(a0_ref[...] + _dot(la1, alb_ref[...]))
    g_all = _dot(lg1, glb_ref[...])
    kk_all = k_all * kk_ref[...]
    k2_all = k_all * (1.0 + (a_all - 1.0) * ka_ref[...])
    cum_all = _cumsum_time(tril_bf, log_decay)
    total_all = cum_all[C - 1:C, :]
    p_incl = jnp.exp(cum_all)
    p_excl = jnp.exp(cum_all - log_decay)
    inv_p = jnp.exp(-cum_all)
    to_end = jnp.exp(total_all - cum_all)
    rt_all = r_all * p_incl
    rk2_all = r_all * k2_all * rk_ref[...]

    pairs = range(n_pairs)
    sls = [slice(p * LANES, (p + 1) * LANES) for p in pairs]

    lhs, rhs, v_sts, ends = [], [], [], []
    for p in pairs:
        sl = sls[p]
        kk = kk_all[:, sl]
        kk = kk / jnp.maximum(jnp.sqrt(head_sum(kk * kk)), 1e-12)
        kka = kk * a_all[:, sl]
        a_t = -kk * p_excl[:, sl]
        b_t = kka * inv_p[:, sl]
        k_t = k2_all[:, sl] * inv_p[:, sl]
        b_e = kka * to_end[:, sl]
        k_e = k2_all[:, sl] * to_end[:, sl]
        lhs.append(jnp.concatenate([stack(a_t), stack(rt_all[:, sl])], axis=0).astype(BF16))
        rhs.append(jnp.concatenate([stack(b_t), stack(k_t)], axis=0).astype(BF16))
        ends.append(jnp.concatenate([stack(b_e), stack(k_e)], axis=0).astype(BF16))
        v_sts.append(stack(v_ref[:, sl]).astype(BF16))

    grams = [_dot_nt(lhs[p], rhs[p]) for p in pairs]
    h2 = 2 * C
    a_ak = [jnp.where(strict_lower, grams[p][:h2, h2:], 0.0).astype(BF16) for p in pairs]
    a_rbk = [jnp.concatenate([jnp.where(lower, grams[p][h2:, :h2], 0.0),
                              jnp.where(lower, grams[p][h2:, h2:], 0.0)], axis=1).astype(BF16) for p in pairs]

    pw = [jnp.where(strict_lower, grams[p][:h2, :h2], 0.0) for p in pairs]
    t_inv = [jnp.where(eye2, 1.0, 0.0) + pw[p] for p in pairs]
    for _ in range(C.bit_length() - 2):
        pw_bf = [pw[p].astype(BF16) for p in pairs]
        pw = [_dot(pw_bf[p], pw_bf[p]) for p in pairs]
        t_inv = [t_inv[p] + _dot(t_inv[p].astype(BF16), pw[p].astype(BF16)) for p in pairs]

    m0 = [st_ref[p] for p in pairs]
    m0_bf = [m0[p].astype(BF16) for p in pairs]
    lm = [_dot(lhs[p], m0_bf[p]) for p in pairs]
    x_rhs = [lm[p][:h2] + _dot(a_ak[p], v_sts[p]) for p in pairs]
    u_bf = [_dot(t_inv[p].astype(BF16), x_rhs[p].astype(BF16)).astype(BF16) for p in pairs]
    uv = [jnp.concatenate([u_bf[p], v_sts[p]], axis=0) for p in pairs]
    y_st = [lm[p][h2:] + _dot(a_rbk[p], uv[p]) for p in pairs]

    for p in pairs:
        total_col = jnp.sum(jnp.where(eye2, total_all[:, sls[p]], 0.0), axis=-1, keepdims=True)
        st_ref[p] = jnp.exp(total_col) * m0[p] + _dot_tn(ends[p], uv[p])

    for p in pairs:
        sl = sls[p]
        y = y_st[p][:C] + y_st[p][C:]
        mu = head_sum(y) * (1.0 / N)
        yc = y - mu
        var = head_sum(yc * yc) * (1.0 / N)
        yn = yc * lax.rsqrt(var + RWKV_LN_EPS) * lnw_ref[:, sl] + lnb_ref[:, sl]
        bonus = head_sum(rk2_all[:, sl])
        o_ref[:, sl] = (yn + bonus * v_ref[:, sl]) * g_all[:, sl]


def _rwkv_mixer(x, norm_gain, mix, w_rkv, w0, w_la, w_lb, a0, a_la, a_lb, g_la, g_lb, k_k, k_a, r_k,
                ln_w, ln_b, w_o, *, batch, seq):
    tokens, d = x.shape
    C = RWKV_CHUNK
    pad = RWKV_LORA_PAD
    tm = _row_tile(seq, 512)
    tn = 512
    n_big = 3 * d // tn
    n_tiles = tokens // tm
    lora_w = w_la.shape[1], a_la.shape[1], g_la.shape[1]
    assert lora_w[0] <= pad and lora_w[1] <= pad and 2 * pad + lora_w[2] == tn and 2 * C == LANES
    w_cat = jnp.concatenate([w_rkv[0], w_rkv[1], w_rkv[2], _pad_cols(w_la, pad), _pad_cols(a_la, pad), g_la],
                            axis=1).astype(BF16)
    starts = (jnp.arange(n_tiles) * tm) % seq == 0
    xb = jnp.where(starts[:, None], 0.0, jnp.roll(x[tm - 1::tm], 1, axis=0)).reshape(n_tiles, 1, d)
    proj = pl.pallas_call(
        functools.partial(_rwkv_proj_kernel, n_big=n_big),
        grid=(n_tiles, n_big + 1),
        in_specs=[
            pl.BlockSpec((tm, d), lambda i, j: (i, 0)),
            pl.BlockSpec((1, 1, d), lambda i, j: (i, 0, 0)),
            pl.BlockSpec((1, d), lambda i, j: (0, 0)),
            pl.BlockSpec((6, d), lambda i, j: (0, 0)),
            pl.BlockSpec((d, tn), lambda i, j: (0, j)),
        ],
        out_specs=pl.BlockSpec((tm, tn), lambda i, j: (i, j)),
        out_shape=jax.ShapeDtypeStruct((tokens, 3 * d + tn), F32),
        scratch_shapes=[pltpu.VMEM((6, tm, d), BF16)],
        compiler_params=_compiler_params(("parallel", "arbitrary")),
        name="rwkv_proj",
    )(x, xb, norm_gain.reshape(1, d), mix, w_cat)

    nc = seq // C
    row1 = lambda t: t.reshape(1, d)
    pad_rows = lambda w: jnp.pad(w, ((0, pad - w.shape[0]), (0, 0))).astype(BF16)
    chunk = lambda blk: (lambda b, c: (b * nc + c, blk))
    const = lambda b, c: (0, 0)
    vec = pl.BlockSpec((1, d), const)
    mixed = pl.pallas_call(
        _rwkv_scan_kernel,
        grid=(batch, nc),
        in_specs=[
            pl.BlockSpec((C, d), chunk(0)),
            pl.BlockSpec((C, d), chunk(1)),
            pl.BlockSpec((C, d), chunk(2)),
            pl.BlockSpec((C, tn), chunk(3 * d // tn)),
            vec,
            pl.BlockSpec((pad, d), const),
            vec,
            pl.BlockSpec((pad, d), const),
            pl.BlockSpec((lora_w[2], d), const),
            vec, vec, vec, vec, vec,
        ],
        out_specs=pl.BlockSpec((C, d), chunk(0)),
        out_shape=jax.ShapeDtypeStruct((tokens, d), F32),
        scratch_shapes=[pltpu.VMEM((d // LANES, LANES, LANES), F32)],
        compiler_params=_compiler_params(("parallel", "arbitrary")),
        name="rwkv_scan",
    )(proj, proj, proj, proj, row1(w0), pad_rows(w_lb), row1(a0), pad_rows(a_lb), g_lb.astype(BF16),
      row1(k_k), row1(k_a), row1(r_k), row1(ln_w), row1(ln_b))
    return _proj(mixed, w_o.astype(BF16), seq=seq, residual=x)


def kernel(x, positions, ffn1_norm, ffn1_w_gu, ffn1_w_down, mixer_norm, ffn2_norm, ffn2_w_gu, ffn2_w_down,
           mlstm_w_in, mlstm_b_gate, mlstm_head_gain, mlstm_w_out,
           attn_w_qkv, attn_q_gain, attn_k_gain, attn_sinks, attn_w_o,
           rwkv_mix, rwkv_w_rkv, rwkv_w0, rwkv_w_lora_a, rwkv_w_lora_b, rwkv_a0, rwkv_a_lora_a, rwkv_a_lora_b,
           rwkv_g_lora_a, rwkv_g_lora_b, rwkv_k_k, rwkv_k_a, rwkv_r_k, rwkv_ln_w, rwkv_ln_b, rwkv_w_o):
    batch, seq, d = x.shape
    depth = ffn1_norm.shape[0]
    h = x.reshape(batch * seq, d)
    for i in range(depth):
        h = _ffn(h, ffn1_norm[i], ffn1_w_gu[i].astype(BF16), ffn1_w_down[i].astype(BF16), seq=seq)
        kind, j = i % 3, i // 3
        if kind == 0:
            h = _mlstm_mixer(h, mixer_norm[i], mlstm_w_in[j], mlstm_b_gate[j], mlstm_head_gain[j],
                             mlstm_w_out[j], batch=batch, seq=seq)
        elif kind == 1:
            h = _swa_mixer(h, positions, mixer_norm[i], attn_w_qkv[j], attn_q_gain[j], attn_k_gain[j],
                           attn_sinks[j], attn_w_o[j], batch=batch, seq=seq)
        else:
            h = _rwkv_mixer(h, mixer_norm[i], rwkv_mix[j], rwkv_w_rkv[j], rwkv_w0[j], rwkv_w_lora_a[j],
                            rwkv_w_lora_b[j], rwkv_a0[j], rwkv_a_lora_a[j], rwkv_a_lora_b[j], rwkv_g_lora_a[j],
                            rwkv_g_lora_b[j], rwkv_k_k[j], rwkv_k_a[j], rwkv_r_k[j], rwkv_ln_w[j], rwkv_ln_b[j],
                            rwkv_w_o[j], batch=batch, seq=seq)
        h = _ffn(h, ffn2_norm[i], ffn2_w_gu[i].astype(BF16), ffn2_w_down[i].astype(BF16), seq=seq)
    return h.reshape(batch, seq, d)
```

```python
import functools

import jax
import jax.numpy as jnp
from jax import lax
from jax.experimental import pallas as pl
from jax.experimental.pallas import tpu as pltpu

F32 = jnp.float32
BF16 = jnp.bfloat16

NORM_EPS = 1e-6

MLSTM_HEADS = 8
MLSTM_DQK = 128
MLSTM_DV = 256
MLSTM_CHUNK = 64

ATTN_HEAD_DIM = 64
ATTN_GROUP = 8
ATTN_WINDOW = 128
ROPE_DIM = 16
ROPE_THETA = 500000.0

RWKV_HEAD = 64
RWKV_CHUNK = 64
RWKV_LN_EPS = 64e-5
RWKV_LORA_PAD = 128

LANES = 128
VMEM_LIMIT_BYTES = 56 * 1024 * 1024


def _compiler_params(semantics):
    return pltpu.CompilerParams(dimension_semantics=semantics, vmem_limit_bytes=VMEM_LIMIT_BYTES)


def _row_tile(seq, preferred):
    tile = preferred
    while seq % tile:
        tile //= 2
    return tile


def _dot(a, b):
    return jnp.dot(a, b, preferred_element_type=F32)


def _dot_nt(a, b):
    return lax.dot_general(a, b, (((1,), (1,)), ((), ())), preferred_element_type=F32)


def _dot_tn(a, b):
    return lax.dot_general(a, b, (((0,), (0,)), ((), ())), preferred_element_type=F32)


def _sigmoid(x):
    return 1.0 / (1.0 + jnp.exp(-x))


def _rmsnorm_rows(x, gain):
    ms = jnp.mean(x * x, axis=-1, keepdims=True)
    return x * lax.rsqrt(ms + NORM_EPS) * gain


def _split3(x):
    hi = x.astype(BF16)
    r1 = x - hi.astype(F32)
    mid = r1.astype(BF16)
    lo = (r1 - mid.astype(F32)).astype(BF16)
    return hi, mid, lo


def _cumsum_time(tril_bf, x):
    hi, mid, lo = _split3(x)
    return _dot(tril_bf, hi) + _dot(tril_bf, mid) + _dot(tril_bf, lo)


def _ffn_kernel(x_ref, gain_ref, wg_ref, wu_ref, wd_ref, o_ref, xn_ref):
    j = pl.program_id(1)

    @pl.when(j == 0)
    def _():
        xn_ref[...] = _rmsnorm_rows(x_ref[...], gain_ref[...]).astype(BF16)
        o_ref[...] = jnp.zeros_like(o_ref)

    xn = xn_ref[...]
    gate = _dot(xn, wg_ref[...])
    up = _dot(xn, wu_ref[...])
    act = (gate * _sigmoid(gate) * up).astype(BF16)
    o_ref[...] += _dot(act, wd_ref[...])

    @pl.when(j == pl.num_programs(1) - 1)
    def _():
        o_ref[...] = x_ref[...] + 0.5 * o_ref[...]


def _ffn(x, gain, w_gu, w_down, *, seq):
    tokens, d = x.shape
    f = w_down.shape[0]
    tm = _row_tile(seq, 512)
    tf = _col_tile(f, 768)
    assert f % tf == 0
    nf = f // tf
    return pl.pallas_call(
        _ffn_kernel,
        grid=(tokens // tm, nf),
        in_specs=[
            pl.BlockSpec((tm, d), lambda i, j: (i, 0)),
            pl.BlockSpec((1, d), lambda i, j: (0, 0)),
            pl.BlockSpec((d, tf), lambda i, j: (0, j)),
            pl.BlockSpec((d, tf), lambda i, j: (0, j + nf)),
            pl.BlockSpec((tf, d), lambda i, j: (j, 0)),
        ],
        out_specs=pl.BlockSpec((tm, d), lambda i, j: (i, 0)),
        out_shape=jax.ShapeDtypeStruct((tokens, d), F32),
        scratch_shapes=[pltpu.VMEM((tm, d), BF16)],
        compiler_params=_compiler_params(("parallel", "arbitrary")),
        name="ffn",
    )(x, gain.reshape(1, d), w_gu, w_gu, w_down)


def _proj_kernel(*refs, norm, residual):
    refs = list(refs)
    x_ref = refs.pop(0)
    gain_ref = refs.pop(0) if norm else None
    w_ref = refs.pop(0)
    res_ref = refs.pop(0) if residual else None
    o_ref = refs.pop(0)

    if norm:
        xn_ref = refs.pop(0)

        @pl.when(pl.program_id(1) == 0)
        def _():
            xn_ref[...] = _rmsnorm_rows(x_ref[...], gain_ref[...]).astype(BF16)

        lhs = xn_ref[...]
    else:
        lhs = x_ref[...]

    acc = _dot(lhs, w_ref[...])
    if residual:
        acc = acc + res_ref[...]
    o_ref[...] = acc


def _col_tile(n, cap):
    best = LANES
    for t in range(LANES, cap + 1, LANES):
        if n % t == 0:
            best = t
    return best


def _proj(x, w, *, seq, gain=None, residual=None):
    tokens, k = x.shape
    n = w.shape[1]
    norm = gain is not None
    assert norm or x.dtype == BF16
    tm = _row_tile(seq, 1024 if norm else 512)
    tn = _col_tile(n, 1024 if norm else 2048)
    in_specs = [pl.BlockSpec((tm, k), lambda i, j: (i, 0))]
    args = [x]
    if norm:
        in_specs.append(pl.BlockSpec((1, k), lambda i, j: (0, 0)))
        args.append(gain.reshape(1, k))
    in_specs.append(pl.BlockSpec((k, tn), lambda i, j: (0, j)))
    args.append(w)
    if residual is not None:
        in_specs.append(pl.BlockSpec((tm, tn), lambda i, j: (i, j)))
        args.append(residual)
    return pl.pallas_call(
        functools.partial(_proj_kernel, norm=norm, residual=residual is not None),
        grid=(tokens // tm, n // tn),
        in_specs=in_specs,
        out_specs=pl.BlockSpec((tm, tn), lambda i, j: (i, j)),
        out_shape=jax.ShapeDtypeStruct((tokens, n), F32),
        scratch_shapes=[pltpu.VMEM((tm, k), BF16)] if norm else [],
        compiler_params=_compiler_params(("parallel", "arbitrary")),
        name="proj",
    )(*args)


def _pad_cols(w, width):
    return jnp.pad(w, ((0, 0), (0, width - w.shape[1])))


def _mlstm_kernel(q_ref, k_ref, v_ref, og_ref, g_ref, bias_ref, hg_ref, o_ref, ct_ref, n_ref, m_ref):
    L, H, DK, DV = MLSTM_CHUNK, MLSTM_HEADS, MLSTM_DQK, MLSTM_DV

    @pl.when(pl.program_id(1) == 0)
    def _():
        ct_ref[...] = jnp.zeros_like(ct_ref)
        n_ref[...] = jnp.zeros_like(n_ref)
        m_ref[...] = jnp.zeros_like(m_ref)

    row = lax.broadcasted_iota(jnp.int32, (L, L), 0)
    col = lax.broadcasted_iota(jnp.int32, (L, L), 1)
    causal = row >= col
    tril_bf = causal.astype(BF16)

    z = g_ref[...] + bias_ref[...]
    log_f = jnp.minimum(z, 0.0) - jnp.log(1.0 + jnp.exp(-jnp.abs(z)))
    bcum = _cumsum_time(tril_bf, log_f)
    z_t = z.T
    bcum_t = bcum.T

    heads = range(H)
    q = [q_ref[:, h * DK:(h + 1) * DK] for h in heads]
    k = [k_ref[:, h * DK:(h + 1) * DK] * (DK ** -0.5) for h in heads]
    q_bf = [t.astype(BF16) for t in q]
    k_bf = [t.astype(BF16) for t in k]
    v_bf = [v_ref[:, h * DV:(h + 1) * DV].astype(BF16) for h in heads]
    m_prev = [m_ref[h] for h in heads]

    b_col = [bcum[:, H + h:H + h + 1] for h in heads]
    g = [bcum[L - 1:L, H + h:H + h + 1] for h in heads]
    qk = [_dot_nt(q_bf[h], k_bf[h]) for h in heads]
    inter = [_dot(q_bf[h], ct_ref[h].astype(BF16)) for h in heads]
    qn = [jnp.sum(q[h] * n_ref[h], axis=-1, keepdims=True) for h in heads]

    log_d = [jnp.where(causal, b_col[h] - bcum_t[H + h:H + h + 1, :] + z_t[h:h + 1, :], -jnp.inf) for h in heads]
    row_max = [jnp.max(log_d[h], axis=-1, keepdims=True) for h in heads]
    log_w = [g[h] - b_col[h] + z[:, h:h + 1] for h in heads]
    m_new = [jnp.maximum(g[h] + m_prev[h], jnp.max(log_w[h], axis=0, keepdims=True)) for h in heads]
    kw = [k[h] * jnp.exp(log_w[h] - m_new[h]) for h in heads]
    outer = [_dot_tn(kw[h].astype(BF16), v_bf[h]) for h in heads]

    log_inter = [b_col[h] + m_prev[h] for h in heads]
    m_out = [jnp.maximum(log_inter[h], row_max[h]) for h in heads]
    s = [qk[h] * jnp.exp(log_d[h] - m_out[h]) for h in heads]
    intra = [_dot(s[h].astype(BF16), v_bf[h]) for h in heads]
    s_sum = [jnp.sum(s[h], axis=-1, keepdims=True) for h in heads]
    w_inter = [jnp.exp(log_inter[h] - m_out[h]) for h in heads]
    den = [s_sum[h] + w_inter[h] * qn[h] for h in heads]
    scale = [1.0 / jnp.maximum(jnp.abs(den[h]), jnp.exp(-m_out[h])) for h in heads]
    hout = [(intra[h] + w_inter[h] * inter[h]) * scale[h] for h in heads]
    ms = [jnp.mean(hout[h] * hout[h], axis=-1, keepdims=True) for h in heads]
    gate = [_sigmoid(og_ref[:, h * DV:(h + 1) * DV]) * hg_ref[:, h * DV:(h + 1) * DV] for h in heads]
    for h in heads:
        o_ref[:, h * DV:(h + 1) * DV] = (hout[h] * lax.rsqrt(ms[h] + NORM_EPS) * gate[h]).astype(o_ref.dtype)

    for h in heads:
        decay = jnp.exp(g[h] + m_prev[h] - m_new[h])
        ct_ref[h] = decay * ct_ref[h] + outer[h]
        n_ref[h] = decay * n_ref[h] + jnp.sum(kw[h], axis=0, keepdims=True)
        m_ref[h] = m_new[h]


def _mlstm_mixer(x, norm_gain, w_in, b_gate, head_gain, w_out, *, batch, seq):
    tokens, d = x.shape
    L, H, DK, DV = MLSTM_CHUNK, MLSTM_HEADS, MLSTM_DQK, MLSTM_DV
    nc = seq // L
    main = 2 * H * DK + 2 * H * DV
    w_pad = _pad_cols(w_in, main + 512).astype(BF16)
    zin = _proj(x, w_pad, seq=seq, gain=norm_gain)
    bias = jnp.pad(b_gate.reshape(1, 2 * H), ((0, 0), (0, LANES - 2 * H)))
    qk_w, v_w = H * DK, H * DV
    gated = pl.pallas_call(
        _mlstm_kernel,
        grid=(batch, nc),
        in_specs=[
            pl.BlockSpec((L, qk_w), lambda b, c: (b * nc + c, 0)),
            pl.BlockSpec((L, qk_w), lambda b, c: (b * nc + c, 1)),
            pl.BlockSpec((L, v_w), lambda b, c: (b * nc + c, 2 * qk_w // v_w)),
            pl.BlockSpec((L, v_w), lambda b, c: (b * nc + c, 2 * qk_w // v_w + 1)),
            pl.BlockSpec((L, LANES), lambda b, c: (b * nc + c, main // LANES)),
            pl.BlockSpec((1, LANES), lambda b, c: (0, 0)),
            pl.BlockSpec((1, v_w), lambda b, c: (0, 0)),
        ],
        out_specs=pl.BlockSpec((L, v_w), lambda b, c: (b * nc + c, 0)),
        out_shape=jax.ShapeDtypeStruct((tokens, v_w), BF16),
        scratch_shapes=[
            pltpu.VMEM((H, DK, DV), F32),
            pltpu.VMEM((H, 1, DK), F32),
            pltpu.VMEM((H, 1, 1), F32),
        ],
        compiler_params=_compiler_params(("parallel", "arbitrary")),
        name="mlstm_scan",
    )(zin, zin, zin, zin, zin, bias, head_gain.reshape(1, v_w))
    return _proj(gated, w_out.astype(BF16), seq=seq, residual=x)


def _qk_norm_rope(slabs, gain2, tables, lane):
    dh = ATTN_HEAD_DIM
    half = ROPE_DIM // 2
    first = lane < dh
    low = lane % dh < half
    idx = range(len(slabs))
    sq = [t * t for t in slabs]
    s0 = [jnp.sum(jnp.where(first, sq[i], 0.0), axis=-1, keepdims=True) for i in idx]
    s1 = [jnp.sum(jnp.where(first, 0.0, sq[i]), axis=-1, keepdims=True) for i in idx]
    tn = [slabs[i] * lax.rsqrt(jnp.where(first, s0[i], s1[i]) * (1.0 / dh) + NORM_EPS) * gain2 for i in idx]
    up = [pltpu.roll(tn[i], LANES - half, 1) for i in idx]
    down = [pltpu.roll(tn[i], half, 1) for i in idx]
    return [tn[i] * tables[i][0] + jnp.where(low, up[i], down[i]) * tables[i][1] for i in idx]


def _swa_kernel(q_ref, kc_ref, kp_ref, vc_ref, vp_ref, pc_ref, pp_ref, freq_ref, qg_ref, kg_ref, sink_ref, o_ref):
    blk, dh, grp = ATTN_WINDOW, ATTN_HEAD_DIM, ATTN_GROUP
    half = ROPE_DIM // 2
    n_kv = kc_ref.shape[1] // dh
    has_prev = pl.program_id(1) > 0

    lane = lax.broadcasted_iota(jnp.int32, (blk, LANES), 1)
    in_rot = lane % dh < ROPE_DIM

    def rope_tables(pos_ref):
        ang = pos_ref[...].astype(F32) * freq_ref[...]
        cos_t = jnp.where(in_rot, jnp.cos(ang), 1.0)
        sin = jnp.sin(ang)
        sin_t = jnp.where(in_rot, jnp.where(lane % dh < half, -sin, sin), 0.0)
        return cos_t, sin_t

    cos_c, sin_c = rope_tables(pc_ref)
    cos_p, sin_p = rope_tables(pp_ref)

    qi = lax.broadcasted_iota(jnp.int32, (blk, 2 * blk), 0) + blk
    kj = lax.broadcasted_iota(jnp.int32, (blk, 2 * blk), 1)
    first_key = jnp.where(has_prev, 0, blk)
    ok = (qi >= kj) & (qi - kj < ATTN_WINDOW) & (kj >= first_key)
    scale = dh ** -0.5
    assert LANES == 2 * dh and scale == 0.125
    first = lane < dh
    first2 = lax.broadcasted_iota(jnp.int32, (2 * blk, LANES), 1) < dh
    slabs_per_kv = grp * dh // LANES

    k_dup, v_split = [], []
    kv_slabs = n_kv * dh // LANES
    k_in = [kp_ref[:, c * LANES:(c + 1) * LANES] for c in range(kv_slabs)]
    k_in += [kc_ref[:, c * LANES:(c + 1) * LANES] for c in range(kv_slabs)]
    k_roped = _qk_norm_rope(k_in, kg_ref[...], [(cos_p, sin_p)] * kv_slabs + [(cos_c, sin_c)] * kv_slabs, lane)
    q_slabs = q_ref.shape[1] // LANES
    q_roped = _qk_norm_rope([q_ref[:, c * LANES:(c + 1) * LANES] for c in range(q_slabs)], qg_ref[...],
                            [(cos_c, sin_c)] * q_slabs, lane)
    for c in range(kv_slabs):
        sl = slice(c * LANES, (c + 1) * LANES)
        kcat = jnp.concatenate([k_roped[c], k_roped[kv_slabs + c]], axis=0)
        vcat = jnp.concatenate([vp_ref[:, sl], vc_ref[:, sl]], axis=0)
        k_rot = pltpu.roll(kcat, dh, 1)
        v_rot = pltpu.roll(vcat, dh, 1)
        k_dup.append(jnp.where(first2, kcat, k_rot).astype(BF16))
        k_dup.append(jnp.where(first2, k_rot, kcat).astype(BF16))
        v_split.append(jnp.concatenate([jnp.where(first2, vcat, 0.0), jnp.where(first2, 0.0, v_rot)],
                                       axis=0).astype(BF16))
        v_split.append(jnp.concatenate([jnp.where(first2, v_rot, 0.0), jnp.where(first2, 0.0, vcat)],
                                       axis=0).astype(BF16))

    for kv in range(n_kv):
        parts = []
        for c in range(kv * slabs_per_kv, (kv + 1) * slabs_per_kv):
            q2 = q_roped[c] * scale
            parts += [jnp.where(first, q2, 0.0), jnp.where(first, 0.0, q2)]
        q_st = jnp.concatenate(parts, axis=0).astype(BF16)
        s_all = _dot_nt(q_st, k_dup[kv])
        hbs = range(grp)
        sink = [sink_ref[:, kv * grp + hb:kv * grp + hb + 1] for hb in hbs]
        s = [jnp.where(ok, s_all[hb * blk:(hb + 1) * blk], -jnp.inf) for hb in hbs]
        m = [jnp.maximum(jnp.max(s[hb], axis=-1, keepdims=True), sink[hb]) for hb in hbs]
        p = [jnp.exp(s[hb] - m[hb]) for hb in hbs]
        p_sum = [jnp.sum(p[hb], axis=-1, keepdims=True) for hb in hbs]
        rinv = [1.0 / (p_sum[hb] + jnp.exp(sink[hb] - m[hb])) for hb in hbs]
        probs = [p[hb].astype(BF16) for hb in hbs]
        for ci in range(slabs_per_kv):
            c = kv * slabs_per_kv + ci
            p_cat = jnp.concatenate([probs[2 * ci], probs[2 * ci + 1]], axis=1)
            out = _dot(p_cat, v_split[kv])
            o_ref[:, c * LANES:(c + 1) * LANES] = (
                out * jnp.where(first, rinv[2 * ci], rinv[2 * ci + 1])).astype(o_ref.dtype)


def _swa_mixer(x, positions, norm_gain, w_qkv, q_gain, k_gain, sinks, w_o, *, batch, seq):
    tokens, d = x.shape
    blk, dh = ATTN_WINDOW, ATTN_HEAD_DIM
    nb = seq // blk
    n_q = sinks.shape[0]
    q_w = n_q * dh
    kv_w = (w_qkv.shape[1] - q_w) // 2
    qkv = _proj(x, w_qkv.astype(BF16), seq=seq, gain=norm_gain)
    pos = positions.reshape(tokens, 1)
    inv_freq = ROPE_THETA ** (-jnp.arange(0, ROPE_DIM, 2, dtype=F32) / ROPE_DIM)
    freq_head = jnp.concatenate([inv_freq, inv_freq, jnp.zeros((dh - ROPE_DIM,), F32)])
    freq = jnp.tile(freq_head, LANES // dh).reshape(1, LANES)
    tile2 = lambda g: jnp.tile(g, LANES // dh).reshape(1, LANES)
    sink_row = jnp.pad(sinks.reshape(1, n_q), ((0, 0), (0, LANES - n_q)))
    cur = lambda b, i: b * nb + i
    prev = lambda b, i: b * nb + jnp.maximum(i - 1, 0)
    k_blk, v_blk = q_w // kv_w, q_w // kv_w + 1
    small = lambda b, i: (0, 0)
    attn = pl.pallas_call(
        _swa_kernel,
        grid=(batch, nb),
        in_specs=[
            pl.BlockSpec((blk, q_w), lambda b, i: (cur(b, i), 0)),
            pl.BlockSpec((blk, kv_w), lambda b, i: (cur(b, i), k_blk)),
            pl.BlockSpec((blk, kv_w), lambda b, i: (prev(b, i), k_blk)),
            pl.BlockSpec((blk, kv_w), lambda b, i: (cur(b, i), v_blk)),
            pl.BlockSpec((blk, kv_w), lambda b, i: (prev(b, i), v_blk)),
            pl.BlockSpec((blk, 1), lambda b, i: (cur(b, i), 0)),
            pl.BlockSpec((blk, 1), lambda b, i: (prev(b, i), 0)),
            pl.BlockSpec((1, LANES), small),
            pl.BlockSpec((1, LANES), small),
            pl.BlockSpec((1, LANES), small),
            pl.BlockSpec((1, LANES), small),
        ],
        out_specs=pl.BlockSpec((blk, q_w), lambda b, i: (cur(b, i), 0)),
        out_shape=jax.ShapeDtypeStruct((tokens, q_w), BF16),
        compiler_params=_compiler_params(("parallel", "arbitrary")),
        name="swa",
    )(qkv, qkv, qkv, qkv, qkv, pos, pos, freq, tile2(q_gain), tile2(k_gain), sink_row)
    return _proj(attn, w_o.astype(BF16), seq=seq, residual=x)


def _rwkv_proj_kernel(x_ref, xb_ref, gain_ref, mix_ref, w_ref, o_ref, xm_ref, *, n_big):
    j = pl.program_id(1)
    tm = x_ref.shape[0]
    tn = o_ref.shape[1]

    @pl.when(j == 0)
    def _():
        hn = _rmsnorm_rows(x_ref[...], gain_ref[...])
        hb = _rmsnorm_rows(xb_ref[0], gain_ref[...])
        row = lax.broadcasted_iota(jnp.int32, hn.shape, 0)
        dx = jnp.where(row == 0, hb, pltpu.roll(hn, 1, 0)) - hn
        for i in range(6):
            xm_ref[i] = (hn + dx * mix_ref[i:i + 1, :]).astype(BF16)

    per_proj = n_big // 3

    @pl.when(j < n_big)
    def _():
        src = jnp.where(j < per_proj, 0, jnp.where(j < 2 * per_proj, 2, 3))
        o_ref[...] = _dot(xm_ref[src], w_ref[...])

    @pl.when(j == n_big)
    def _():
        pad = RWKV_LORA_PAD
        o_ref[:, 0:pad] = _dot(xm_ref[1], w_ref[:, 0:pad])
        o_ref[:, pad:2 * pad] = _dot(xm_ref[4], w_ref[:, pad:2 * pad])
        o_ref[:, 2 * pad:tn] = _dot(xm_ref[5], w_ref[:, 2 * pad:tn])


def _rwkv_scan_kernel(r_ref, k_ref, v_ref, lo_ref, w0_ref, wlb_ref, a0_ref, alb_ref, glb_ref,
                      kk_ref, ka_ref, rk_ref, lnw_ref, lnb_ref, o_ref, st_ref):
    C, N = RWKV_CHUNK, RWKV_HEAD
    pad = RWKV_LORA_PAD
    n_pairs = r_ref.shape[1] // LANES

    @pl.when(pl.program_id(1) == 0)
    def _():
        st_ref[...] = jnp.zeros_like(st_ref)

    lane = lax.broadcasted_iota(jnp.int32, (C, LANES), 1)
    first = lane < N
    row_c = lax.broadcasted_iota(jnp.int32, (C, C), 0)
    col_c = lax.broadcasted_iota(jnp.int32, (C, C), 1)
    tril_bf = (row_c >= col_c).astype(BF16)
    row2 = lax.broadcasted_iota(jnp.int32, (2 * C, 2 * C), 0)
    col2 = lax.broadcasted_iota(jnp.int32, (2 * C, 2 * C), 1)
    same_head = (row2 // C) == (col2 // C)
    strict_lower = same_head & (row2 % C > col2 % C)
    lower = same_head & (row2 % C >= col2 % C)
    eye2 = row2 == col2

    def head_sum(t):
        s0 = jnp.sum(jnp.where(first, t, 0.0), axis=-1, keepdims=True)
        s1 = jnp.sum(jnp.where(first, 0.0, t), axis=-1, keepdims=True)
        return jnp.where(first, s0, s1)

    def stack(t):
        return jnp.concatenate([jnp.where(first, t, 0.0), jnp.where(first, 0.0, t)], axis=0)

    lw1 = jnp.tanh(lo_ref[:, 0:pad]).astype(BF16)
    la1 = lo_ref[:, pad:2 * pad].astype(BF16)
    lg1 = _sigmoid(lo_ref[:, 2 * pad:]).astype(BF16)

    r_all = r_ref[...]
    k_all = k_ref[...]
    zw = w0_ref[...] + _dot(lw1, wlb_ref[...])
    softplus = jnp.maximum(-zw, 0.0) + jnp.log(1.0 + jnp.exp(-jnp.abs(zw)))
    log_decay = -jnp.exp(-softplus - 0.5)
    a_all = _sigmoid(a0_ref[...] + _dot(la1, alb_ref[...]))
    g_all = _dot(lg1, glb_ref[...])
    kk_all = k_all * kk_ref[...]
    k2_all = k_all * (1.0 + (a_all - 1.0) * ka_ref[...])
    cum_all = _cumsum_time(tril_bf, log_decay)
    total_all = cum_all[C - 1:C, :]
    p_incl = jnp.exp(cum_all)
    p_excl = jnp.exp(cum_all - log_decay)
    inv_p = jnp.exp(-cum_all)
    to_end = jnp.exp(total_all - cum_all)
    rt_all = r_all * p_incl
    rk2_all = r_all * k2_all * rk_ref[...]

    pairs = range(n_pairs)
    sls = [slice(p * LANES, (p + 1) * LANES) for p in pairs]

    lhs, rhs, v_sts, ends = [], [], [], []
    for p in pairs:
        sl = sls[p]
        kk = kk_all[:, sl]
        kk = kk / jnp.maximum(jnp.sqrt(head_sum(kk * kk)), 1e-12)
        kka = kk * a_all[:, sl]
        a_t = -kk * p_excl[:, sl]
        b_t = kka * inv_p[:, sl]
        k_t = k2_all[:, sl] * inv_p[:, sl]
        b_e = kka * to_end[:, sl]
        k_e = k2_all[:, sl] * to_end[:, sl]
        lhs.append(jnp.concatenate([stack(a_t), stack(rt_all[:, sl])], axis=0).astype(BF16))
        rhs.append(jnp.concatenate([stack(b_t), stack(k_t)], axis=0).astype(BF16))
        ends.append(jnp.concatenate([stack(b_e), stack(k_e)], axis=0).astype(BF16))
        v_sts.append(stack(v_ref[:, sl]).astype(BF16))

    grams = [_dot_nt(lhs[p], rhs[p]) for p in pairs]
    h2 = 2 * C
    a_ak = [jnp.where(strict_lower, grams[p][:h2, h2:], 0.0).astype(BF16) for p in pairs]
    a_rbk = [jnp.concatenate([jnp.where(lower, grams[p][h2:, :h2], 0.0),
                              jnp.where(lower, grams[p][h2:, h2:], 0.0)], axis=1).astype(BF16) for p in pairs]

    pw = [jnp.where(strict_lower, grams[p][:h2, :h2], 0.0) for p in pairs]
    t_inv = [jnp.where(eye2, 1.0, 0.0) + pw[p] for p in pairs]
    for _ in range(C.bit_length() - 2):
        pw_bf = [pw[p].astype(BF16) for p in pairs]
        pw = [_dot(pw_bf[p], pw_bf[p]) for p in pairs]
        t_inv = [t_inv[p] + _dot(t_inv[p].astype(BF16), pw[p].astype(BF16)) for p in pairs]

    m0 = [st_ref[p] for p in pairs]
    m0_bf = [m0[p].astype(BF16) for p in pairs]
    lm = [_dot(lhs[p], m0_bf[p]) for p in pairs]
    x_rhs = [lm[p][:h2] + _dot(a_ak[p], v_sts[p]) for p in pairs]
    u_bf = [_dot(t_inv[p].astype(BF16), x_rhs[p].astype(BF16)).astype(BF16) for p in pairs]
    uv = [jnp.concatenate([u_bf[p], v_sts[p]], axis=0) for p in pairs]
    y_st = [lm[p][h2:] + _dot(a_rbk[p], uv[p]) for p in pairs]

    for p in pairs:
        total_col = jnp.sum(jnp.where(eye2, total_all[:, sls[p]], 0.0), axis=-1, keepdims=True)
        st_ref[p] = jnp.exp(total_col) * m0[p] + _dot_tn(ends[p], uv[p])

    for p in pairs:
        sl = sls[p]
        y = y_st[p][:C] + y_st[p][C:]
        mu = head_sum(y) * (1.0 / N)
        yc = y - mu
        var = head_sum(yc * yc) * (1.0 / N)
        yn = yc * lax.rsqrt(var + RWKV_LN_EPS) * lnw_ref[:, sl] + lnb_ref[:, sl]
        bonus = head_sum(rk2_all[:, sl])
        o_ref[:, sl] = ((yn + bonus * v_ref[:, sl]) * g_all[:, sl]).astype(o_ref.dtype)


def _rwkv_mixer(x, norm_gain, mix, w_rkv, w0, w_la, w_lb, a0, a_la, a_lb, g_la, g_lb, k_k, k_a, r_k,
                ln_w, ln_b, w_o, *, batch, seq):
    tokens, d = x.shape
    C = RWKV_CHUNK
    pad = RWKV_LORA_PAD
    tm = _row_tile(seq, 512)
    tn = 512
    n_big = 3 * d // tn
    n_tiles = tokens // tm
    lora_w = w_la.shape[1], a_la.shape[1], g_la.shape[1]
    assert lora_w[0] <= pad and lora_w[1] <= pad and 2 * pad + lora_w[2] == tn and 2 * C == LANES
    w_cat = jnp.concatenate([w_rkv[0], w_rkv[1], w_rkv[2], _pad_cols(w_la, pad), _pad_cols(a_la, pad), g_la],
                            axis=1).astype(BF16)
    starts = (jnp.arange(n_tiles) * tm) % seq == 0
    xb = jnp.where(starts[:, None], 0.0, jnp.roll(x[tm - 1::tm], 1, axis=0)).reshape(n_tiles, 1, d)
    proj = pl.pallas_call(
        functools.partial(_rwkv_proj_kernel, n_big=n_big),
        grid=(n_tiles, n_big + 1),
        in_specs=[
            pl.BlockSpec((tm, d), lambda i, j: (i, 0)),
            pl.BlockSpec((1, 1, d), lambda i, j: (i, 0, 0)),
            pl.BlockSpec((1, d), lambda i, j: (0, 0)),
            pl.BlockSpec((6, d), lambda i, j: (0, 0)),
            pl.BlockSpec((d, tn), lambda i, j: (0, j)),
        ],
        out_specs=pl.BlockSpec((tm, tn), lambda i, j: (i, j)),
        out_shape=jax.ShapeDtypeStruct((tokens, 3 * d + tn), F32),
        scratch_shapes=[pltpu.VMEM((6, tm, d), BF16)],
        compiler_params=_compiler_params(("parallel", "arbitrary")),
        name="rwkv_proj",
    )(x, xb, norm_gain.reshape(1, d), mix, w_cat)

    nc = seq // C
    row1 = lambda t: t.reshape(1, d)
    pad_rows = lambda w: jnp.pad(w, ((0, pad - w.shape[0]), (0, 0))).astype(BF16)
    chunk = lambda blk: (lambda b, c: (b * nc + c, blk))
    const = lambda b, c: (0, 0)
    vec = pl.BlockSpec((1, d), const)
    mixed = pl.pallas_call(
        _rwkv_scan_kernel,
        grid=(batch, nc),
        in_specs=[
            pl.BlockSpec((C, d), chunk(0)),
            pl.BlockSpec((C, d), chunk(1)),
            pl.BlockSpec((C, d), chunk(2)),
            pl.BlockSpec((C, tn), chunk(3 * d // tn)),
            vec,
            pl.BlockSpec((pad, d), const),
            vec,
            pl.BlockSpec((pad, d), const),
            pl.BlockSpec((lora_w[2], d), const),
            vec, vec, vec, vec, vec,
        ],
        out_specs=pl.BlockSpec((C, d), chunk(0)),
        out_shape=jax.ShapeDtypeStruct((tokens, d), BF16),
        scratch_shapes=[pltpu.VMEM((d // LANES, LANES, LANES), F32)],
        compiler_params=_compiler_params(("parallel", "arbitrary")),
        name="rwkv_scan",
    )(proj, proj, proj, proj, row1(w0), pad_rows(w_lb), row1(a0), pad_rows(a_lb), g_lb.astype(BF16),
      row1(k_k), row1(k_a), row1(r_k), row1(ln_w), row1(ln_b))
    return _proj(mixed, w_o.astype(BF16), seq=seq, residual=x)


def kernel(x, positions, ffn1_norm, ffn1_w_gu, ffn1_w_down, mixer_norm, ffn2_norm, ffn2_w_gu, ffn2_w_down,
           mlstm_w_in, mlstm_b_gate, mlstm_head_gain, mlstm_w_out,
           attn_w_qkv, attn_q_gain, attn_k_gain, attn_sinks, attn_w_o,
           rwkv_mix, rwkv_w_rkv, rwkv_w0, rwkv_w_lora_a, rwkv_w_lora_b, rwkv_a0, rwkv_a_lora_a, rwkv_a_lora_b,
           rwkv_g_lora_a, rwkv_g_lora_b, rwkv_k_k, rwkv_k_a, rwkv_r_k, rwkv_ln_w, rwkv_ln_b, rwkv_w_o):
    batch, seq, d = x.shape
    depth = ffn1_norm.shape[0]
    h = x.reshape(batch * seq, d)
    for i in range(depth):
        h = _ffn(h, ffn1_norm[i], ffn1_w_gu[i].astype(BF16), ffn1_w_down[i].astype(BF16), seq=seq)
        kind, j = i % 3, i // 3
        if kind == 0:
            h = _mlstm_mixer(h, mixer_norm[i], mlstm_w_in[j], mlstm_b_gate[j], mlstm_head_gain[j],
                             mlstm_w_out[j], batch=batch, seq=seq)
        elif kind == 1:
            h = _swa_mixer(h, positions, mixer_norm[i], attn_w_qkv[j], attn_q_gain[j], attn_k_gain[j],
                           attn_sinks[j], attn_w_o[j], batch=batch, seq=seq)
        else:
            h = _rwkv_mixer(h, mixer_norm[i], rwkv_mix[j], rwkv_w_rkv[j], rwkv_w0[j], rwkv_w_lora_a[j],
                            rwkv_w_lora_b[j], rwkv_a0[j], rwkv_a_lora_a[j], rwkv_a_lora_b[j], rwkv_g_lora_a[j],
                            rwkv_g_lora_b[j], rwkv_k_k[j], rwkv_k_a[j], rwkv_r_k[j], rwkv_ln_w[j], rwkv_ln_b[j],
                            rwkv_w_o[j], batch=batch, seq=seq)
        h = _ffn(h, ffn2_norm[i], ffn2_w_gu[i].astype(BF16), ffn2_w_down[i].astype(BF16), seq=seq)
    return h.reshape(batch, seq, d)
```

```python
import functools

import jax
import jax.numpy as jnp
from jax import lax
from jax.experimental import pallas as pl
from jax.experimental.pallas import tpu as pltpu

F32 = jnp.float32
BF16 = jnp.bfloat16

NORM_EPS = 1e-6

MLSTM_HEADS = 8
MLSTM_DQK = 128
MLSTM_DV = 256
MLSTM_CHUNK = 64

ATTN_HEAD_DIM = 64
ATTN_GROUP = 8
ATTN_WINDOW = 128
ROPE_DIM = 16
ROPE_THETA = 500000.0

RWKV_HEAD = 64
RWKV_CHUNK = 64
RWKV_LN_EPS = 64e-5
RWKV_LORA_PAD = 128

LANES = 128
VMEM_LIMIT_BYTES = 56 * 1024 * 1024


def _compiler_params(semantics):
    return pltpu.CompilerParams(dimension_semantics=semantics, vmem_limit_bytes=VMEM_LIMIT_BYTES)


def _row_tile(seq, preferred):
    tile = preferred
    while seq % tile:
        tile //= 2
    return tile


def _dot(a, b):
    return jnp.dot(a, b, preferred_element_type=F32)


def _dot_nt(a, b):
    return lax.dot_general(a, b, (((1,), (1,)), ((), ())), preferred_element_type=F32)


def _dot_tn(a, b):
    return lax.dot_general(a, b, (((0,), (0,)), ((), ())), preferred_element_type=F32)


def _sigmoid(x):
    return 1.0 / (1.0 + jnp.exp(-x))


def _rmsnorm_rows(x, gain):
    ms = jnp.mean(x * x, axis=-1, keepdims=True)
    return x * lax.rsqrt(ms + NORM_EPS) * gain


def _split3(x):
    hi = x.astype(BF16)
    r1 = x - hi.astype(F32)
    mid = r1.astype(BF16)
    lo = (r1 - mid.astype(F32)).astype(BF16)
    return hi, mid, lo


def _cumsum_time(tril_bf, x):
    hi, mid, lo = _split3(x)
    return _dot(tril_bf, hi) + _dot(tril_bf, mid) + _dot(tril_bf, lo)


def _ffn_kernel(x_ref, gain_ref, wgu_ref, wd_ref, o_ref, xn_ref):
    tf = wd_ref.shape[0]

    @pl.when(pl.program_id(1) == 0)
    def _():
        x = x_ref[...]
        xn_ref[...] = _rmsnorm_rows(x, gain_ref[...]).astype(BF16)
        o_ref[...] = x

    gu = _dot(xn_ref[...], wgu_ref[0])
    gate = gu[:, :tf]
    act = (gate * _sigmoid(gate) * (0.5 * gu[:, tf:])).astype(BF16)
    o_ref[...] += _dot(act, wd_ref[...])


def _ffn(x, gain, w_gu, w_down, *, seq):
    tokens, d = x.shape
    f = w_down.shape[0]
    tm = _row_tile(seq, 1024)
    tf = _col_tile(f, 768)
    nf = f // tf
    w_blocks = w_gu.reshape(d, 2, nf, tf).transpose(2, 0, 1, 3).reshape(nf, d, 2 * tf)
    return pl.pallas_call(
        _ffn_kernel,
        grid=(tokens // tm, nf),
        in_specs=[
            pl.BlockSpec((tm, d), lambda i, j: (i, 0)),
            pl.BlockSpec((1, d), lambda i, j: (0, 0)),
            pl.BlockSpec((1, d, 2 * tf), lambda i, j: (j, 0, 0)),
            pl.BlockSpec((tf, d), lambda i, j: (j, 0)),
        ],
        out_specs=pl.BlockSpec((tm, d), lambda i, j: (i, 0)),
        out_shape=jax.ShapeDtypeStruct((tokens, d), F32),
        scratch_shapes=[pltpu.VMEM((tm, d), BF16)],
        compiler_params=_compiler_params(("parallel", "arbitrary")),
        name="ffn",
    )(x, gain.reshape(1, d), w_blocks, w_down)


def _proj_kernel(*refs, norm, residual):
    refs = list(refs)
    x_ref = refs.pop(0)
    gain_ref = refs.pop(0) if norm else None
    w_ref = refs.pop(0)
    res_ref = refs.pop(0) if residual else None
    o_ref = refs.pop(0)

    if norm:
        xn_ref = refs.pop(0)

        @pl.when(pl.program_id(1) == 0)
        def _():
            xn_ref[...] = _rmsnorm_rows(x_ref[...], gain_ref[...]).astype(BF16)

        lhs = xn_ref[...]
    else:
        lhs = x_ref[...]

    acc = _dot(lhs, w_ref[...])
    if residual:
        acc = acc + res_ref[...]
    o_ref[...] = acc


def _col_tile(n, cap):
    best = LANES
    for t in range(LANES, cap + 1, LANES):
        if n % t == 0:
            best = t
    return best


def _proj(x, w, *, seq, gain=None, residual=None):
    tokens, k = x.shape
    n = w.shape[1]
    norm = gain is not None
    assert norm or x.dtype == BF16
    tm = _row_tile(seq, 1024 if norm else 512)
    tn = _col_tile(n, 1024 if norm else 2048)
    in_specs = [pl.BlockSpec((tm, k), lambda i, j: (i, 0))]
    args = [x]
    if norm:
        in_specs.append(pl.BlockSpec((1, k), lambda i, j: (0, 0)))
        args.append(gain.reshape(1, k))
    in_specs.append(pl.BlockSpec((k, tn), lambda i, j: (0, j)))
    args.append(w)
    if residual is not None:
        in_specs.append(pl.BlockSpec((tm, tn), lambda i, j: (i, j)))
        args.append(residual)
    return pl.pallas_call(
        functools.partial(_proj_kernel, norm=norm, residual=residual is not None),
        grid=(tokens // tm, n // tn),
        in_specs=in_specs,
        out_specs=pl.BlockSpec((tm, tn), lambda i, j: (i, j)),
        out_shape=jax.ShapeDtypeStruct((tokens, n), F32),
        scratch_shapes=[pltpu.VMEM((tm, k), BF16)] if norm else [],
        compiler_params=_compiler_params(("parallel", "arbitrary")),
        name="proj",
    )(*args)


def _pad_cols(w, width):
    return jnp.pad(w, ((0, 0), (0, width - w.shape[1])))


def _mlstm_kernel(q_ref, k_ref, v_ref, og_ref, g_ref, bias_ref, hg_ref, o_ref, ct_ref, n_ref, m_ref):
    L, H, DK, DV = MLSTM_CHUNK, MLSTM_HEADS, MLSTM_DQK, MLSTM_DV

    @pl.when(pl.program_id(1) == 0)
    def _():
        ct_ref[...] = jnp.zeros_like(ct_ref)
        n_ref[...] = jnp.zeros_like(n_ref)
        m_ref[...] = jnp.zeros_like(m_ref)

    row = lax.broadcasted_iota(jnp.int32, (L, L), 0)
    col = lax.broadcasted_iota(jnp.int32, (L, L), 1)
    causal = row >= col
    tril_bf = causal.astype(BF16)

    z = g_ref[...] + bias_ref[...]
    log_f = jnp.minimum(z, 0.0) - jnp.log(1.0 + jnp.exp(-jnp.abs(z)))
    bcum = _cumsum_time(tril_bf, log_f)
    z_t = z.T
    bcum_t = bcum.T

    heads = range(H)
    q = [q_ref[:, h * DK:(h + 1) * DK] for h in heads]
    k = [k_ref[:, h * DK:(h + 1) * DK] * (DK ** -0.5) for h in heads]
    q_bf = [t.astype(BF16) for t in q]
    k_bf = [t.astype(BF16) for t in k]
    v_bf = [v_ref[:, h * DV:(h + 1) * DV].astype(BF16) for h in heads]
    m_prev = [m_ref[h] for h in heads]

    b_col = [bcum[:, H + h:H + h + 1] for h in heads]
    g = [bcum[L - 1:L, H + h:H + h + 1] for h in heads]
    qk = [_dot_nt(q_bf[h], k_bf[h]) for h in heads]
    inter = [_dot(q_bf[h], ct_ref[h].astype(BF16)) for h in heads]
    qn = [jnp.sum(q[h] * n_ref[h], axis=-1, keepdims=True) for h in heads]

    log_d = [jnp.where(causal, b_col[h] - bcum_t[H + h:H + h + 1, :] + z_t[h:h + 1, :], -jnp.inf) for h in heads]
    row_max = [jnp.max(log_d[h], axis=-1, keepdims=True) for h in heads]
    log_w = [g[h] - b_col[h] + z[:, h:h + 1] for h in heads]
    m_new = [jnp.maximum(g[h] + m_prev[h], jnp.max(log_w[h], axis=0, keepdims=True)) for h in heads]
    kw = [k[h] * jnp.exp(log_w[h] - m_new[h]) for h in heads]
    outer = [_dot_tn(kw[h].astype(BF16), v_bf[h]) for h in heads]

    log_inter = [b_col[h] + m_prev[h] for h in heads]
    m_out = [jnp.maximum(log_inter[h], row_max[h]) for h in heads]
    s = [qk[h] * jnp.exp(log_d[h] - m_out[h]) for h in heads]
    intra = [_dot(s[h].astype(BF16), v_bf[h]) for h in heads]
    s_sum = [jnp.sum(s[h], axis=-1, keepdims=True) for h in heads]
    w_inter = [jnp.exp(log_inter[h] - m_out[h]) for h in heads]
    den = [s_sum[h] + w_inter[h] * qn[h] for h in heads]
    scale = [1.0 / jnp.maximum(jnp.abs(den[h]), jnp.exp(-m_out[h])) for h in heads]
    hout = [(intra[h] + w_inter[h] * inter[h]) * scale[h] for h in heads]
    ms = [jnp.mean(hout[h] * hout[h], axis=-1, keepdims=True) for h in heads]
    gate = [_sigmoid(og_ref[:, h * DV:(h + 1) * DV]) * hg_ref[:, h * DV:(h + 1) * DV] for h in heads]
    for h in heads:
        o_ref[:, h * DV:(h + 1) * DV] = (hout[h] * lax.rsqrt(ms[h] + NORM_EPS) * gate[h]).astype(o_ref.dtype)

    for h in heads:
        decay = jnp.exp(g[h] + m_prev[h] - m_new[h])
        ct_ref[h] = decay * ct_ref[h] + outer[h]
        n_ref[h] = decay * n_ref[h] + jnp.sum(kw[h], axis=0, keepdims=True)
        m_ref[h] = m_new[h]


def _mlstm_mixer(x, norm_gain, w_in, b_gate, head_gain, w_out, *, batch, seq):
    tokens, d = x.shape
    L, H, DK, DV = MLSTM_CHUNK, MLSTM_HEADS, MLSTM_DQK, MLSTM_DV
    nc = seq // L
    main = 2 * H * DK + 2 * H * DV
    w_pad = _pad_cols(w_in, main + 512).astype(BF16)
    zin = _proj(x, w_pad, seq=seq, gain=norm_gain)
    bias = jnp.pad(b_gate.reshape(1, 2 * H), ((0, 0), (0, LANES - 2 * H)))
    qk_w, v_w = H * DK, H * DV
    gated = pl.pallas_call(
        _mlstm_kernel,
        grid=(batch, nc),
        in_specs=[
            pl.BlockSpec((L, qk_w), lambda b, c: (b * nc + c, 0)),
            pl.BlockSpec((L, qk_w), lambda b, c: (b * nc + c, 1)),
            pl.BlockSpec((L, v_w), lambda b, c: (b * nc + c, 2 * qk_w // v_w)),
            pl.BlockSpec((L, v_w), lambda b, c: (b * nc + c, 2 * qk_w // v_w + 1)),
            pl.BlockSpec((L, LANES), lambda b, c: (b * nc + c, main // LANES)),
            pl.BlockSpec((1, LANES), lambda b, c: (0, 0)),
            pl.BlockSpec((1, v_w), lambda b, c: (0, 0)),
        ],
        out_specs=pl.BlockSpec((L, v_w), lambda b, c: (b * nc + c, 0)),
        out_shape=jax.ShapeDtypeStruct((tokens, v_w), BF16),
        scratch_shapes=[
            pltpu.VMEM((H, DK, DV), F32),
            pltpu.VMEM((H, 1, DK), F32),
            pltpu.VMEM((H, 1, 1), F32),
        ],
        compiler_params=_compiler_params(("parallel", "arbitrary")),
        name="mlstm_scan",
    )(zin, zin, zin, zin, zin, bias, head_gain.reshape(1, v_w))
    return _proj(gated, w_out.astype(BF16), seq=seq, residual=x)


def _qk_norm_rope(slabs, gain2, tables, lane):
    dh = ATTN_HEAD_DIM
    half = ROPE_DIM // 2
    first = lane < dh
    low = lane % dh < half
    idx = range(len(slabs))
    sq = [t * t for t in slabs]
    s0 = [jnp.sum(jnp.where(first, sq[i], 0.0), axis=-1, keepdims=True) for i in idx]
    s1 = [jnp.sum(jnp.where(first, 0.0, sq[i]), axis=-1, keepdims=True) for i in idx]
    tn = [slabs[i] * lax.rsqrt(jnp.where(first, s0[i], s1[i]) * (1.0 / dh) + NORM_EPS) * gain2 for i in idx]
    up = [pltpu.roll(tn[i], LANES - half, 1) for i in idx]
    down = [pltpu.roll(tn[i], half, 1) for i in idx]
    return [tn[i] * tables[i][0] + jnp.where(low, up[i], down[i]) * tables[i][1] for i in idx]


def _swa_kernel(q_ref, kc_ref, kp_ref, vc_ref, vp_ref, pc_ref, pp_ref, freq_ref, qg_ref, kg_ref, sink_ref, o_ref):
    blk, dh, grp = ATTN_WINDOW, ATTN_HEAD_DIM, ATTN_GROUP
    half = ROPE_DIM // 2
    n_kv = kc_ref.shape[1] // dh
    has_prev = pl.program_id(1) > 0

    lane = lax.broadcasted_iota(jnp.int32, (blk, LANES), 1)
    in_rot = lane % dh < ROPE_DIM

    def rope_tables(pos_ref):
        ang = pos_ref[...].astype(F32) * freq_ref[...]
        cos_t = jnp.where(in_rot, jnp.cos(ang), 1.0)
        sin = jnp.sin(ang)
        sin_t = jnp.where(in_rot, jnp.where(lane % dh < half, -sin, sin), 0.0)
        return cos_t, sin_t

    cos_c, sin_c = rope_tables(pc_ref)
    cos_p, sin_p = rope_tables(pp_ref)

    qi = lax.broadcasted_iota(jnp.int32, (blk, 2 * blk), 0) + blk
    kj = lax.broadcasted_iota(jnp.int32, (blk, 2 * blk), 1)
    first_key = jnp.where(has_prev, 0, blk)
    ok = (qi >= kj) & (qi - kj < ATTN_WINDOW) & (kj >= first_key)
    scale = dh ** -0.5
    assert LANES == 2 * dh and scale == 0.125
    first = lane < dh
    first2 = lax.broadcasted_iota(jnp.int32, (2 * blk, LANES), 1) < dh
    slabs_per_kv = grp * dh // LANES

    k_dup, v_split = [], []
    kv_slabs = n_kv * dh // LANES
    k_in = [kp_ref[:, c * LANES:(c + 1) * LANES] for c in range(kv_slabs)]
    k_in += [kc_ref[:, c * LANES:(c + 1) * LANES] for c in range(kv_slabs)]
    k_roped = _qk_norm_rope(k_in, kg_ref[...], [(cos_p, sin_p)] * kv_slabs + [(cos_c, sin_c)] * kv_slabs, lane)
    q_slabs = q_ref.shape[1] // LANES
    q_roped = _qk_norm_rope([q_ref[:, c * LANES:(c + 1) * LANES] for c in range(q_slabs)], qg_ref[...],
                            [(cos_c, sin_c)] * q_slabs, lane)
    for c in range(kv_slabs):
        sl = slice(c * LANES, (c + 1) * LANES)
        kcat = jnp.concatenate([k_roped[c], k_roped[kv_slabs + c]], axis=0)
        vcat = jnp.concatenate([vp_ref[:, sl], vc_ref[:, sl]], axis=0)
        k_rot = pltpu.roll(kcat, dh, 1)
        v_rot = pltpu.roll(vcat, dh, 1)
        k_dup.append(jnp.where(first2, kcat, k_rot).astype(BF16))
        k_dup.append(jnp.where(first2, k_rot, kcat).astype(BF16))
        v_split.append(jnp.concatenate([jnp.where(first2, vcat, 0.0), jnp.where(first2, 0.0, v_rot)],
                                       axis=0).astype(BF16))
        v_split.append(jnp.concatenate([jnp.where(first2, v_rot, 0.0), jnp.where(first2, 0.0, vcat)],
                                       axis=0).astype(BF16))

    for kv in range(n_kv):
        parts = []
        for c in range(kv * slabs_per_kv, (kv + 1) * slabs_per_kv):
            q2 = q_roped[c] * scale
            parts += [jnp.where(first, q2, 0.0), jnp.where(first, 0.0, q2)]
        q_st = jnp.concatenate(parts, axis=0).astype(BF16)
        s_all = _dot_nt(q_st, k_dup[kv])
        hbs = range(grp)
        sink = [sink_ref[:, kv * grp + hb:kv * grp + hb + 1] for hb in hbs]
        s = [jnp.where(ok, s_all[hb * blk:(hb + 1) * blk], -jnp.inf) for hb in hbs]
        m = [jnp.maximum(jnp.max(s[hb], axis=-1, keepdims=True), sink[hb]) for hb in hbs]
        p = [jnp.exp(s[hb] - m[hb]) for hb in hbs]
        p_sum = [jnp.sum(p[hb], axis=-1, keepdims=True) for hb in hbs]
        rinv = [1.0 / (p_sum[hb] + jnp.exp(sink[hb] - m[hb])) for hb in hbs]
        probs = [p[hb].astype(BF16) for hb in hbs]
        for ci in range(slabs_per_kv):
            c = kv * slabs_per_kv + ci
            p_cat = jnp.concatenate([probs[2 * ci], probs[2 * ci + 1]], axis=1)
            out = _dot(p_cat, v_split[kv])
            o_ref[:, c * LANES:(c + 1) * LANES] = (
                out * jnp.where(first, rinv[2 * ci], rinv[2 * ci + 1])).astype(o_ref.dtype)


def _swa_mixer(x, positions, norm_gain, w_qkv, q_gain, k_gain, sinks, w_o, *, batch, seq):
    tokens, d = x.shape
    blk, dh = ATTN_WINDOW, ATTN_HEAD_DIM
    nb = seq // blk
    n_q = sinks.shape[0]
    q_w = n_q * dh
    kv_w = (w_qkv.shape[1] - q_w) // 2
    qkv = _proj(x, w_qkv.astype(BF16), seq=seq, gain=norm_gain)
    pos = positions.reshape(tokens, 1)
    inv_freq = ROPE_THETA ** (-jnp.arange(0, ROPE_DIM, 2, dtype=F32) / ROPE_DIM)
    freq_head = jnp.concatenate([inv_freq, inv_freq, jnp.zeros((dh - ROPE_DIM,), F32)])
    freq = jnp.tile(freq_head, LANES // dh).reshape(1, LANES)
    tile2 = lambda g: jnp.tile(g, LANES // dh).reshape(1, LANES)
    sink_row = jnp.pad(sinks.reshape(1, n_q), ((0, 0), (0, LANES - n_q)))
    cur = lambda b, i: b * nb + i
    prev = lambda b, i: b * nb + jnp.maximum(i - 1, 0)
    k_blk, v_blk = q_w // kv_w, q_w // kv_w + 1
    small = lambda b, i: (0, 0)
    attn = pl.pallas_call(
        _swa_kernel,
        grid=(batch, nb),
        in_specs=[
            pl.BlockSpec((blk, q_w), lambda b, i: (cur(b, i), 0)),
            pl.BlockSpec((blk, kv_w), lambda b, i: (cur(b, i), k_blk)),
            pl.BlockSpec((blk, kv_w), lambda b, i: (prev(b, i), k_blk)),
            pl.BlockSpec((blk, kv_w), lambda b, i: (cur(b, i), v_blk)),
            pl.BlockSpec((blk, kv_w), lambda b, i: (prev(b, i), v_blk)),
            pl.BlockSpec((blk, 1), lambda b, i: (cur(b, i), 0)),
            pl.BlockSpec((blk, 1), lambda b, i: (prev(b, i), 0)),
            pl.BlockSpec((1, LANES), small),
            pl.BlockSpec((1, LANES), small),
            pl.BlockSpec((1, LANES), small),
            pl.BlockSpec((1, LANES), small),
        ],
        out_specs=pl.BlockSpec((blk, q_w), lambda b, i: (cur(b, i), 0)),
        out_shape=jax.ShapeDtypeStruct((tokens, q_w), BF16),
        compiler_params=_compiler_params(("parallel", "arbitrary")),
        name="swa",
    )(qkv, qkv, qkv, qkv, qkv, pos, pos, freq, tile2(q_gain), tile2(k_gain), sink_row)
    return _proj(attn, w_o.astype(BF16), seq=seq, residual=x)


def _rwkv_proj_kernel(x_ref, xb_ref, gain_ref, mix_ref, w_ref, o_ref, xm_ref, *, n_big):
    j = pl.program_id(1)
    tm = x_ref.shape[0]
    tn = o_ref.shape[1]

    @pl.when(j == 0)
    def _():
        hn = _rmsnorm_rows(x_ref[...], gain_ref[...])
        hb = _rmsnorm_rows(xb_ref[0], gain_ref[...])
        row = lax.broadcasted_iota(jnp.int32, hn.shape, 0)
        dx = jnp.where(row == 0, hb, pltpu.roll(hn, 1, 0)) - hn
        for i in range(6):
            xm_ref[i] = (hn + dx * mix_ref[i:i + 1, :]).astype(BF16)

    per_proj = n_big // 3

    @pl.when(j < n_big)
    def _():
        src = jnp.where(j < per_proj, 0, jnp.where(j < 2 * per_proj, 2, 3))
        o_ref[...] = _dot(xm_ref[src], w_ref[...])

    @pl.when(j == n_big)
    def _():
        pad = RWKV_LORA_PAD
        o_ref[:, 0:pad] = _dot(xm_ref[1], w_ref[:, 0:pad])
        o_ref[:, pad:2 * pad] = _dot(xm_ref[4], w_ref[:, pad:2 * pad])
        o_ref[:, 2 * pad:tn] = _dot(xm_ref[5], w_ref[:, 2 * pad:tn])


def _rwkv_scan_kernel(r_ref, k_ref, v_ref, lo_ref, w0_ref, wlb_ref, a0_ref, alb_ref, glb_ref,
                      kk_ref, ka_ref, rk_ref, lnw_ref, lnb_ref, o_ref, st_ref):
    C, N = RWKV_CHUNK, RWKV_HEAD
    pad = RWKV_LORA_PAD
    n_pairs = r_ref.shape[1] // LANES

    @pl.when(pl.program_id(1) == 0)
    def _():
        st_ref[...] = jnp.zeros_like(st_ref)

    lane = lax.broadcasted_iota(jnp.int32, (C, LANES), 1)
    first = lane < N
    row_c = lax.broadcasted_iota(jnp.int32, (C, C), 0)
    col_c = lax.broadcasted_iota(jnp.int32, (C, C), 1)
    tril_bf = (row_c >= col_c).astype(BF16)
    row2 = lax.broadcasted_iota(jnp.int32, (2 * C, 2 * C), 0)
    col2 = lax.broadcasted_iota(jnp.int32, (2 * C, 2 * C), 1)
    same_head = (row2 // C) == (col2 // C)
    strict_lower = same_head & (row2 % C > col2 % C)
    lower = same_head & (row2 % C >= col2 % C)
    eye2 = row2 == col2

    def head_sum(t):
        s0 = jnp.sum(jnp.where(first, t, 0.0), axis=-1, keepdims=True)
        s1 = jnp.sum(jnp.where(first, 0.0, t), axis=-1, keepdims=True)
        return jnp.where(first, s0, s1)

    def stack(t):
        return jnp.concatenate([jnp.where(first, t, 0.0), jnp.where(first, 0.0, t)], axis=0)

    lw1 = jnp.tanh(lo_ref[:, 0:pad]).astype(BF16)
    la1 = lo_ref[:, pad:2 * pad].astype(BF16)
    lg1 = _sigmoid(lo_ref[:, 2 * pad:]).astype(BF16)

    r_all = r_ref[...]
    k_all = k_ref[...]
    zw = w0_ref[...] + _dot(lw1, wlb_ref[...])
    softplus = jnp.maximum(-zw, 0.0) + jnp.log(1.0 + jnp.exp(-jnp.abs(zw)))
    log_decay = -jnp.exp(-softplus - 0.5)
    a_all = _sigmoid(a0_ref[...] + _dot(la1, alb_ref[...]))
    g_all = _dot(lg1, glb_ref[...])
    kk_all = k_all * kk_ref[...]
    k2_all = k_all * (1.0 + (a_all - 1.0) * ka_ref[...])
    cum_all = _cumsum_time(tril_bf, log_decay)
    total_all = cum_all[C - 1:C, :]
    p_incl = jnp.exp(cum_all)
    p_excl = jnp.exp(cum_all - log_decay)
    inv_p = jnp.exp(-cum_all)
    to_end = jnp.exp(total_all - cum_all)
    rt_all = r_all * p_incl
    rk2_all = r_all * k2_all * rk_ref[...]

    pairs = range(n_pairs)
    sls = [slice(p * LANES, (p + 1) * LANES) for p in pairs]

    lhs, rhs, v_sts, ends = [], [], [], []
    for p in pairs:
        sl = sls[p]
        kk = kk_all[:, sl]
        kk = kk / jnp.maximum(jnp.sqrt(head_sum(kk * kk)), 1e-12)
        kka = kk * a_all[:, sl]
        a_t = -kk * p_excl[:, sl]
        b_t = kka * inv_p[:, sl]
        k_t = k2_all[:, sl] * inv_p[:, sl]
        b_e = kka * to_end[:, sl]
        k_e = k2_all[:, sl] * to_end[:, sl]
        lhs.append(jnp.concatenate([stack(a_t), stack(rt_all[:, sl])], axis=0).astype(BF16))
        rhs.append(jnp.concatenate([stack(b_t), stack(k_t)], axis=0).astype(BF16))
        ends.append(jnp.concatenate([stack(b_e), stack(k_e)], axis=0).astype(BF16))
        v_sts.append(stack(v_ref[:, sl]).astype(BF16))

    grams = [_dot_nt(lhs[p], rhs[p]) for p in pairs]
    h2 = 2 * C
    a_ak = [jnp.where(strict_lower, grams[p][:h2, h2:], 0.0).astype(BF16) for p in pairs]
    a_rbk = [jnp.concatenate([jnp.where(lower, grams[p][h2:, :h2], 0.0),
                              jnp.where(lower, grams[p][h2:, h2:], 0.0)], axis=1).astype(BF16) for p in pairs]

    pw = [jnp.where(strict_lower, grams[p][:h2, :h2], 0.0) for p in pairs]
    t_inv = [jnp.where(eye2, 1.0, 0.0) + pw[p] for p in pairs]
    for _ in range(C.bit_length() - 2):
        pw_bf = [pw[p].astype(BF16) for p in pairs]
        pw = [_dot(pw_bf[p], pw_bf[p]) for p in pairs]
        t_inv = [t_inv[p] + _dot(t_inv[p].astype(BF16), pw[p].astype(BF16)) for p in pairs]

    m0 = [st_ref[p] for p in pairs]
    m0_bf = [m0[p].astype(BF16) for p in pairs]
    lm = [_dot(lhs[p], m0_bf[p]) for p in pairs]
    x_rhs = [lm[p][:h2] + _dot(a_ak[p], v_sts[p]) for p in pairs]
    u_bf = [_dot(t_inv[p].astype(BF16), x_rhs[p].astype(BF16)).astype(BF16) for p in pairs]
    uv = [jnp.concatenate([u_bf[p], v_sts[p]], axis=0) for p in pairs]
    y_st = [lm[p][h2:] + _dot(a_rbk[p], uv[p]) for p in pairs]

    for p in pairs:
        total_col = jnp.sum(jnp.where(eye2, total_all[:, sls[p]], 0.0), axis=-1, keepdims=True)
        st_ref[p] = jnp.exp(total_col) * m0[p] + _dot_tn(ends[p], uv[p])

    for p in pairs:
        sl = sls[p]
        y = y_st[p][:C] + y_st[p][C:]
        mu = head_sum(y) * (1.0 / N)
        yc = y - mu
        var = head_sum(yc * yc) * (1.0 / N)
        yn = yc * lax.rsqrt(var + RWKV_LN_EPS) * lnw_ref[:, sl] + lnb_ref[:, sl]
        bonus = head_sum(rk2_all[:, sl])
        o_ref[:, sl] = ((yn + bonus * v_ref[:, sl]) * g_all[:, sl]).astype(o_ref.dtype)


def _rwkv_mixer(x, norm_gain, mix, w_rkv, w0, w_la, w_lb, a0, a_la, a_lb, g_la, g_lb, k_k, k_a, r_k,
                ln_w, ln_b, w_o, *, batch, seq):
    tokens, d = x.shape
    C = RWKV_CHUNK
    pad = RWKV_LORA_PAD
    tm = _row_tile(seq, 512)
    tn = 512
    n_big = 3 * d // tn
    n_tiles = tokens // tm
    lora_w = w_la.shape[1], a_la.shape[1], g_la.shape[1]
    assert lora_w[0] <= pad and lora_w[1] <= pad and 2 * pad + lora_w[2] == tn and 2 * C == LANES
    w_cat = jnp.concatenate([w_rkv[0], w_rkv[1], w_rkv[2], _pad_cols(w_la, pad), _pad_cols(a_la, pad), g_la],
                            axis=1).astype(BF16)
    starts = (jnp.arange(n_tiles) * tm) % seq == 0
    xb = jnp.where(starts[:, None], 0.0, jnp.roll(x[tm - 1::tm], 1, axis=0)).reshape(n_tiles, 1, d)
    proj = pl.pallas_call(
        functools.partial(_rwkv_proj_kernel, n_big=n_big),
        grid=(n_tiles, n_big + 1),
        in_specs=[
            pl.BlockSpec((tm, d), lambda i, j: (i, 0)),
            pl.BlockSpec((1, 1, d), lambda i, j: (i, 0, 0)),
            pl.BlockSpec((1, d), lambda i, j: (0, 0)),
            pl.BlockSpec((6, d), lambda i, j: (0, 0)),
            pl.BlockSpec((d, tn), lambda i, j: (0, j)),
        ],
        out_specs=pl.BlockSpec((tm, tn), lambda i, j: (i, j)),
        out_shape=jax.ShapeDtypeStruct((tokens, 3 * d + tn), F32),
        scratch_shapes=[pltpu.VMEM((6, tm, d), BF16)],
        compiler_params=_compiler_params(("parallel", "arbitrary")),
        name="rwkv_proj",
    )(x, xb, norm_gain.reshape(1, d), mix, w_cat)

    nc = seq // C
    row1 = lambda t: t.reshape(1, d)
    pad_rows = lambda w: jnp.pad(w, ((0, pad - w.shape[0]), (0, 0))).astype(BF16)
    chunk = lambda blk: (lambda b, c: (b * nc + c, blk))
    const = lambda b, c: (0, 0)
    vec = pl.BlockSpec((1, d), const)
    mixed = pl.pallas_call(
        _rwkv_scan_kernel,
        grid=(batch, nc),
        in_specs=[
            pl.BlockSpec((C, d), chunk(0)),
            pl.BlockSpec((C, d), chunk(1)),
            pl.BlockSpec((C, d), chunk(2)),
            pl.BlockSpec((C, tn), chunk(3 * d // tn)),
            vec,
            pl.BlockSpec((pad, d), const),
            vec,
            pl.BlockSpec((pad, d), const),
            pl.BlockSpec((lora_w[2], d), const),
            vec, vec, vec, vec, vec,
        ],
        out_specs=pl.BlockSpec((C, d), chunk(0)),
        out_shape=jax.ShapeDtypeStruct((tokens, d), BF16),
        scratch_shapes=[pltpu.VMEM((d // LANES, LANES, LANES), F32)],
        compiler_params=_compiler_params(("parallel", "arbitrary")),
        name="rwkv_scan",
    )(proj, proj, proj, proj, row1(w0), pad_rows(w_lb), row1(a0), pad_rows(a_lb), g_lb.astype(BF16),
      row1(k_k), row1(k_a), row1(r_k), row1(ln_w), row1(ln_b))
    return _proj(mixed, w_o.astype(BF16), seq=seq, residual=x)


def kernel(x, positions, ffn1_norm, ffn1_w_gu, ffn1_w_down, mixer_norm, ffn2_norm, ffn2_w_gu, ffn2_w_down,
           mlstm_w_in, mlstm_b_gate, mlstm_head_gain, mlstm_w_out,
           attn_w_qkv, attn_q_gain, attn_k_gain, attn_sinks, attn_w_o,
           rwkv_mix, rwkv_w_rkv, rwkv_w0, rwkv_w_lora_a, rwkv_w_lora_b, rwkv_a0, rwkv_a_lora_a, rwkv_a_lora_b,
           rwkv_g_lora_a, rwkv_g_lora_b, rwkv_k_k, rwkv_k_a, rwkv_r_k, rwkv_ln_w, rwkv_ln_b, rwkv_w_o):
    batch, seq, d = x.shape
    depth = ffn1_norm.shape[0]
    h = x.reshape(batch * seq, d)
    for i in range(depth):
        h = _ffn(h, ffn1_norm[i], ffn1_w_gu[i].astype(BF16), ffn1_w_down[i].astype(BF16), seq=seq)
        kind, j = i % 3, i // 3
        if kind == 0:
            h = _mlstm_mixer(h, mixer_norm[i], mlstm_w_in[j], mlstm_b_gate[j], mlstm_head_gain[j],
                             mlstm_w_out[j], batch=batch, seq=seq)
        elif kind == 1:
            h = _swa_mixer(h, positions, mixer_norm[i], attn_w_qkv[j], attn_q_gain[j], attn_k_gain[j],
                           attn_sinks[j], attn_w_o[j], batch=batch, seq=seq)
        else:
            h = _rwkv_mixer(h, mixer_norm[i], rwkv_mix[j], rwkv_w_rkv[j], rwkv_w0[j], rwkv_w_lora_a[j],
                            rwkv_w_lora_b[j], rwkv_a0[j], rwkv_a_lora_a[j], rwkv_a_lora_b[j], rwkv_g_lora_a[j],
                            rwkv_g_lora_b[j], rwkv_k_k[j], rwkv_k_a[j], rwkv_r_k[j], rwkv_ln_w[j], rwkv_ln_b[j],
                            rwkv_w_o[j], batch=batch, seq=seq)
        h = _ffn(h, ffn2_norm[i], ffn2_w_gu[i].astype(BF16), ffn2_w_down[i].astype(BF16), seq=seq)
    return h.reshape(batch, seq, d)
```

```python
import functools

import jax
import jax.numpy as jnp
from jax import lax
from jax.experimental import pallas as pl
from jax.experimental.pallas import tpu as pltpu

F32 = jnp.float32
BF16 = jnp.bfloat16

NORM_EPS = 1e-6

MLSTM_HEADS = 8
MLSTM_DQK = 128
MLSTM_DV = 256
MLSTM_CHUNK = 64

ATTN_HEAD_DIM = 64
ATTN_GROUP = 8
ATTN_WINDOW = 128
ROPE_DIM = 16
ROPE_THETA = 500000.0

RWKV_HEAD = 64
RWKV_CHUNK = 64
RWKV_LN_EPS = 64e-5
RWKV_LORA_PAD = 128

LANES = 128
VMEM_LIMIT_BYTES = 56 * 1024 * 1024


def _compiler_params(semantics):
    return pltpu.CompilerParams(dimension_semantics=semantics, vmem_limit_bytes=VMEM_LIMIT_BYTES)


def _row_tile(seq, preferred):
    tile = preferred
    while seq % tile:
        tile //= 2
    return tile


def _dot(a, b):
    return jnp.dot(a, b, preferred_element_type=F32)


def _dot_nt(a, b):
    return lax.dot_general(a, b, (((1,), (1,)), ((), ())), preferred_element_type=F32)


def _dot_tn(a, b):
    return lax.dot_general(a, b, (((0,), (0,)), ((), ())), preferred_element_type=F32)


def _sigmoid(x):
    return 1.0 / (1.0 + jnp.exp(-x))


def _rmsnorm_rows(x, gain):
    ms = jnp.mean(x * x, axis=-1, keepdims=True)
    return x * lax.rsqrt(ms + NORM_EPS) * gain


def _split3(x):
    hi = x.astype(BF16)
    r1 = x - hi.astype(F32)
    mid = r1.astype(BF16)
    lo = (r1 - mid.astype(F32)).astype(BF16)
    return hi, mid, lo


def _cumsum_time(tril_bf, x):
    hi, mid, lo = _split3(x)
    return _dot(tril_bf, hi) + _dot(tril_bf, mid) + _dot(tril_bf, lo)


def _ffn_kernel(x_ref, gain_ref, wg_ref, wu_ref, wd_ref, o_ref, xn_ref):
    @pl.when(pl.program_id(1) == 0)
    def _():
        x = x_ref[...]
        xn_ref[...] = _rmsnorm_rows(x, gain_ref[...]).astype(BF16)
        o_ref[...] = x

    xn = xn_ref[...]
    gate = _dot(xn, wg_ref[...])
    up = _dot(xn, wu_ref[...])
    act = (gate * _sigmoid(gate) * (0.5 * up)).astype(BF16)
    o_ref[...] += _dot(act, wd_ref[...])


def _ffn(x, gain, w_gu, w_down, *, seq):
    tokens, d = x.shape
    f = w_down.shape[0]
    tm = _row_tile(seq, 1024)
    tf = _col_tile(f, 768)
    nf = f // tf
    return pl.pallas_call(
        _ffn_kernel,
        grid=(tokens // tm, nf),
        in_specs=[
            pl.BlockSpec((tm, d), lambda i, j: (i, 0)),
            pl.BlockSpec((1, d), lambda i, j: (0, 0)),
            pl.BlockSpec((d, tf), lambda i, j: (0, j)),
            pl.BlockSpec((d, tf), lambda i, j: (0, j + nf)),
            pl.BlockSpec((tf, d), lambda i, j: (j, 0)),
        ],
        out_specs=pl.BlockSpec((tm, d), lambda i, j: (i, 0)),
        out_shape=jax.ShapeDtypeStruct((tokens, d), F32),
        scratch_shapes=[pltpu.VMEM((tm, d), BF16)],
        compiler_params=_compiler_params(("parallel", "arbitrary")),
        name="ffn",
    )(x, gain.reshape(1, d), w_gu, w_gu, w_down)


def _proj_kernel(*refs, norm, residual):
    refs = list(refs)
    x_ref = refs.pop(0)
    gain_ref = refs.pop(0) if norm else None
    w_ref = refs.pop(0)
    res_ref = refs.pop(0) if residual else None
    o_ref = refs.pop(0)

    if norm:
        xn_ref = refs.pop(0)

        @pl.when(pl.program_id(1) == 0)
        def _():
            xn_ref[...] = _rmsnorm_rows(x_ref[...], gain_ref[...]).astype(BF16)

        lhs = xn_ref[...]
    else:
        lhs = x_ref[...]

    acc = _dot(lhs, w_ref[...])
    if residual:
        acc = acc + res_ref[...]
    o_ref[...] = acc


def _col_tile(n, cap):
    best = LANES
    for t in range(LANES, cap + 1, LANES):
        if n % t == 0:
            best = t
    return best


def _proj(x, w, *, seq, gain=None, residual=None):
    tokens, k = x.shape
    n = w.shape[1]
    norm = gain is not None
    assert norm or x.dtype == BF16
    tm = _row_tile(seq, 1024 if norm else 512)
    tn = _col_tile(n, 1024 if norm else 2048)
    in_specs = [pl.BlockSpec((tm, k), lambda i, j: (i, 0))]
    args = [x]
    if norm:
        in_specs.append(pl.BlockSpec((1, k), lambda i, j: (0, 0)))
        args.append(gain.reshape(1, k))
    in_specs.append(pl.BlockSpec((k, tn), lambda i, j: (0, j)))
    args.append(w)
    if residual is not None:
        in_specs.append(pl.BlockSpec((tm, tn), lambda i, j: (i, j)))
        args.append(residual)
    return pl.pallas_call(
        functools.partial(_proj_kernel, norm=norm, residual=residual is not None),
        grid=(tokens // tm, n // tn),
        in_specs=in_specs,
        out_specs=pl.BlockSpec((tm, tn), lambda i, j: (i, j)),
        out_shape=jax.ShapeDtypeStruct((tokens, n), F32),
        scratch_shapes=[pltpu.VMEM((tm, k), BF16)] if norm else [],
        compiler_params=_compiler_params(("parallel", "arbitrary")),
        name="proj",
    )(*args)


def _pad_cols(w, width):
    return jnp.pad(w, ((0, 0), (0, width - w.shape[1])))


def _mlstm_kernel(q_ref, k_ref, v_ref, og_ref, g_ref, bias_ref, hg_ref, o_ref, ct_ref, n_ref, m_ref):
    L, H, DK, DV = MLSTM_CHUNK, MLSTM_HEADS, MLSTM_DQK, MLSTM_DV

    @pl.when(pl.program_id(1) == 0)
    def _():
        ct_ref[...] = jnp.zeros_like(ct_ref)
        n_ref[...] = jnp.zeros_like(n_ref)
        m_ref[...] = jnp.zeros_like(m_ref)

    row = lax.broadcasted_iota(jnp.int32, (L, L), 0)
    col = lax.broadcasted_iota(jnp.int32, (L, L), 1)
    causal = row >= col
    tril_bf = causal.astype(BF16)

    z = g_ref[...] + bias_ref[...]
    log_f = jnp.minimum(z, 0.0) - jnp.log(1.0 + jnp.exp(-jnp.abs(z)))
    bcum = _cumsum_time(tril_bf, log_f)
    z_t = z.T
    bcum_t = bcum.T

    heads = range(H)
    q = [q_ref[:, h * DK:(h + 1) * DK] for h in heads]
    k = [k_ref[:, h * DK:(h + 1) * DK] * (DK ** -0.5) for h in heads]
    q_bf = [t.astype(BF16) for t in q]
    k_bf = [t.astype(BF16) for t in k]
    v_bf = [v_ref[:, h * DV:(h + 1) * DV].astype(BF16) for h in heads]
    m_prev = [m_ref[h] for h in heads]

    b_col = [bcum[:, H + h:H + h + 1] for h in heads]
    g = [bcum[L - 1:L, H + h:H + h + 1] for h in heads]
    qk = [_dot_nt(q_bf[h], k_bf[h]) for h in heads]
    inter = [_dot(q_bf[h], ct_ref[h].astype(BF16)) for h in heads]
    qn = [jnp.sum(q[h] * n_ref[h], axis=-1, keepdims=True) for h in heads]

    log_d = [jnp.where(causal, b_col[h] - bcum_t[H + h:H + h + 1, :] + z_t[h:h + 1, :], -jnp.inf) for h in heads]
    row_max = [jnp.max(log_d[h], axis=-1, keepdims=True) for h in heads]
    log_w = [g[h] - b_col[h] + z[:, h:h + 1] for h in heads]
    m_new = [jnp.maximum(g[h] + m_prev[h], jnp.max(log_w[h], axis=0, keepdims=True)) for h in heads]
    kw = [k[h] * jnp.exp(log_w[h] - m_new[h]) for h in heads]
    outer = [_dot_tn(kw[h].astype(BF16), v_bf[h]) for h in heads]

    log_inter = [b_col[h] + m_prev[h] for h in heads]
    m_out = [jnp.maximum(log_inter[h], row_max[h]) for h in heads]
    s = [qk[h] * jnp.exp(log_d[h] - m_out[h]) for h in heads]
    intra = [_dot(s[h].astype(BF16), v_bf[h]) for h in heads]
    s_sum = [jnp.sum(s[h], axis=-1, keepdims=True) for h in heads]
    w_inter = [jnp.exp(log_inter[h] - m_out[h]) for h in heads]
    den = [s_sum[h] + w_inter[h] * qn[h] for h in heads]
    scale = [1.0 / jnp.maximum(jnp.abs(den[h]), jnp.exp(-m_out[h])) for h in heads]
    hout = [(intra[h] + w_inter[h] * inter[h]) * scale[h] for h in heads]
    ms = [jnp.mean(hout[h] * hout[h], axis=-1, keepdims=True) for h in heads]
    gate = [_sigmoid(og_ref[:, h * DV:(h + 1) * DV]) * hg_ref[:, h * DV:(h + 1) * DV] for h in heads]
    for h in heads:
        o_ref[:, h * DV:(h + 1) * DV] = (hout[h] * lax.rsqrt(ms[h] + NORM_EPS) * gate[h]).astype(o_ref.dtype)

    for h in heads:
        decay = jnp.exp(g[h] + m_prev[h] - m_new[h])
        ct_ref[h] = decay * ct_ref[h] + outer[h]
        n_ref[h] = decay * n_ref[h] + jnp.sum(kw[h], axis=0, keepdims=True)
        m_ref[h] = m_new[h]


def _mlstm_mixer(x, norm_gain, w_in, b_gate, head_gain, w_out, *, batch, seq):
    tokens, d = x.shape
    L, H, DK, DV = MLSTM_CHUNK, MLSTM_HEADS, MLSTM_DQK, MLSTM_DV
    nc = seq // L
    main = 2 * H * DK + 2 * H * DV
    w_pad = _pad_cols(w_in, main + 512).astype(BF16)
    zin = _proj(x, w_pad, seq=seq, gain=norm_gain)
    bias = jnp.pad(b_gate.reshape(1, 2 * H), ((0, 0), (0, LANES - 2 * H)))
    qk_w, v_w = H * DK, H * DV
    gated = pl.pallas_call(
        _mlstm_kernel,
        grid=(batch, nc),
        in_specs=[
            pl.BlockSpec((L, qk_w), lambda b, c: (b * nc + c, 0)),
            pl.BlockSpec((L, qk_w), lambda b, c: (b * nc + c, 1)),
            pl.BlockSpec((L, v_w), lambda b, c: (b * nc + c, 2 * qk_w // v_w)),
            pl.BlockSpec((L, v_w), lambda b, c: (b * nc + c, 2 * qk_w // v_w + 1)),
            pl.BlockSpec((L, LANES), lambda b, c: (b * nc + c, main // LANES)),
            pl.BlockSpec((1, LANES), lambda b, c: (0, 0)),
            pl.BlockSpec((1, v_w), lambda b, c: (0, 0)),
        ],
        out_specs=pl.BlockSpec((L, v_w), lambda b, c: (b * nc + c, 0)),
        out_shape=jax.ShapeDtypeStruct((tokens, v_w), BF16),
        scratch_shapes=[
            pltpu.VMEM((H, DK, DV), F32),
            pltpu.VMEM((H, 1, DK), F32),
            pltpu.VMEM((H, 1, 1), F32),
        ],
        compiler_params=_compiler_params(("parallel", "arbitrary")),
        name="mlstm_scan",
    )(zin, zin, zin, zin, zin, bias, head_gain.reshape(1, v_w))
    return _proj(gated, w_out.astype(BF16), seq=seq, residual=x)


def _qk_norm_rope(slabs, gain2, tables, lane):
    dh = ATTN_HEAD_DIM
    half = ROPE_DIM // 2
    first = lane < dh
    low = lane % dh < half
    idx = range(len(slabs))
    sq = [t * t for t in slabs]
    s0 = [jnp.sum(jnp.where(first, sq[i], 0.0), axis=-1, keepdims=True) for i in idx]
    s1 = [jnp.sum(jnp.where(first, 0.0, sq[i]), axis=-1, keepdims=True) for i in idx]
    tn = [slabs[i] * lax.rsqrt(jnp.where(first, s0[i], s1[i]) * (1.0 / dh) + NORM_EPS) * gain2 for i in idx]
    up = [pltpu.roll(tn[i], LANES - half, 1) for i in idx]
    down = [pltpu.roll(tn[i], half, 1) for i in idx]
    return [tn[i] * tables[i][0] + jnp.where(low, up[i], down[i]) * tables[i][1] for i in idx]


def _swa_kernel(q_ref, kc_ref, kp_ref, vc_ref, vp_ref, pc_ref, pp_ref, freq_ref, qg_ref, kg_ref, sink_ref, o_ref):
    blk, dh, grp = ATTN_WINDOW, ATTN_HEAD_DIM, ATTN_GROUP
    half = ROPE_DIM // 2
    n_kv = kc_ref.shape[1] // dh
    has_prev = pl.program_id(1) > 0

    lane = lax.broadcasted_iota(jnp.int32, (blk, LANES), 1)
    in_rot = lane % dh < ROPE_DIM

    def rope_tables(pos_ref):
        ang = pos_ref[...].astype(F32) * freq_ref[...]
        cos_t = jnp.where(in_rot, jnp.cos(ang), 1.0)
        sin = jnp.sin(ang)
        sin_t = jnp.where(in_rot, jnp.where(lane % dh < half, -sin, sin), 0.0)
        return cos_t, sin_t

    cos_c, sin_c = rope_tables(pc_ref)
    cos_p, sin_p = rope_tables(pp_ref)

    qi = lax.broadcasted_iota(jnp.int32, (blk, 2 * blk), 0) + blk
    kj = lax.broadcasted_iota(jnp.int32, (blk, 2 * blk), 1)
    first_key = jnp.where(has_prev, 0, blk)
    ok = (qi >= kj) & (qi - kj < ATTN_WINDOW) & (kj >= first_key)
    scale = dh ** -0.5
    assert LANES == 2 * dh and scale == 0.125
    first = lane < dh
    first2 = lax.broadcasted_iota(jnp.int32, (2 * blk, LANES), 1) < dh
    slabs_per_kv = grp * dh // LANES

    k_dup, v_split = [], []
    kv_slabs = n_kv * dh // LANES
    k_in = [kp_ref[:, c * LANES:(c + 1) * LANES] for c in range(kv_slabs)]
    k_in += [kc_ref[:, c * LANES:(c + 1) * LANES] for c in range(kv_slabs)]
    k_roped = _qk_norm_rope(k_in, kg_ref[...], [(cos_p, sin_p)] * kv_slabs + [(cos_c, sin_c)] * kv_slabs, lane)
    q_slabs = q_ref.shape[1] // LANES
    q_roped = _qk_norm_rope([q_ref[:, c * LANES:(c + 1) * LANES] for c in range(q_slabs)], qg_ref[...],
                            [(cos_c, sin_c)] * q_slabs, lane)
    for c in range(kv_slabs):
        sl = slice(c * LANES, (c + 1) * LANES)
        kcat = jnp.concatenate([k_roped[c], k_roped[kv_slabs + c]], axis=0)
        vcat = jnp.concatenate([vp_ref[:, sl], vc_ref[:, sl]], axis=0)
        k_rot = pltpu.roll(kcat, dh, 1)
        v_rot = pltpu.roll(vcat, dh, 1)
        k_dup.append(jnp.where(first2, kcat, k_rot).astype(BF16))
        k_dup.append(jnp.where(first2, k_rot, kcat).astype(BF16))
        v_split.append(jnp.concatenate([jnp.where(first2, vcat, 0.0), jnp.where(first2, 0.0, v_rot)],
                                       axis=0).astype(BF16))
        v_split.append(jnp.concatenate([jnp.where(first2, v_rot, 0.0), jnp.where(first2, 0.0, vcat)],
                                       axis=0).astype(BF16))

    for kv in range(n_kv):
        parts = []
        for c in range(kv * slabs_per_kv, (kv + 1) * slabs_per_kv):
            q2 = q_roped[c] * scale
            parts += [jnp.where(first, q2, 0.0), jnp.where(first, 0.0, q2)]
        q_st = jnp.concatenate(parts, axis=0).astype(BF16)
        s_all = _dot_nt(q_st, k_dup[kv])
        hbs = range(grp)
        sink = [sink_ref[:, kv * grp + hb:kv * grp + hb + 1] for hb in hbs]
        s = [jnp.where(ok, s_all[hb * blk:(hb + 1) * blk], -jnp.inf) for hb in hbs]
        m = [jnp.maximum(jnp.max(s[hb], axis=-1, keepdims=True), sink[hb]) for hb in hbs]
        p = [jnp.exp(s[hb] - m[hb]) for hb in hbs]
        p_sum = [jnp.sum(p[hb], axis=-1, keepdims=True) for hb in hbs]
        rinv = [1.0 / (p_sum[hb] + jnp.exp(sink[hb] - m[hb])) for hb in hbs]
        probs = [p[hb].astype(BF16) for hb in hbs]
        for ci in range(slabs_per_kv):
            c = kv * slabs_per_kv + ci
            p_cat = jnp.concatenate([probs[2 * ci], probs[2 * ci + 1]], axis=1)
            out = _dot(p_cat, v_split[kv])
            o_ref[:, c * LANES:(c + 1) * LANES] = (
                out * jnp.where(first, rinv[2 * ci], rinv[2 * ci + 1])).astype(o_ref.dtype)


def _swa_mixer(x, positions, norm_gain, w_qkv, q_gain, k_gain, sinks, w_o, *, batch, seq):
    tokens, d = x.shape
    blk, dh = ATTN_WINDOW, ATTN_HEAD_DIM
    nb = seq // blk
    n_q = sinks.shape[0]
    q_w = n_q * dh
    kv_w = (w_qkv.shape[1] - q_w) // 2
    qkv = _proj(x, w_qkv.astype(BF16), seq=seq, gain=norm_gain)
    pos = positions.reshape(tokens, 1)
    inv_freq = ROPE_THETA ** (-jnp.arange(0, ROPE_DIM, 2, dtype=F32) / ROPE_DIM)
    freq_head = jnp.concatenate([inv_freq, inv_freq, jnp.zeros((dh - ROPE_DIM,), F32)])
    freq = jnp.tile(freq_head, LANES // dh).reshape(1, LANES)
    tile2 = lambda g: jnp.tile(g, LANES // dh).reshape(1, LANES)
    sink_row = jnp.pad(sinks.reshape(1, n_q), ((0, 0), (0, LANES - n_q)))
    cur = lambda b, i: b * nb + i
    prev = lambda b, i: b * nb + jnp.maximum(i - 1, 0)
    k_blk, v_blk = q_w // kv_w, q_w // kv_w + 1
    small = lambda b, i: (0, 0)
    attn = pl.pallas_call(
        _swa_kernel,
        grid=(batch, nb),
        in_specs=[
            pl.BlockSpec((blk, q_w), lambda b, i: (cur(b, i), 0)),
            pl.BlockSpec((blk, kv_w), lambda b, i: (cur(b, i), k_blk)),
            pl.BlockSpec((blk, kv_w), lambda b, i: (prev(b, i), k_blk)),
            pl.BlockSpec((blk, kv_w), lambda b, i: (cur(b, i), v_blk)),
            pl.BlockSpec((blk, kv_w), lambda b, i: (prev(b, i), v_blk)),
            pl.BlockSpec((blk, 1), lambda b, i: (cur(b, i), 0)),
            pl.BlockSpec((blk, 1), lambda b, i: (prev(b, i), 0)),
            pl.BlockSpec((1, LANES), small),
            pl.BlockSpec((1, LANES), small),
            pl.BlockSpec((1, LANES), small),
            pl.BlockSpec((1, LANES), small),
        ],
        out_specs=pl.BlockSpec((blk, q_w), lambda b, i: (cur(b, i), 0)),
        out_shape=jax.ShapeDtypeStruct((tokens, q_w), BF16),
        compiler_params=_compiler_params(("parallel", "arbitrary")),
        name="swa",
    )(qkv, qkv, qkv, qkv, qkv, pos, pos, freq, tile2(q_gain), tile2(k_gain), sink_row)
    return _proj(attn, w_o.astype(BF16), seq=seq, residual=x)


def _rwkv_proj_kernel(x_ref, xb_ref, gain_ref, mix_ref, w_ref, o_ref, xm_ref, *, n_big):
    j = pl.program_id(1)
    tm = x_ref.shape[0]
    tn = o_ref.shape[1]

    @pl.when(j == 0)
    def _():
        hn = _rmsnorm_rows(x_ref[...], gain_ref[...])
        hb = _rmsnorm_rows(xb_ref[0], gain_ref[...])
        row = lax.broadcasted_iota(jnp.int32, hn.shape, 0)
        dx = jnp.where(row == 0, hb, pltpu.roll(hn, 1, 0)) - hn
        for i in range(6):
            xm_ref[i] = (hn + dx * mix_ref[i:i + 1, :]).astype(BF16)

    per_proj = n_big // 3

    @pl.when(j < n_big)
    def _():
        src = jnp.where(j < per_proj, 0, jnp.where(j < 2 * per_proj, 2, 3))
        o_ref[...] = _dot(xm_ref[src], w_ref[...])

    @pl.when(j == n_big)
    def _():
        pad = RWKV_LORA_PAD
        o_ref[:, 0:pad] = _dot(xm_ref[1], w_ref[:, 0:pad])
        o_ref[:, pad:2 * pad] = _dot(xm_ref[4], w_ref[:, pad:2 * pad])
        o_ref[:, 2 * pad:tn] = _dot(xm_ref[5], w_ref[:, 2 * pad:tn])


def _rwkv_scan_kernel(r_ref, k_ref, v_ref, lo_ref, w0_ref, wlb_ref, a0_ref, alb_ref, glb_ref,
                      kk_ref, ka_ref, rk_ref, lnw_ref, lnb_ref, o_ref, st_ref):
    C, N = RWKV_CHUNK, RWKV_HEAD
    pad = RWKV_LORA_PAD
    n_pairs = r_ref.shape[1] // LANES

    @pl.when(pl.program_id(1) == 0)
    def _():
        st_ref[...] = jnp.zeros_like(st_ref)

    lane = lax.broadcasted_iota(jnp.int32, (C, LANES), 1)
    first = lane < N
    row_c = lax.broadcasted_iota(jnp.int32, (C, C), 0)
    col_c = lax.broadcasted_iota(jnp.int32, (C, C), 1)
    tril_bf = (row_c >= col_c).astype(BF16)
    row2 = lax.broadcasted_iota(jnp.int32, (2 * C, 2 * C), 0)
    col2 = lax.broadcasted_iota(jnp.int32, (2 * C, 2 * C), 1)
    same_head = (row2 // C) == (col2 // C)
    strict_lower = same_head & (row2 % C > col2 % C)
    lower = same_head & (row2 % C >= col2 % C)
    eye2 = row2 == col2

    def head_sum(t):
        s0 = jnp.sum(jnp.where(first, t, 0.0), axis=-1, keepdims=True)
        s1 = jnp.sum(jnp.where(first, 0.0, t), axis=-1, keepdims=True)
        return jnp.where(first, s0, s1)

    def stack(t):
        return jnp.concatenate([jnp.where(first, t, 0.0), jnp.where(first, 0.0, t)], axis=0)

    lw1 = jnp.tanh(lo_ref[:, 0:pad]).astype(BF16)
    la1 = lo_ref[:, pad:2 * pad].astype(BF16)
    lg1 = _sigmoid(lo_ref[:, 2 * pad:]).astype(BF16)

    r_all = r_ref[...]
    k_all = k_ref[...]
    zw = w0_ref[...] + _dot(lw1, wlb_ref[...])
    softplus = jnp.maximum(-zw, 0.0) + jnp.log(1.0 + jnp.exp(-jnp.abs(zw)))
    log_decay = -jnp.exp(-softplus - 0.5)
    a_all = _sigmoid(a0_ref[...] + _dot(la1, alb_ref[...]))
    g_all = _dot(lg1, glb_ref[...])
    kk_all = k_all * kk_ref[...]
    k2_all = k_all * (1.0 + (a_all - 1.0) * ka_ref[...])
    cum_all = _cumsum_time(tril_bf, log_decay)
    total_all = cum_all[C - 1:C, :]
    p_incl = jnp.exp(cum_all)
    p_excl = jnp.exp(cum_all - log_decay)
    inv_p = jnp.exp(-cum_all)
    to_end = jnp.exp(total_all - cum_all)
    rt_all = r_all * p_incl
    rk2_all = r_all * k2_all * rk_ref[...]

    pairs = range(n_pairs)
    sls = [slice(p * LANES, (p + 1) * LANES) for p in pairs]

    lhs, rhs, v_sts, ends = [], [], [], []
    for p in pairs:
        sl = sls[p]
        kk = kk_all[:, sl]
        kk = kk / jnp.maximum(jnp.sqrt(head_sum(kk * kk)), 1e-12)
        kka = kk * a_all[:, sl]
        a_t = -kk * p_excl[:, sl]
        b_t = kka * inv_p[:, sl]
        k_t = k2_all[:, sl] * inv_p[:, sl]
        b_e = kka * to_end[:, sl]
        k_e = k2_all[:, sl] * to_end[:, sl]
        lhs.append(jnp.concatenate([stack(a_t), stack(rt_all[:, sl])], axis=0).astype(BF16))
        rhs.append(jnp.concatenate([stack(b_t), stack(k_t)], axis=0).astype(BF16))
        ends.append(jnp.concatenate([stack(b_e), stack(k_e)], axis=0).astype(BF16))
        v_sts.append(stack(v_ref[:, sl]).astype(BF16))

    grams = [_dot_nt(lhs[p], rhs[p]) for p in pairs]
    h2 = 2 * C
    a_ak = [jnp.where(strict_lower, grams[p][:h2, h2:], 0.0).astype(BF16) for p in pairs]
    a_rbk = [jnp.concatenate([jnp.where(lower, grams[p][h2:, :h2], 0.0),
                              jnp.where(lower, grams[p][h2:, h2:], 0.0)], axis=1).astype(BF16) for p in pairs]

    pw = [jnp.where(strict_lower, grams[p][:h2, :h2], 0.0) for p in pairs]
    t_inv = [jnp.where(eye2, 1.0, 0.0) + pw[p] for p in pairs]
    for _ in range(C.bit_length() - 2):
        pw_bf = [pw[p].astype(BF16) for p in pairs]
        pw = [_dot(pw_bf[p], pw_bf[p]) for p in pairs]
        t_inv = [t_inv[p] + _dot(t_inv[p].astype(BF16), pw[p].astype(BF16)) for p in pairs]

    m0 = [st_ref[p] for p in pairs]
    m0_bf = [m0[p].astype(BF16) for p in pairs]
    lm = [_dot(lhs[p], m0_bf[p]) for p in pairs]
    x_rhs = [lm[p][:h2] + _dot(a_ak[p], v_sts[p]) for p in pairs]
    u_bf = [_dot(t_inv[p].astype(BF16), x_rhs[p].astype(BF16)).astype(BF16) for p in pairs]
    uv = [jnp.concatenate([u_bf[p], v_sts[p]], axis=0) for p in pairs]
    y_st = [lm[p][h2:] + _dot(a_rbk[p], uv[p]) for p in pairs]

    for p in pairs:
        total_col = jnp.sum(jnp.where(eye2, total_all[:, sls[p]], 0.0), axis=-1, keepdims=True)
        st_ref[p] = jnp.exp(total_col) * m0[p] + _dot_tn(ends[p], uv[p])

    for p in pairs:
        sl = sls[p]
        y = y_st[p][:C] + y_st[p][C:]
        mu = head_sum(y) * (1.0 / N)
        yc = y - mu
        var = head_sum(yc * yc) * (1.0 / N)
        yn = yc * lax.rsqrt(var + RWKV_LN_EPS) * lnw_ref[:, sl] + lnb_ref[:, sl]
        bonus = head_sum(rk2_all[:, sl])
        o_ref[:, sl] = ((yn + bonus * v_ref[:, sl]) * g_all[:, sl]).astype(o_ref.dtype)


def _rwkv_mixer(x, norm_gain, mix, w_rkv, w0, w_la, w_lb, a0, a_la, a_lb, g_la, g_lb, k_k, k_a, r_k,
                ln_w, ln_b, w_o, *, batch, seq):
    tokens, d = x.shape
    C = RWKV_CHUNK
    pad = RWKV_LORA_PAD
    tm = _row_tile(seq, 512)
    tn = 512
    n_big = 3 * d // tn
    n_tiles = tokens // tm
    lora_w = w_la.shape[1], a_la.shape[1], g_la.shape[1]
    assert lora_w[0] <= pad and lora_w[1] <= pad and 2 * pad + lora_w[2] == tn and 2 * C == LANES
    w_cat = jnp.concatenate([w_rkv[0], w_rkv[1], w_rkv[2], _pad_cols(w_la, pad), _pad_cols(a_la, pad), g_la],
                            axis=1).astype(BF16)
    starts = (jnp.arange(n_tiles) * tm) % seq == 0
    xb = jnp.where(starts[:, None], 0.0, jnp.roll(x[tm - 1::tm], 1, axis=0)).reshape(n_tiles, 1, d)
    proj = pl.pallas_call(
        functools.partial(_rwkv_proj_kernel, n_big=n_big),
        grid=(n_tiles, n_big + 1),
        in_specs=[
            pl.BlockSpec((tm, d), lambda i, j: (i, 0)),
            pl.BlockSpec((1, 1, d), lambda i, j: (i, 0, 0)),
            pl.BlockSpec((1, d), lambda i, j: (0, 0)),
            pl.BlockSpec((6, d), lambda i, j: (0, 0)),
            pl.BlockSpec((d, tn), lambda i, j: (0, j)),
        ],
        out_specs=pl.BlockSpec((tm, tn), lambda i, j: (i, j)),
        out_shape=jax.ShapeDtypeStruct((tokens, 3 * d + tn), F32),
        scratch_shapes=[pltpu.VMEM((6, tm, d), BF16)],
        compiler_params=_compiler_params(("parallel", "arbitrary")),
        name="rwkv_proj",
    )(x, xb, norm_gain.reshape(1, d), mix, w_cat)

    nc = seq // C
    row1 = lambda t: t.reshape(1, d)
    pad_rows = lambda w: jnp.pad(w, ((0, pad - w.shape[0]), (0, 0))).astype(BF16)
    chunk = lambda blk: (lambda b, c: (b * nc + c, blk))
    const = lambda b, c: (0, 0)
    vec = pl.BlockSpec((1, d), const)
    mixed = pl.pallas_call(
        _rwkv_scan_kernel,
        grid=(batch, nc),
        in_specs=[
            pl.BlockSpec((C, d), chunk(0)),
            pl.BlockSpec((C, d), chunk(1)),
            pl.BlockSpec((C, d), chunk(2)),
            pl.BlockSpec((C, tn), chunk(3 * d // tn)),
            vec,
            pl.BlockSpec((pad, d), const),
            vec,
            pl.BlockSpec((pad, d), const),
            pl.BlockSpec((lora_w[2], d), const),
            vec, vec, vec, vec, vec,
        ],
        out_specs=pl.BlockSpec((C, d), chunk(0)),
        out_shape=jax.ShapeDtypeStruct((tokens, d), BF16),
        scratch_shapes=[pltpu.VMEM((d // LANES, LANES, LANES), F32)],
        compiler_params=_compiler_params(("parallel", "arbitrary")),
        name="rwkv_scan",
    )(proj, proj, proj, proj, row1(w0), pad_rows(w_lb), row1(a0), pad_rows(a_lb), g_lb.astype(BF16),
      row1(k_k), row1(k_a), row1(r_k), row1(ln_w), row1(ln_b))
    return _proj(mixed, w_o.astype(BF16), seq=seq, residual=x)


def kernel(x, positions, ffn1_norm, ffn1_w_gu, ffn1_w_down, mixer_norm, ffn2_norm, ffn2_w_gu, ffn2_w_down,
           mlstm_w_in, mlstm_b_gate, mlstm_head_gain, mlstm_w_out,
           attn_w_qkv, attn_q_gain, attn_k_gain, attn_sinks, attn_w_o,
           rwkv_mix, rwkv_w_rkv, rwkv_w0, rwkv_w_lora_a, rwkv_w_lora_b, rwkv_a0, rwkv_a_lora_a, rwkv_a_lora_b,
           rwkv_g_lora_a, rwkv_g_lora_b, rwkv_k_k, rwkv_k_a, rwkv_r_k, rwkv_ln_w, rwkv_ln_b, rwkv_w_o):
    batch, seq, d = x.shape
    depth = ffn1_norm.shape[0]
    h = x.reshape(batch * seq, d)
    for i in range(depth):
        h = _ffn(h, ffn1_norm[i], ffn1_w_gu[i].astype(BF16), ffn1_w_down[i].astype(BF16), seq=seq)
        kind, j = i % 3, i // 3
        if kind == 0:
            h = _mlstm_mixer(h, mixer_norm[i], mlstm_w_in[j], mlstm_b_gate[j], mlstm_head_gain[j],
                             mlstm_w_out[j], batch=batch, seq=seq)
        elif kind == 1:
            h = _swa_mixer(h, positions, mixer_norm[i], attn_w_qkv[j], attn_q_gain[j], attn_k_gain[j],
                           attn_sinks[j], attn_w_o[j], batch=batch, seq=seq)
        else:
            h = _rwkv_mixer(h, mixer_norm[i], rwkv_mix[j], rwkv_w_rkv[j], rwkv_w0[j], rwkv_w_lora_a[j],
                            rwkv_w_lora_b[j], rwkv_a0[j], rwkv_a_lora_a[j], rwkv_a_lora_b[j], rwkv_g_lora_a[j],
                            rwkv_g_lora_b[j], rwkv_k_k[j], rwkv_k_a[j], rwkv_r_k[j], rwkv_ln_w[j], rwkv_ln_b[j],
                            rwkv_w_o[j], batch=batch, seq=seq)
        h = _ffn(h, ffn2_norm[i], ffn2_w_gu[i].astype(BF16), ffn2_w_down[i].astype(BF16), seq=seq)
    return h.reshape(batch, seq, d)
```

```python
import functools

import jax
import jax.numpy as jnp
from jax import lax
from jax.experimental import pallas as pl
from jax.experimental.pallas import tpu as pltpu

F32 = jnp.float32
BF16 = jnp.bfloat16

NORM_EPS = 1e-6

MLSTM_HEADS = 8
MLSTM_DQK = 128
MLSTM_DV = 256
MLSTM_CHUNK = 64
MLSTM_GATE_PAD = 512

ATTN_HEAD_DIM = 64
ATTN_GROUP = 8
ATTN_WINDOW = 128
ROPE_DIM = 16
ROPE_THETA = 500000.0

RWKV_HEAD = 64
RWKV_CHUNK = 64
RWKV_LN_EPS = 64e-5
RWKV_LORA_PAD = 128
RWKV_PROLOGUE_ROWS = 256

LANES = 128
VMEM_LIMIT_BYTES = 56 * 1024 * 1024


def _compiler_params(semantics):
    return pltpu.CompilerParams(dimension_semantics=semantics, vmem_limit_bytes=VMEM_LIMIT_BYTES)


def _row_tile(seq, preferred):
    tile = preferred
    while seq % tile:
        tile //= 2
    return tile


def _dot(a, b):
    return jnp.dot(a, b, preferred_element_type=F32)


def _dot_nt(a, b):
    return lax.dot_general(a, b, (((1,), (1,)), ((), ())), preferred_element_type=F32)


def _dot_tn(a, b):
    return lax.dot_general(a, b, (((0,), (0,)), ((), ())), preferred_element_type=F32)


def _sigmoid(x):
    return 1.0 / (1.0 + jnp.exp(-x))


def _rmsnorm_rows(x, gain):
    ms = jnp.mean(x * x, axis=-1, keepdims=True)
    return x * lax.rsqrt(ms + NORM_EPS) * gain


def _split3(x):
    hi = x.astype(BF16)
    r1 = x - hi.astype(F32)
    mid = r1.astype(BF16)
    lo = (r1 - mid.astype(F32)).astype(BF16)
    return hi, mid, lo


def _cumsum_time(tril_bf, x):
    hi, mid, lo = _split3(x)
    return _dot(tril_bf, hi) + _dot(tril_bf, mid) + _dot(tril_bf, lo)


def _ffn_kernel(x_ref, gain_ref, wg_ref, wu_ref, wd_ref, o_ref, xn_ref):
    @pl.when(pl.program_id(1) == 0)
    def _():
        x = x_ref[...]
        xn_ref[...] = _rmsnorm_rows(x, gain_ref[...]).astype(BF16)
        o_ref[...] = x

    xn = xn_ref[...]
    gate = _dot(xn, wg_ref[...])
    up = _dot(xn, wu_ref[...])
    act = (gate * _sigmoid(gate) * (0.5 * up)).astype(BF16)
    o_ref[...] += _dot(act, wd_ref[...])


def _ffn(x, gain, w_gu, w_down, layer, *, seq):
    tokens, d = x.shape
    f = w_down.shape[1]
    tm = _row_tile(seq, 1024)
    tf = _col_tile(f, 768)
    nf = f // tf
    return pl.pallas_call(
        _ffn_kernel,
        grid=(tokens // tm, nf),
        in_specs=[
            pl.BlockSpec((tm, d), lambda i, j: (i, 0)),
            pl.BlockSpec((1, d), lambda i, j: (0, 0)),
            pl.BlockSpec((None, d, tf), lambda i, j: (layer, 0, j)),
            pl.BlockSpec((None, d, tf), lambda i, j: (layer, 0, j + nf)),
            pl.BlockSpec((None, tf, d), lambda i, j: (layer, j, 0)),
        ],
        out_specs=pl.BlockSpec((tm, d), lambda i, j: (i, 0)),
        out_shape=jax.ShapeDtypeStruct((tokens, d), F32),
        scratch_shapes=[pltpu.VMEM((tm, d), BF16)],
        compiler_params=_compiler_params(("parallel", "arbitrary")),
        name="ffn",
    )(x, gain.reshape(1, d), w_gu, w_gu, w_down)


def _proj_kernel(*refs, norm, residual):
    refs = list(refs)
    x_ref = refs.pop(0)
    gain_ref = refs.pop(0) if norm else None
    w_ref = refs.pop(0)
    res_ref = refs.pop(0) if residual else None
    o_ref = refs.pop(0)

    if norm:
        xn_ref = refs.pop(0)

        @pl.when(pl.program_id(1) == 0)
        def _():
            xn_ref[...] = _rmsnorm_rows(x_ref[...], gain_ref[...]).astype(BF16)

        lhs = xn_ref[...]
    else:
        lhs = x_ref[...]

    acc = _dot(lhs, w_ref[...])
    if residual:
        acc = acc + res_ref[...]
    o_ref[...] = acc


def _col_tile(n, cap):
    best = LANES
    for t in range(LANES, cap + 1, LANES):
        if n % t == 0:
            best = t
    return best


def _proj(x, w, layer, *, seq, gain=None, residual=None):
    tokens, k = x.shape
    n = w.shape[2]
    norm = gain is not None
    assert norm or x.dtype == BF16
    tm = _row_tile(seq, 1024 if norm else 512)
    tn = _col_tile(n, 1024 if norm else 2048)
    in_specs = [pl.BlockSpec((tm, k), lambda i, j: (i, 0))]
    args = [x]
    if norm:
        in_specs.append(pl.BlockSpec((1, k), lambda i, j: (0, 0)))
        args.append(gain.reshape(1, k))
    in_specs.append(pl.BlockSpec((None, k, tn), lambda i, j: (layer, 0, j)))
    args.append(w)
    if residual is not None:
        in_specs.append(pl.BlockSpec((tm, tn), lambda i, j: (i, j)))
        args.append(residual)
    return pl.pallas_call(
        functools.partial(_proj_kernel, norm=norm, residual=residual is not None),
        grid=(tokens // tm, n // tn),
        in_specs=in_specs,
        out_specs=pl.BlockSpec((tm, tn), lambda i, j: (i, j)),
        out_shape=jax.ShapeDtypeStruct((tokens, n), F32),
        scratch_shapes=[pltpu.VMEM((tm, k), BF16)] if norm else [],
        compiler_params=_compiler_params(("parallel", "arbitrary")),
        name="proj",
    )(*args)


def _pad_cols(w, width):
    return jnp.pad(w, ((0, 0), (0, width - w.shape[1])))


def _mlstm_kernel(q_ref, k_ref, v_ref, og_ref, g_ref, bias_ref, hg_ref, o_ref, ct_ref, n_ref, m_ref):
    L, H, DK, DV = MLSTM_CHUNK, MLSTM_HEADS, MLSTM_DQK, MLSTM_DV

    @pl.when(pl.program_id(1) == 0)
    def _():
        ct_ref[...] = jnp.zeros_like(ct_ref)
        n_ref[...] = jnp.zeros_like(n_ref)
        m_ref[...] = jnp.zeros_like(m_ref)

    row = lax.broadcasted_iota(jnp.int32, (L, L), 0)
    col = lax.broadcasted_iota(jnp.int32, (L, L), 1)
    causal = row >= col
    tril_bf = causal.astype(BF16)

    z = g_ref[...] + bias_ref[...]
    log_f = jnp.minimum(z, 0.0) - jnp.log(1.0 + jnp.exp(-jnp.abs(z)))
    bcum = _cumsum_time(tril_bf, log_f)
    z_t = z.T
    bcum_t = bcum.T

    heads = range(H)
    q = [q_ref[:, h * DK:(h + 1) * DK] for h in heads]
    k = [k_ref[:, h * DK:(h + 1) * DK] * (DK ** -0.5) for h in heads]
    q_bf = [t.astype(BF16) for t in q]
    k_bf = [t.astype(BF16) for t in k]
    v_bf = [v_ref[:, h * DV:(h + 1) * DV].astype(BF16) for h in heads]
    m_prev = [m_ref[h] for h in heads]

    b_col = [bcum[:, H + h:H + h + 1] for h in heads]
    g = [bcum[L - 1:L, H + h:H + h + 1] for h in heads]
    qk = [_dot_nt(q_bf[h], k_bf[h]) for h in heads]
    inter = [_dot(q_bf[h], ct_ref[h].astype(BF16)) for h in heads]
    qn = [jnp.sum(q[h] * n_ref[h], axis=-1, keepdims=True) for h in heads]

    log_d = [jnp.where(causal, b_col[h] - bcum_t[H + h:H + h + 1, :] + z_t[h:h + 1, :], -jnp.inf) for h in heads]
    row_max = [jnp.max(log_d[h], axis=-1, keepdims=True) for h in heads]
    log_w = [g[h] - b_col[h] + z[:, h:h + 1] for h in heads]
    m_new = [jnp.maximum(g[h] + m_prev[h], jnp.max(log_w[h], axis=0, keepdims=True)) for h in heads]
    kw = [k[h] * jnp.exp(log_w[h] - m_new[h]) for h in heads]
    outer = [_dot_tn(kw[h].astype(BF16), v_bf[h]) for h in heads]

    log_inter = [b_col[h] + m_prev[h] for h in heads]
    m_out = [jnp.maximum(log_inter[h], row_max[h]) for h in heads]
    s = [qk[h] * jnp.exp(log_d[h] - m_out[h]) for h in heads]
    intra = [_dot(s[h].astype(BF16), v_bf[h]) for h in heads]
    s_sum = [jnp.sum(s[h], axis=-1, keepdims=True) for h in heads]
    w_inter = [jnp.exp(log_inter[h] - m_out[h]) for h in heads]
    den = [s_sum[h] + w_inter[h] * qn[h] for h in heads]
    scale = [1.0 / jnp.maximum(jnp.abs(den[h]), jnp.exp(-m_out[h])) for h in heads]
    hout = [(intra[h] + w_inter[h] * inter[h]) * scale[h] for h in heads]
    ms = [jnp.mean(hout[h] * hout[h], axis=-1, keepdims=True) for h in heads]
    gate = [_sigmoid(og_ref[:, h * DV:(h + 1) * DV]) * hg_ref[:, h * DV:(h + 1) * DV] for h in heads]
    for h in heads:
        o_ref[:, h * DV:(h + 1) * DV] = (hout[h] * lax.rsqrt(ms[h] + NORM_EPS) * gate[h]).astype(o_ref.dtype)

    for h in heads:
        decay = jnp.exp(g[h] + m_prev[h] - m_new[h])
        ct_ref[h] = decay * ct_ref[h] + outer[h]
        n_ref[h] = decay * n_ref[h] + jnp.sum(kw[h], axis=0, keepdims=True)
        m_ref[h] = m_new[h]


def _mlstm_mixer(x, norm_gain, w_in_padded, b_gate, head_gain, w_out, layer, *, batch, seq):
    tokens, d = x.shape
    L, H, DK, DV = MLSTM_CHUNK, MLSTM_HEADS, MLSTM_DQK, MLSTM_DV
    nc = seq // L
    main = 2 * H * DK + 2 * H * DV
    assert w_in_padded.shape[2] == main + MLSTM_GATE_PAD
    zin = _proj(x, w_in_padded, layer, seq=seq, gain=norm_gain)
    bias = jnp.pad(b_gate.reshape(1, 2 * H), ((0, 0), (0, LANES - 2 * H)))
    qk_w, v_w = H * DK, H * DV
    gated = pl.pallas_call(
        _mlstm_kernel,
        grid=(batch, nc),
        in_specs=[
            pl.BlockSpec((L, qk_w), lambda b, c: (b * nc + c, 0)),
            pl.BlockSpec((L, qk_w), lambda b, c: (b * nc + c, 1)),
            pl.BlockSpec((L, v_w), lambda b, c: (b * nc + c, 2 * qk_w // v_w)),
            pl.BlockSpec((L, v_w), lambda b, c: (b * nc + c, 2 * qk_w // v_w + 1)),
            pl.BlockSpec((L, LANES), lambda b, c: (b * nc + c, main // LANES)),
            pl.BlockSpec((1, LANES), lambda b, c: (0, 0)),
            pl.BlockSpec((1, v_w), lambda b, c: (0, 0)),
        ],
        out_specs=pl.BlockSpec((L, v_w), lambda b, c: (b * nc + c, 0)),
        out_shape=jax.ShapeDtypeStruct((tokens, v_w), BF16),
        scratch_shapes=[
            pltpu.VMEM((H, DK, DV), F32),
            pltpu.VMEM((H, 1, DK), F32),
            pltpu.VMEM((H, 1, 1), F32),
        ],
        compiler_params=_compiler_params(("parallel", "arbitrary")),
        name="mlstm_scan",
    )(zin, zin, zin, zin, zin, bias, head_gain.reshape(1, v_w))
    return _proj(gated, w_out, layer, seq=seq, residual=x)


def _qk_norm_rope(slabs, gain2, tables, lane):
    dh = ATTN_HEAD_DIM
    half = ROPE_DIM // 2
    first = lane < dh
    low = lane % dh < half
    idx = range(len(slabs))
    sq = [t * t for t in slabs]
    s0 = [jnp.sum(jnp.where(first, sq[i], 0.0), axis=-1, keepdims=True) for i in idx]
    s1 = [jnp.sum(jnp.where(first, 0.0, sq[i]), axis=-1, keepdims=True) for i in idx]
    tn = [slabs[i] * lax.rsqrt(jnp.where(first, s0[i], s1[i]) * (1.0 / dh) + NORM_EPS) * gain2 for i in idx]
    up = [pltpu.roll(tn[i], LANES - half, 1) for i in idx]
    down = [pltpu.roll(tn[i], half, 1) for i in idx]
    return [tn[i] * tables[i][0] + jnp.where(low, up[i], down[i]) * tables[i][1] for i in idx]


def _swa_kernel(q_ref, kc_ref, kp_ref, vc_ref, vp_ref, pc_ref, pp_ref, freq_ref, qg_ref, kg_ref, sink_ref, o_ref):
    blk, dh, grp = ATTN_WINDOW, ATTN_HEAD_DIM, ATTN_GROUP
    half = ROPE_DIM // 2
    n_kv = kc_ref.shape[1] // dh
    has_prev = pl.program_id(1) > 0

    lane = lax.broadcasted_iota(jnp.int32, (blk, LANES), 1)
    in_rot = lane % dh < ROPE_DIM

    def rope_tables(pos_ref):
        ang = pos_ref[...].astype(F32) * freq_ref[...]
        cos_t = jnp.where(in_rot, jnp.cos(ang), 1.0)
        sin = jnp.sin(ang)
        sin_t = jnp.where(in_rot, jnp.where(lane % dh < half, -sin, sin), 0.0)
        return cos_t, sin_t

    cos_c, sin_c = rope_tables(pc_ref)
    cos_p, sin_p = rope_tables(pp_ref)

    qi = lax.broadcasted_iota(jnp.int32, (blk, 2 * blk), 0) + blk
    kj = lax.broadcasted_iota(jnp.int32, (blk, 2 * blk), 1)
    first_key = jnp.where(has_prev, 0, blk)
    ok = (qi >= kj) & (qi - kj < ATTN_WINDOW) & (kj >= first_key)
    scale = dh ** -0.5
    assert LANES == 2 * dh and scale == 0.125
    first = lane < dh
    first2 = lax.broadcasted_iota(jnp.int32, (2 * blk, LANES), 1) < dh
    slabs_per_kv = grp * dh // LANES

    k_dup, v_split = [], []
    kv_slabs = n_kv * dh // LANES
    k_in = [kp_ref[:, c * LANES:(c + 1) * LANES] for c in range(kv_slabs)]
    k_in += [kc_ref[:, c * LANES:(c + 1) * LANES] for c in range(kv_slabs)]
    k_roped = _qk_norm_rope(k_in, kg_ref[...], [(cos_p, sin_p)] * kv_slabs + [(cos_c, sin_c)] * kv_slabs, lane)
    q_slabs = q_ref.shape[1] // LANES
    q_roped = _qk_norm_rope([q_ref[:, c * LANES:(c + 1) * LANES] for c in range(q_slabs)], qg_ref[...],
                            [(cos_c, sin_c)] * q_slabs, lane)
    for c in range(kv_slabs):
        sl = slice(c * LANES, (c + 1) * LANES)
        kcat = jnp.concatenate([k_roped[c], k_roped[kv_slabs + c]], axis=0)
        vcat = jnp.concatenate([vp_ref[:, sl], vc_ref[:, sl]], axis=0)
        k_rot = pltpu.roll(kcat, dh, 1)
        v_rot = pltpu.roll(vcat, dh, 1)
        k_dup.append(jnp.where(first2, kcat, k_rot).astype(BF16))
        k_dup.append(jnp.where(first2, k_rot, kcat).astype(BF16))
        v_split.append(jnp.concatenate([jnp.where(first2, vcat, 0.0), jnp.where(first2, 0.0, v_rot)],
                                       axis=0).astype(BF16))
        v_split.append(jnp.concatenate([jnp.where(first2, v_rot, 0.0), jnp.where(first2, 0.0, vcat)],
                                       axis=0).astype(BF16))

    for kv in range(n_kv):
        parts = []
        for c in range(kv * slabs_per_kv, (kv + 1) * slabs_per_kv):
            q2 = q_roped[c] * scale
            parts += [jnp.where(first, q2, 0.0), jnp.where(first, 0.0, q2)]
        q_st = jnp.concatenate(parts, axis=0).astype(BF16)
        s_all = _dot_nt(q_st, k_dup[kv])
        hbs = range(grp)
        sink = [sink_ref[:, kv * grp + hb:kv * grp + hb + 1] for hb in hbs]
        s = [jnp.where(ok, s_all[hb * blk:(hb + 1) * blk], -jnp.inf) for hb in hbs]
        m = [jnp.maximum(jnp.max(s[hb], axis=-1, keepdims=True), sink[hb]) for hb in hbs]
        p = [jnp.exp(s[hb] - m[hb]) for hb in hbs]
        p_sum = [jnp.sum(p[hb], axis=-1, keepdims=True) for hb in hbs]
        rinv = [1.0 / (p_sum[hb] + jnp.exp(sink[hb] - m[hb])) for hb in hbs]
        probs = [p[hb].astype(BF16) for hb in hbs]
        for ci in range(slabs_per_kv):
            c = kv * slabs_per_kv + ci
            p_cat = jnp.concatenate([probs[2 * ci], probs[2 * ci + 1]], axis=1)
            out = _dot(p_cat, v_split[kv])
            o_ref[:, c * LANES:(c + 1) * LANES] = (
                out * jnp.where(first, rinv[2 * ci], rinv[2 * ci + 1])).astype(o_ref.dtype)


def _swa_mixer(x, positions, norm_gain, w_qkv, q_gain, k_gain, sinks, w_o, layer, *, batch, seq):
    tokens, d = x.shape
    blk, dh = ATTN_WINDOW, ATTN_HEAD_DIM
    nb = seq // blk
    n_q = sinks.shape[0]
    q_w = n_q * dh
    kv_w = (w_qkv.shape[2] - q_w) // 2
    qkv = _proj(x, w_qkv, layer, seq=seq, gain=norm_gain)
    pos = positions.reshape(tokens, 1)
    inv_freq = ROPE_THETA ** (-jnp.arange(0, ROPE_DIM, 2, dtype=F32) / ROPE_DIM)
    freq_head = jnp.concatenate([inv_freq, inv_freq, jnp.zeros((dh - ROPE_DIM,), F32)])
    freq = jnp.tile(freq_head, LANES // dh).reshape(1, LANES)
    tile2 = lambda g: jnp.tile(g, LANES // dh).reshape(1, LANES)
    sink_row = jnp.pad(sinks.reshape(1, n_q), ((0, 0), (0, LANES - n_q)))
    cur = lambda b, i: b * nb + i
    prev = lambda b, i: b * nb + jnp.maximum(i - 1, 0)
    k_blk, v_blk = q_w // kv_w, q_w // kv_w + 1
    small = lambda b, i: (0, 0)
    attn = pl.pallas_call(
        _swa_kernel,
        grid=(batch, nb),
        in_specs=[
            pl.BlockSpec((blk, q_w), lambda b, i: (cur(b, i), 0)),
            pl.BlockSpec((blk, kv_w), lambda b, i: (cur(b, i), k_blk)),
            pl.BlockSpec((blk, kv_w), lambda b, i: (prev(b, i), k_blk)),
            pl.BlockSpec((blk, kv_w), lambda b, i: (cur(b, i), v_blk)),
            pl.BlockSpec((blk, kv_w), lambda b, i: (prev(b, i), v_blk)),
            pl.BlockSpec((blk, 1), lambda b, i: (cur(b, i), 0)),
            pl.BlockSpec((blk, 1), lambda b, i: (prev(b, i), 0)),
            pl.BlockSpec((1, LANES), small),
            pl.BlockSpec((1, LANES), small),
            pl.BlockSpec((1, LANES), small),
            pl.BlockSpec((1, LANES), small),
        ],
        out_specs=pl.BlockSpec((blk, q_w), lambda b, i: (cur(b, i), 0)),
        out_shape=jax.ShapeDtypeStruct((tokens, q_w), BF16),
        compiler_params=_compiler_params(("parallel", "arbitrary")),
        name="swa",
    )(qkv, qkv, qkv, qkv, qkv, pos, pos, freq, tile2(q_gain), tile2(k_gain), sink_row)
    return _proj(attn, w_o, layer, seq=seq, residual=x)


def _rwkv_proj_kernel(x_ref, xb_ref, gain_ref, mix_ref, wl_ref, w_ref, o_ref, lo_ref, xm_ref, *, per_proj):
    j = pl.program_id(1)
    tm, d = x_ref.shape
    pad = RWKV_LORA_PAD

    @pl.when(j == 0)
    def _():
        gain = gain_ref[...]
        rc = min(RWKV_PROLOGUE_ROWS, tm)
        row = lax.broadcasted_iota(jnp.int32, (rc, d), 0)
        for c in range(tm // rc):
            rows = slice(c * rc, (c + 1) * rc)
            hn = _rmsnorm_rows(x_ref[rows, :], gain)
            before = xb_ref[0] if c == 0 else x_ref[c * rc - 1:c * rc, :]
            dx = jnp.where(row == 0, _rmsnorm_rows(before, gain), pltpu.roll(hn, 1, 0)) - hn

            def mixed(i):
                return (hn + dx * mix_ref[i:i + 1, :]).astype(BF16)

            xm_ref[0, rows, :] = mixed(0)
            xm_ref[1, rows, :] = mixed(2)
            xm_ref[2, rows, :] = mixed(3)
            lo_ref[rows, 0:pad] = _dot(mixed(1), wl_ref[:, 0:pad])
            lo_ref[rows, pad:2 * pad] = _dot(mixed(4), wl_ref[:, pad:2 * pad])
            lo_ref[rows, 2 * pad:] = _dot(mixed(5), wl_ref[:, 2 * pad:])

    o_ref[...] = _dot(xm_ref[j // per_proj], w_ref[...])


def _rwkv_scan_kernel(r_ref, k_ref, v_ref, lo_ref, w0_ref, wlb_ref, a0_ref, alb_ref, glb_ref,
                      kk_ref, ka_ref, rk_ref, lnw_ref, lnb_ref, o_ref, st_ref):
    C, N = RWKV_CHUNK, RWKV_HEAD
    pad = RWKV_LORA_PAD
    n_pairs = r_ref.shape[1] // LANES

    @pl.when(pl.program_id(1) == 0)
    def _():
        st_ref[...] = jnp.zeros_like(st_ref)

    lane = lax.broadcasted_iota(jnp.int32, (C, LANES), 1)
    first = lane < N
    row_c = lax.broadcasted_iota(jnp.int32, (C, C), 0)
    col_c = lax.broadcasted_iota(jnp.int32, (C, C), 1)
    tril_bf = (row_c >= col_c).astype(BF16)
    row2 = lax.broadcasted_iota(jnp.int32, (2 * C, 2 * C), 0)
    col2 = lax.broadcasted_iota(jnp.int32, (2 * C, 2 * C), 1)
    same_head = (row2 // C) == (col2 // C)
    strict_lower = same_head & (row2 % C > col2 % C)
    lower = same_head & (row2 % C >= col2 % C)
    eye2 = row2 == col2

    def head_sum(t):
        s0 = jnp.sum(jnp.where(first, t, 0.0), axis=-1, keepdims=True)
        s1 = jnp.sum(jnp.where(first, 0.0, t), axis=-1, keepdims=True)
        return jnp.where(first, s0, s1)

    def stack(t):
        return jnp.concatenate([jnp.where(first, t, 0.0), jnp.where(first, 0.0, t)], axis=0)

    lw1 = jnp.tanh(lo_ref[:, 0:pad]).astype(BF16)
    la1 = lo_ref[:, pad:2 * pad].astype(BF16)
    lg1 = _sigmoid(lo_ref[:, 2 * pad:]).astype(BF16)

    r_all = r_ref[...]
    k_all = k_ref[...]
    zw = w0_ref[...] + _dot(lw1, wlb_ref[...])
    softplus = jnp.maximum(-zw, 0.0) + jnp.log(1.0 + jnp.exp(-jnp.abs(zw)))
    log_decay = -jnp.exp(-softplus - 0.5)
    a_all = _sigmoid(a0_ref[...] + _dot(la1, alb_ref[...]))
    g_all = _dot(lg1, glb_ref[...])
    kk_all = k_all * kk_ref[...]
    k2_all = k_all * (1.0 + (a_all - 1.0) * ka_ref[...])
    cum_all = _cumsum_time(tril_bf, log_decay)
    total_all = cum_all[C - 1:C, :]
    p_incl = jnp.exp(cum_all)
    p_excl = jnp.exp(cum_all - log_decay)
    inv_p = jnp.exp(-cum_all)
    to_end = jnp.exp(total_all - cum_all)
    rt_all = r_all * p_incl
    rk2_all = r_all * k2_all * rk_ref[...]

    pairs = range(n_pairs)
    sls = [slice(p * LANES, (p + 1) * LANES) for p in pairs]

    lhs, rhs, v_sts, ends = [], [], [], []
    for p in pairs:
        sl = sls[p]
        kk = kk_all[:, sl]
        kk = kk / jnp.maximum(jnp.sqrt(head_sum(kk * kk)), 1e-12)
        kka = kk * a_all[:, sl]
        a_t = -kk * p_excl[:, sl]
        b_t = kka * inv_p[:, sl]
        k_t = k2_all[:, sl] * inv_p[:, sl]
        b_e = kka * to_end[:, sl]
        k_e = k2_all[:, sl] * to_end[:, sl]
        lhs.append(jnp.concatenate([stack(a_t), stack(rt_all[:, sl])], axis=0).astype(BF16))
        rhs.append(jnp.concatenate([stack(b_t), stack(k_t)], axis=0).astype(BF16))
        ends.append(jnp.concatenate([stack(b_e), stack(k_e)], axis=0).astype(BF16))
        v_sts.append(stack(v_ref[:, sl]).astype(BF16))

    grams = [_dot_nt(lhs[p], rhs[p]) for p in pairs]
    h2 = 2 * C
    a_ak = [jnp.where(strict_lower, grams[p][:h2, h2:], 0.0).astype(BF16) for p in pairs]
    a_rbk = [jnp.concatenate([jnp.where(lower, grams[p][h2:, :h2], 0.0),
                              jnp.where(lower, grams[p][h2:, h2:], 0.0)], axis=1).astype(BF16) for p in pairs]

    pw = [jnp.where(strict_lower, grams[p][:h2, :h2], 0.0) for p in pairs]
    t_inv = [jnp.where(eye2, 1.0, 0.0) + pw[p] for p in pairs]
    for _ in range(C.bit_length() - 2):
        pw_bf = [pw[p].astype(BF16) for p in pairs]
        pw = [_dot(pw_bf[p], pw_bf[p]) for p in pairs]
        t_inv = [t_inv[p] + _dot(t_inv[p].astype(BF16), pw[p].astype(BF16)) for p in pairs]

    m0 = [st_ref[p] for p in pairs]
    m0_bf = [m0[p].astype(BF16) for p in pairs]
    lm = [_dot(lhs[p], m0_bf[p]) for p in pairs]
    x_rhs = [lm[p][:h2] + _dot(a_ak[p], v_sts[p]) for p in pairs]
    u_bf = [_dot(t_inv[p].astype(BF16), x_rhs[p].astype(BF16)).astype(BF16) for p in pairs]
    uv = [jnp.concatenate([u_bf[p], v_sts[p]], axis=0) for p in pairs]
    y_st = [lm[p][h2:] + _dot(a_rbk[p], uv[p]) for p in pairs]

    for p in pairs:
        total_col = jnp.sum(jnp.where(eye2, total_all[:, sls[p]], 0.0), axis=-1, keepdims=True)
        st_ref[p] = jnp.exp(total_col) * m0[p] + _dot_tn(ends[p], uv[p])

    for p in pairs:
        sl = sls[p]
        y = y_st[p][:C] + y_st[p][C:]
        mu = head_sum(y) * (1.0 / N)
        yc = y - mu
        var = head_sum(yc * yc) * (1.0 / N)
        yn = yc * lax.rsqrt(var + RWKV_LN_EPS) * lnw_ref[:, sl] + lnb_ref[:, sl]
        bonus = head_sum(rk2_all[:, sl])
        o_ref[:, sl] = ((yn + bonus * v_ref[:, sl]) * g_all[:, sl]).astype(o_ref.dtype)


def _rwkv_mixer(x, norm_gain, mix, w_rkv, w0, w_la, w_lb, a0, a_la, a_lb, g_la, g_lb, k_k, k_a, r_k,
                ln_w, ln_b, w_o, layer, *, batch, seq):
    tokens, d = x.shape
    C = RWKV_CHUNK
    pad = RWKV_LORA_PAD
    tm = _row_tile(seq, 1024)
    tn = 512
    per_proj = d // tn
    n_tiles = tokens // tm
    lora_w = w_la.shape[1], a_la.shape[1], g_la.shape[1]
    lora_cols = 2 * pad + lora_w[2]
    assert lora_w[0] <= pad and lora_w[1] <= pad and 2 * C == LANES
    w_lora = jnp.concatenate([_pad_cols(w_la, pad), _pad_cols(a_la, pad), g_la], axis=1).astype(BF16)
    starts = (jnp.arange(n_tiles) * tm) % seq == 0
    xb = jnp.where(starts[:, None], 0.0, jnp.roll(x[tm - 1::tm], 1, axis=0)).reshape(n_tiles, 1, d)
    proj, lora = pl.pallas_call(
        functools.partial(_rwkv_proj_kernel, per_proj=per_proj),
        grid=(n_tiles, 3 * per_proj),
        in_specs=[
            pl.BlockSpec((tm, d), lambda i, j: (i, 0)),
            pl.BlockSpec((1, 1, d), lambda i, j: (i, 0, 0)),
            pl.BlockSpec((1, d), lambda i, j: (0, 0)),
            pl.BlockSpec((6, d), lambda i, j: (0, 0)),
            pl.BlockSpec((d, lora_cols), lambda i, j: (0, 0)),
            pl.BlockSpec((None, d, tn), lambda i, j: (j // per_proj, 0, j % per_proj)),
        ],
        out_specs=[
            pl.BlockSpec((tm, tn), lambda i, j: (i, j)),
            pl.BlockSpec((tm, lora_cols), lambda i, j: (i, 0)),
        ],
        out_shape=[
            jax.ShapeDtypeStruct((tokens, 3 * d), F32),
            jax.ShapeDtypeStruct((tokens, lora_cols), F32),
        ],
        scratch_shapes=[pltpu.VMEM((3, tm, d), BF16)],
        compiler_params=_compiler_params(("parallel", "arbitrary")),
        name="rwkv_proj",
    )(x, xb, norm_gain.reshape(1, d), mix, w_lora, w_rkv.astype(BF16))

    nc = seq // C
    row1 = lambda t: t.reshape(1, d)
    pad_rows = lambda w: jnp.pad(w, ((0, pad - w.shape[0]), (0, 0))).astype(BF16)
    chunk = lambda blk: (lambda b, c: (b * nc + c, blk))
    const = lambda b, c: (0, 0)
    vec = pl.BlockSpec((1, d), const)
    mixed = pl.pallas_call(
        _rwkv_scan_kernel,
        grid=(batch, nc),
        in_specs=[
            pl.BlockSpec((C, d), chunk(0)),
            pl.BlockSpec((C, d), chunk(1)),
            pl.BlockSpec((C, d), chunk(2)),
            pl.BlockSpec((C, lora_cols), chunk(0)),
            vec,
            pl.BlockSpec((pad, d), const),
            vec,
            pl.BlockSpec((pad, d), const),
            pl.BlockSpec((lora_w[2], d), const),
            vec, vec, vec, vec, vec,
        ],
        out_specs=pl.BlockSpec((C, d), chunk(0)),
        out_shape=jax.ShapeDtypeStruct((tokens, d), BF16),
        scratch_shapes=[pltpu.VMEM((d // LANES, LANES, LANES), F32)],
        compiler_params=_compiler_params(("parallel", "arbitrary")),
        name="rwkv_scan",
    )(proj, proj, proj, lora, row1(w0), pad_rows(w_lb), row1(a0), pad_rows(a_lb), g_lb.astype(BF16),
      row1(k_k), row1(k_a), row1(r_k), row1(ln_w), row1(ln_b))
    return _proj(mixed, w_o, layer, seq=seq, residual=x)


def kernel(x, positions, ffn1_norm, ffn1_w_gu, ffn1_w_down, mixer_norm, ffn2_norm, ffn2_w_gu, ffn2_w_down,
           mlstm_w_in, mlstm_b_gate, mlstm_head_gain, mlstm_w_out,
           attn_w_qkv, attn_q_gain, attn_k_gain, attn_sinks, attn_w_o,
           rwkv_mix, rwkv_w_rkv, rwkv_w0, rwkv_w_lora_a, rwkv_w_lora_b, rwkv_a0, rwkv_a_lora_a, rwkv_a_lora_b,
           rwkv_g_lora_a, rwkv_g_lora_b, rwkv_k_k, rwkv_k_a, rwkv_r_k, rwkv_ln_w, rwkv_ln_b, rwkv_w_o):
    batch, seq, d = x.shape
    depth = ffn1_norm.shape[0]
    h = x.reshape(batch * seq, d)
    ffn1_gu, ffn1_down = ffn1_w_gu.astype(BF16), ffn1_w_down.astype(BF16)
    ffn2_gu, ffn2_down = ffn2_w_gu.astype(BF16), ffn2_w_down.astype(BF16)
    mlstm_in = jnp.pad(mlstm_w_in, ((0, 0), (0, 0), (0, MLSTM_GATE_PAD - 2 * MLSTM_HEADS))).astype(BF16)
    mlstm_out = mlstm_w_out.astype(BF16)
    attn_qkv, attn_o = attn_w_qkv.astype(BF16), attn_w_o.astype(BF16)
    rwkv_o = rwkv_w_o.astype(BF16)
    for i in range(depth):
        h = _ffn(h, ffn1_norm[i], ffn1_gu, ffn1_down, i, seq=seq)
        kind, j = i % 3, i // 3
        if kind == 0:
            h = _mlstm_mixer(h, mixer_norm[i], mlstm_in, mlstm_b_gate[j], mlstm_head_gain[j], mlstm_out, j,
                             batch=batch, seq=seq)
        elif kind == 1:
            h = _swa_mixer(h, positions, mixer_norm[i], attn_qkv, attn_q_gain[j], attn_k_gain[j],
                           attn_sinks[j], attn_o, j, batch=batch, seq=seq)
        else:
            h = _rwkv_mixer(h, mixer_norm[i], rwkv_mix[j], rwkv_w_rkv[j], rwkv_w0[j], rwkv_w_lora_a[j],
                            rwkv_w_lora_b[j], rwkv_a0[j], rwkv_a_lora_a[j], rwkv_a_lora_b[j], rwkv_g_lora_a[j],
                            rwkv_g_lora_b[j], rwkv_k_k[j], rwkv_k_a[j], rwkv_r_k[j], rwkv_ln_w[j], rwkv_ln_b[j],
                            rwkv_o, j, batch=batch, seq=seq)
        h = _ffn(h, ffn2_norm[i], ffn2_gu, ffn2_down, i, seq=seq)
    return h.reshape(batch, seq, d)
```

```python
import functools

import jax
import jax.numpy as jnp
from jax import lax
from jax.experimental import pallas as pl
from jax.experimental.pallas import tpu as pltpu

F32 = jnp.float32
BF16 = jnp.bfloat16

NORM_EPS = 1e-6

MLSTM_HEADS = 8
MLSTM_DQK = 128
MLSTM_DV = 256
MLSTM_CHUNK = 64
MLSTM_GATE_PAD = 512
MLSTM_ROWS = 4

ATTN_HEAD_DIM = 64
ATTN_GROUP = 8
ATTN_WINDOW = 128
ROPE_DIM = 16
ROPE_THETA = 500000.0

RWKV_HEAD = 64
RWKV_CHUNK = 64
RWKV_LN_EPS = 64e-5
RWKV_LORA_PAD = 128
RWKV_PROLOGUE_ROWS = 256

LANES = 128
SUBLANES = 8
VMEM_LIMIT_BYTES = 56 * 1024 * 1024


def _compiler_params(semantics):
    return pltpu.CompilerParams(dimension_semantics=semantics, vmem_limit_bytes=VMEM_LIMIT_BYTES)


def _row_tile(seq, preferred):
    tile = preferred
    while seq % tile:
        tile //= 2
    return tile


def _dot(a, b):
    return jnp.dot(a, b, preferred_element_type=F32)


def _dot_nt(a, b):
    return lax.dot_general(a, b, (((1,), (1,)), ((), ())), preferred_element_type=F32)


def _dot_tn(a, b):
    return lax.dot_general(a, b, (((0,), (0,)), ((), ())), preferred_element_type=F32)


def _sigmoid(x):
    return 1.0 / (1.0 + jnp.exp(-x))


def _rmsnorm_rows(x, gain):
    ms = jnp.mean(x * x, axis=-1, keepdims=True)
    return x * lax.rsqrt(ms + NORM_EPS) * gain


def _split3(x):
    hi = x.astype(BF16)
    r1 = x - hi.astype(F32)
    mid = r1.astype(BF16)
    lo = (r1 - mid.astype(F32)).astype(BF16)
    return hi, mid, lo


def _cumsum_time(tril_bf, x):
    hi, mid, lo = _split3(x)
    return _dot(tril_bf, hi) + _dot(tril_bf, mid) + _dot(tril_bf, lo)


def _ffn_kernel(x_ref, gain_ref, wg_ref, wu_ref, wd_ref, o_ref, xn_ref):
    @pl.when(pl.program_id(1) == 0)
    def _():
        x = x_ref[...]
        xn_ref[...] = _rmsnorm_rows(x, gain_ref[...]).astype(BF16)
        o_ref[...] = x

    xn = xn_ref[...]
    gate = _dot(xn, wg_ref[...])
    up = _dot(xn, wu_ref[...])
    act = (gate * _sigmoid(gate) * (0.5 * up)).astype(BF16)
    o_ref[...] += _dot(act, wd_ref[...])


def _ffn(x, gain, w_gu, w_down, layer, *, seq):
    tokens, d = x.shape
    f = w_down.shape[1]
    tm = _row_tile(seq, 1024)
    tf = _col_tile(f, 768)
    nf = f // tf
    return pl.pallas_call(
        _ffn_kernel,
        grid=(tokens // tm, nf),
        in_specs=[
            pl.BlockSpec((tm, d), lambda i, j: (i, 0)),
            pl.BlockSpec((1, d), lambda i, j: (0, 0)),
            pl.BlockSpec((None, d, tf), lambda i, j: (layer, 0, j)),
            pl.BlockSpec((None, d, tf), lambda i, j: (layer, 0, j + nf)),
            pl.BlockSpec((None, tf, d), lambda i, j: (layer, j, 0)),
        ],
        out_specs=pl.BlockSpec((tm, d), lambda i, j: (i, 0)),
        out_shape=jax.ShapeDtypeStruct((tokens, d), F32),
        scratch_shapes=[pltpu.VMEM((tm, d), BF16)],
        compiler_params=_compiler_params(("parallel", "arbitrary")),
        name="ffn",
    )(x, gain.reshape(1, d), w_gu, w_gu, w_down)


def _proj_kernel(*refs, norm, residual):
    refs = list(refs)
    x_ref = refs.pop(0)
    gain_ref = refs.pop(0) if norm else None
    w_ref = refs.pop(0)
    res_ref = refs.pop(0) if residual else None
    o_ref = refs.pop(0)

    if norm:
        xn_ref = refs.pop(0)

        @pl.when(pl.program_id(1) == 0)
        def _():
            xn_ref[...] = _rmsnorm_rows(x_ref[...], gain_ref[...]).astype(BF16)

        lhs = xn_ref[...]
    else:
        lhs = x_ref[...]

    acc = _dot(lhs, w_ref[...])
    if residual:
        acc = acc + res_ref[...]
    o_ref[...] = acc


def _col_tile(n, cap):
    best = LANES
    for t in range(LANES, cap + 1, LANES):
        if n % t == 0:
            best = t
    return best


def _proj(x, w, layer, *, seq, gain=None, residual=None):
    tokens, k = x.shape
    n = w.shape[2]
    norm = gain is not None
    assert norm or x.dtype == BF16
    tm = _row_tile(seq, 1024 if norm else 512)
    tn = _col_tile(n, 1024 if norm else 2048)
    in_specs = [pl.BlockSpec((tm, k), lambda i, j: (i, 0))]
    args = [x]
    if norm:
        in_specs.append(pl.BlockSpec((1, k), lambda i, j: (0, 0)))
        args.append(gain.reshape(1, k))
    in_specs.append(pl.BlockSpec((None, k, tn), lambda i, j: (layer, 0, j)))
    args.append(w)
    if residual is not None:
        in_specs.append(pl.BlockSpec((tm, tn), lambda i, j: (i, j)))
        args.append(residual)
    return pl.pallas_call(
        functools.partial(_proj_kernel, norm=norm, residual=residual is not None),
        grid=(tokens // tm, n // tn),
        in_specs=in_specs,
        out_specs=pl.BlockSpec((tm, tn), lambda i, j: (i, j)),
        out_shape=jax.ShapeDtypeStruct((tokens, n), F32),
        scratch_shapes=[pltpu.VMEM((tm, k), BF16)] if norm else [],
        compiler_params=_compiler_params(("parallel", "arbitrary")),
        name="proj",
    )(*args)


def _pad_cols(w, width):
    return jnp.pad(w, ((0, 0), (0, width - w.shape[1])))


def _mlstm_kernel(q_ref, k_ref, v_ref, og_ref, g_ref, bias_ref, hg_ref, o_ref, ct_ref, n_ref, m_ref):
    L, H, DK, DV = MLSTM_CHUNK, MLSTM_HEADS, MLSTM_DQK, MLSTM_DV
    R = q_ref.shape[0]

    @pl.when(pl.program_id(1) == 0)
    def _():
        ct_ref[...] = jnp.zeros_like(ct_ref)
        n_ref[...] = jnp.zeros_like(n_ref)
        m_ref[...] = jnp.zeros_like(m_ref)

    row = lax.broadcasted_iota(jnp.int32, (L, L), 0)
    col = lax.broadcasted_iota(jnp.int32, (L, L), 1)
    causal = row >= col
    tril_bf = causal.astype(BF16)

    z = [g_ref[r] + bias_ref[...] for r in range(R)]
    log_f = [jnp.minimum(t, 0.0) - jnp.log(1.0 + jnp.exp(-jnp.abs(t))) for t in z]
    bcum = [_cumsum_time(tril_bf, t) for t in log_f]
    z_t = [t.T for t in z]
    bcum_t = [t.T for t in bcum]

    units = range(R * H)
    rh = [divmod(u, H) for u in units]
    q = [q_ref[r, :, h * DK:(h + 1) * DK] for r, h in rh]
    k = [k_ref[r, :, h * DK:(h + 1) * DK] * (DK ** -0.5) for r, h in rh]
    q_bf = [t.astype(BF16) for t in q]
    k_bf = [t.astype(BF16) for t in k]
    v_bf = [v_ref[r, :, h * DV:(h + 1) * DV].astype(BF16) for r, h in rh]
    m_prev = [m_ref[u] for u in units]

    b_col = [bcum[r][:, H + h:H + h + 1] for r, h in rh]
    g = [bcum[r][L - 1:L, H + h:H + h + 1] for r, h in rh]
    qk = [_dot_nt(q_bf[u], k_bf[u]) for u in units]
    inter = [_dot(q_bf[u], ct_ref[u].astype(BF16)) for u in units]
    qn = [jnp.sum(q[u] * n_ref[u], axis=-1, keepdims=True) for u in units]

    log_d = [jnp.where(causal, b_col[u] - bcum_t[r][H + h:H + h + 1, :] + z_t[r][h:h + 1, :], -jnp.inf)
             for u, (r, h) in enumerate(rh)]
    row_max = [jnp.max(log_d[u], axis=-1, keepdims=True) for u in units]
    log_w = [g[u] - b_col[u] + z[r][:, h:h + 1] for u, (r, h) in enumerate(rh)]
    m_new = [jnp.maximum(g[u] + m_prev[u], jnp.max(log_w[u], axis=0, keepdims=True)) for u in units]
    kw = [k[u] * jnp.exp(log_w[u] - m_new[u]) for u in units]
    outer = [_dot_tn(kw[u].astype(BF16), v_bf[u]) for u in units]

    log_inter = [b_col[u] + m_prev[u] for u in units]
    m_out = [jnp.maximum(log_inter[u], row_max[u]) for u in units]
    s = [qk[u] * jnp.exp(log_d[u] - m_out[u]) for u in units]
    intra = [_dot(s[u].astype(BF16), v_bf[u]) for u in units]
    s_sum = [jnp.sum(s[u], axis=-1, keepdims=True) for u in units]
    w_inter = [jnp.exp(log_inter[u] - m_out[u]) for u in units]
    den = [s_sum[u] + w_inter[u] * qn[u] for u in units]
    scale = [1.0 / jnp.maximum(jnp.abs(den[u]), jnp.exp(-m_out[u])) for u in units]
    hout = [(intra[u] + w_inter[u] * inter[u]) * scale[u] for u in units]
    ms = [jnp.mean(hout[u] * hout[u], axis=-1, keepdims=True) for u in units]
    gate = [_sigmoid(og_ref[r, :, h * DV:(h + 1) * DV]) * hg_ref[:, h * DV:(h + 1) * DV] for r, h in rh]
    for u, (r, h) in enumerate(rh):
        o_ref[r, :, h * DV:(h + 1) * DV] = (hout[u] * lax.rsqrt(ms[u] + NORM_EPS) * gate[u]).astype(o_ref.dtype)

    for u in units:
        decay = jnp.exp(g[u] + m_prev[u] - m_new[u])
        ct_ref[u] = decay * ct_ref[u] + outer[u]
        n_ref[u] = decay * n_ref[u] + jnp.sum(kw[u], axis=0, keepdims=True)
        m_ref[u] = m_new[u]


def _mlstm_mixer(x, norm_gain, w_in_padded, b_gate, head_gain, w_out, layer, *, batch, seq):
    tokens, d = x.shape
    L, H, DK, DV = MLSTM_CHUNK, MLSTM_HEADS, MLSTM_DQK, MLSTM_DV
    R = MLSTM_ROWS if batch % MLSTM_ROWS == 0 else 1
    nc = seq // L
    main = 2 * H * DK + 2 * H * DV
    assert w_in_padded.shape[2] == main + MLSTM_GATE_PAD
    zin = _proj(x, w_in_padded, layer, seq=seq, gain=norm_gain).reshape(batch, seq, main + MLSTM_GATE_PAD)
    bias = jnp.pad(b_gate.reshape(1, 2 * H), ((0, 0), (0, LANES - 2 * H)))
    qk_w, v_w = H * DK, H * DV
    gated = pl.pallas_call(
        _mlstm_kernel,
        grid=(batch // R, nc),
        in_specs=[
            pl.BlockSpec((R, L, qk_w), lambda b, c: (b, c, 0)),
            pl.BlockSpec((R, L, qk_w), lambda b, c: (b, c, 1)),
            pl.BlockSpec((R, L, v_w), lambda b, c: (b, c, 2 * qk_w // v_w)),
            pl.BlockSpec((R, L, v_w), lambda b, c: (b, c, 2 * qk_w // v_w + 1)),
            pl.BlockSpec((R, L, LANES), lambda b, c: (b, c, main // LANES)),
            pl.BlockSpec((1, LANES), lambda b, c: (0, 0)),
            pl.BlockSpec((1, v_w), lambda b, c: (0, 0)),
        ],
        out_specs=pl.BlockSpec((R, L, v_w), lambda b, c: (b, c, 0)),
        out_shape=jax.ShapeDtypeStruct((batch, seq, v_w), BF16),
        scratch_shapes=[
            pltpu.VMEM((R * H, DK, DV), F32),
            pltpu.VMEM((R * H, 1, DK), F32),
            pltpu.VMEM((R * H, 1, 1), F32),
        ],
        compiler_params=_compiler_params(("parallel", "arbitrary")),
        name="mlstm_scan",
    )(zin, zin, zin, zin, zin, bias, head_gain.reshape(1, v_w))
    return _proj(gated.reshape(tokens, v_w), w_out, layer, seq=seq, residual=x)


def _qk_norm_rope(slabs, gain2, tables, lane):
    dh = ATTN_HEAD_DIM
    half = ROPE_DIM // 2
    low = lane % dh < half
    idx = range(len(slabs))
    r_i = lax.broadcasted_iota(jnp.int32, (LANES, LANES), 0)
    c_i = lax.broadcasted_iota(jnp.int32, (LANES, LANES), 1)
    same_head = ((r_i // dh) == (c_i // dh)).astype(BF16)
    sq = [t * t for t in slabs]
    sq_hi = [t.astype(BF16) for t in sq]
    sq_lo = [(sq[i] - sq_hi[i].astype(F32)).astype(BF16) for i in idx]
    ssq = [_dot(sq_hi[i], same_head) + _dot(sq_lo[i], same_head) for i in idx]
    tn = [slabs[i] * lax.rsqrt(ssq[i] * (1.0 / dh) + NORM_EPS) * gain2 for i in idx]
    up = [pltpu.roll(tn[i], LANES - half, 1) for i in idx]
    down = [pltpu.roll(tn[i], half, 1) for i in idx]
    return [tn[i] * tables[i][0] + jnp.where(low, up[i], down[i]) * tables[i][1] for i in idx]


def _swa_kernel(q_ref, kc_ref, kp_ref, vc_ref, vp_ref, pc_ref, pp_ref, freq_ref, qg_ref, kg_ref, sink_ref, o_ref):
    blk, dh, grp = ATTN_WINDOW, ATTN_HEAD_DIM, ATTN_GROUP
    half = ROPE_DIM // 2
    n_kv = kc_ref.shape[1] // dh
    has_prev = pl.program_id(1) > 0

    lane = lax.broadcasted_iota(jnp.int32, (blk, LANES), 1)
    in_rot = lane % dh < ROPE_DIM

    def rope_tables(pos_ref):
        ang = pos_ref[...].astype(F32) * freq_ref[...]
        cos_t = jnp.where(in_rot, jnp.cos(ang), 1.0)
        sin = jnp.sin(ang)
        sin_t = jnp.where(in_rot, jnp.where(lane % dh < half, -sin, sin), 0.0)
        return cos_t, sin_t

    cos_c, sin_c = rope_tables(pc_ref)
    cos_p, sin_p = rope_tables(pp_ref)

    qi = lax.broadcasted_iota(jnp.int32, (blk, 2 * blk), 0) + blk
    kj = lax.broadcasted_iota(jnp.int32, (blk, 2 * blk), 1)
    first_key = jnp.where(has_prev, 0, blk)
    ok = (qi >= kj) & (qi - kj < ATTN_WINDOW) & (kj >= first_key)
    scale = dh ** -0.5
    assert LANES == 2 * dh and scale == 0.125
    first = lane < dh
    first2 = lax.broadcasted_iota(jnp.int32, (2 * blk, LANES), 1) < dh
    key_row = lax.broadcasted_iota(jnp.int32, (2 * blk, LANES), 0)
    half_ones = ((lax.broadcasted_iota(jnp.int32, (4 * blk, LANES), 0) < 2 * blk)
                 == (lax.broadcasted_iota(jnp.int32, (4 * blk, LANES), 1) < dh)).astype(BF16)
    slabs_per_kv = grp * dh // LANES

    k_dup, v_split = [], []
    kv_slabs = n_kv * dh // LANES
    k_in = [kp_ref[:, c * LANES:(c + 1) * LANES] for c in range(kv_slabs)]
    k_in += [kc_ref[:, c * LANES:(c + 1) * LANES] for c in range(kv_slabs)]
    k_roped = _qk_norm_rope(k_in, kg_ref[...], [(cos_p, sin_p)] * kv_slabs + [(cos_c, sin_c)] * kv_slabs, lane)
    q_slabs = q_ref.shape[1] // LANES
    q_roped = _qk_norm_rope([q_ref[:, c * LANES:(c + 1) * LANES] for c in range(q_slabs)], qg_ref[...],
                            [(cos_c, sin_c)] * q_slabs, lane)
    for c in range(kv_slabs):
        sl = slice(c * LANES, (c + 1) * LANES)
        kcat = jnp.concatenate([k_roped[c], k_roped[kv_slabs + c]], axis=0)
        vcat = jnp.concatenate([vp_ref[:, sl], vc_ref[:, sl]], axis=0)
        vcat = jnp.where(key_row == 0, 0.0, vcat)
        k_rot = pltpu.roll(kcat, dh, 1)
        v_rot = pltpu.roll(vcat, dh, 1)
        k_dup.append(jnp.where(first2, kcat, k_rot).astype(BF16))
        k_dup.append(jnp.where(first2, k_rot, kcat).astype(BF16))
        v_split.append(jnp.concatenate([jnp.where(first2, vcat, 0.0), jnp.where(first2, 0.0, v_rot)],
                                       axis=0).astype(BF16))
        v_split.append(jnp.concatenate([jnp.where(first2, v_rot, 0.0), jnp.where(first2, 0.0, vcat)],
                                       axis=0).astype(BF16))

    s_all = []
    for kv in range(n_kv):
        parts = []
        for c in range(kv * slabs_per_kv, (kv + 1) * slabs_per_kv):
            q2 = q_roped[c] * scale
            parts += [jnp.where(first, q2, 0.0), jnp.where(first, 0.0, q2)]
        q_st = jnp.concatenate(parts, axis=0).astype(BF16)
        s_all.append(_dot_nt(q_st, k_dup[kv]))

    heads = range(n_kv * grp)
    s = [jnp.where(ok, s_all[h // grp][(h % grp) * blk:(h % grp + 1) * blk], sink_ref[h:h + 1, :]) for h in heads]
    m = [jnp.max(s[h], axis=-1, keepdims=True) for h in heads]
    probs = [jnp.exp(s[h] - m[h]).astype(BF16) for h in heads]
    slabs = range(q_slabs)
    p_cat = [jnp.concatenate([probs[2 * c], probs[2 * c + 1]], axis=1) for c in slabs]
    outs = [_dot(p_cat[c], v_split[c // slabs_per_kv]) for c in slabs]
    dens = [_dot(p_cat[c], half_ones) for c in slabs]
    for c in slabs:
        o_ref[:, c * LANES:(c + 1) * LANES] = (outs[c] / dens[c]).astype(o_ref.dtype)


def _swa_mixer(x, positions, norm_gain, w_qkv, q_gain, k_gain, sinks, w_o, layer, *, batch, seq):
    tokens, d = x.shape
    blk, dh = ATTN_WINDOW, ATTN_HEAD_DIM
    nb = seq // blk
    n_q = sinks.shape[0]
    q_w = n_q * dh
    kv_w = (w_qkv.shape[2] - q_w) // 2
    qkv = _proj(x, w_qkv, layer, seq=seq, gain=norm_gain)
    pos = positions.reshape(tokens, 1)
    inv_freq = ROPE_THETA ** (-jnp.arange(0, ROPE_DIM, 2, dtype=F32) / ROPE_DIM)
    freq_head = jnp.concatenate([inv_freq, inv_freq, jnp.zeros((dh - ROPE_DIM,), F32)])
    freq = jnp.tile(freq_head, LANES // dh).reshape(1, LANES)
    tile2 = lambda g: jnp.tile(g, LANES // dh).reshape(1, LANES)
    sink_rows = jnp.concatenate([sinks.reshape(n_q, 1), jnp.full((n_q, 2 * blk - 1), -jnp.inf, F32)], axis=1)
    cur = lambda b, i: b * nb + i
    prev = lambda b, i: b * nb + jnp.maximum(i - 1, 0)
    k_blk, v_blk = q_w // kv_w, q_w // kv_w + 1
    small = lambda b, i: (0, 0)
    attn = pl.pallas_call(
        _swa_kernel,
        grid=(batch, nb),
        in_specs=[
            pl.BlockSpec((blk, q_w), lambda b, i: (cur(b, i), 0)),
            pl.BlockSpec((blk, kv_w), lambda b, i: (cur(b, i), k_blk)),
            pl.BlockSpec((blk, kv_w), lambda b, i: (prev(b, i), k_blk)),
            pl.BlockSpec((blk, kv_w), lambda b, i: (cur(b, i), v_blk)),
            pl.BlockSpec((blk, kv_w), lambda b, i: (prev(b, i), v_blk)),
            pl.BlockSpec((blk, 1), lambda b, i: (cur(b, i), 0)),
            pl.BlockSpec((blk, 1), lambda b, i: (prev(b, i), 0)),
            pl.BlockSpec((1, LANES), small),
            pl.BlockSpec((1, LANES), small),
            pl.BlockSpec((1, LANES), small),
            pl.BlockSpec((n_q, 2 * blk), small),
        ],
        out_specs=pl.BlockSpec((blk, q_w), lambda b, i: (cur(b, i), 0)),
        out_shape=jax.ShapeDtypeStruct((tokens, q_w), BF16),
        compiler_params=_compiler_params(("parallel", "arbitrary")),
        name="swa",
    )(qkv, qkv, qkv, qkv, qkv, pos, pos, freq, tile2(q_gain), tile2(k_gain), sink_rows)
    return _proj(attn, w_o, layer, seq=seq, residual=x)


def _rwkv_proj_kernel(x_ref, xb_ref, gain_ref, mix_ref, wl_ref, w_ref, o_ref, lo_ref, xm_ref, *, per_proj, seq):
    j = pl.program_id(1)
    tm, d = x_ref.shape
    pad = RWKV_LORA_PAD

    @pl.when(j == 0)
    def _():
        gain = gain_ref[...]
        rc = min(RWKV_PROLOGUE_ROWS, tm)
        row = lax.broadcasted_iota(jnp.int32, (rc, d), 0)
        starts_sequence = (pl.program_id(0) * tm) % seq == 0
        for c in range(tm // rc):
            rows = slice(c * rc, (c + 1) * rc)
            hn = _rmsnorm_rows(x_ref[rows, :], gain)
            if c == 0:
                before = jnp.where(starts_sequence, 0.0, xb_ref[SUBLANES - 1:SUBLANES, :])
            else:
                before = x_ref[c * rc - 1:c * rc, :]
            dx = jnp.where(row == 0, _rmsnorm_rows(before, gain), pltpu.roll(hn, 1, 0)) - hn

            def mixed(i):
                return (hn + dx * mix_ref[i:i + 1, :]).astype(BF16)

            xm_ref[0, rows, :] = mixed(0)
            xm_ref[1, rows, :] = mixed(2)
            xm_ref[2, rows, :] = mixed(3)
            lo_ref[rows, 0:pad] = _dot(mixed(1), wl_ref[:, 0:pad])
            lo_ref[rows, pad:2 * pad] = _dot(mixed(4), wl_ref[:, pad:2 * pad])
            lo_ref[rows, 2 * pad:] = _dot(mixed(5), wl_ref[:, 2 * pad:])

    o_ref[...] = _dot(xm_ref[j // per_proj], w_ref[...])


def _rwkv_scan_kernel(r_ref, k_ref, v_ref, lo_ref, w0_ref, wlb_ref, a0_ref, alb_ref, glb_ref,
                      kk_ref, ka_ref, rk_ref, lnw_ref, lnb_ref, o_ref, st_ref):
    C, N = RWKV_CHUNK, RWKV_HEAD
    pad = RWKV_LORA_PAD
    n_pairs = r_ref.shape[1] // LANES

    @pl.when(pl.program_id(1) == 0)
    def _():
        st_ref[...] = jnp.zeros_like(st_ref)

    lane = lax.broadcasted_iota(jnp.int32, (C, LANES), 1)
    first = lane < N
    row_c = lax.broadcasted_iota(jnp.int32, (C, C), 0)
    col_c = lax.broadcasted_iota(jnp.int32, (C, C), 1)
    tril_bf = (row_c >= col_c).astype(BF16)
    row2 = lax.broadcasted_iota(jnp.int32, (2 * C, 2 * C), 0)
    col2 = lax.broadcasted_iota(jnp.int32, (2 * C, 2 * C), 1)
    same_head = (row2 // C) == (col2 // C)
    strict_lower = same_head & (row2 % C > col2 % C)
    lower = same_head & (row2 % C >= col2 % C)
    eye2 = row2 == col2

    def head_sum(t):
        s0 = jnp.sum(jnp.where(first, t, 0.0), axis=-1, keepdims=True)
        s1 = jnp.sum(jnp.where(first, 0.0, t), axis=-1, keepdims=True)
        return jnp.where(first, s0, s1)

    def stack(t):
        return jnp.concatenate([jnp.where(first, t, 0.0), jnp.where(first, 0.0, t)], axis=0)

    lw1 = jnp.tanh(lo_ref[:, 0:pad]).astype(BF16)
    la1 = lo_ref[:, pad:2 * pad].astype(BF16)
    lg1 = _sigmoid(lo_ref[:, 2 * pad:]).astype(BF16)

    r_all = r_ref[...]
    k_all = k_ref[...]
    zw = w0_ref[...] + _dot(lw1, wlb_ref[...])
    softplus = jnp.maximum(-zw, 0.0) + jnp.log(1.0 + jnp.exp(-jnp.abs(zw)))
    log_decay = -jnp.exp(-softplus - 0.5)
    a_all = _sigmoid(a0_ref[...] + _dot(la1, alb_ref[...]))
    g_all = _dot(lg1, glb_ref[...])
    kk_all = k_all * kk_ref[...]
    k2_all = k_all * (1.0 + (a_all - 1.0) * ka_ref[...])
    cum_all = _cumsum_time(tril_bf, log_decay)
    total_all = cum_all[C - 1:C, :]
    p_incl = jnp.exp(cum_all)
    p_excl = jnp.exp(cum_all - log_decay)
    inv_p = jnp.exp(-cum_all)
    to_end = jnp.exp(total_all - cum_all)
    rt_all = r_all * p_incl
    rk2_all = r_all * k2_all * rk_ref[...]

    pairs = range(n_pairs)
    sls = [slice(p * LANES, (p + 1) * LANES) for p in pairs]

    lhs, rhs, v_sts, ends = [], [], [], []
    for p in pairs:
        sl = sls[p]
        kk = kk_all[:, sl]
        kk = kk / jnp.maximum(jnp.sqrt(head_sum(kk * kk)), 1e-12)
        kka = kk * a_all[:, sl]
        a_t = -kk * p_excl[:, sl]
        b_t = kka * inv_p[:, sl]
        k_t = k2_all[:, sl] * inv_p[:, sl]
        b_e = kka * to_end[:, sl]
        k_e = k2_all[:, sl] * to_end[:, sl]
        lhs.append(jnp.concatenate([stack(a_t), stack(rt_all[:, sl])], axis=0).astype(BF16))
        rhs.append(jnp.concatenate([stack(b_t), stack(k_t)], axis=0).astype(BF16))
        ends.append(jnp.concatenate([stack(b_e), stack(k_e)], axis=0).astype(BF16))
        v_sts.append(stack(v_ref[:, sl]).astype(BF16))

    grams = [_dot_nt(lhs[p], rhs[p]) for p in pairs]
    h2 = 2 * C
    a_ak = [jnp.where(strict_lower, grams[p][:h2, h2:], 0.0).astype(BF16) for p in pairs]
    a_rbk = [jnp.concatenate([jnp.where(lower, grams[p][h2:, :h2], 0.0),
                              jnp.where(lower, grams[p][h2:, h2:], 0.0)], axis=1).astype(BF16) for p in pairs]

    pw = [jnp.where(strict_lower, grams[p][:h2, :h2], 0.0) for p in pairs]
    t_inv = [jnp.where(eye2, 1.0, 0.0) + pw[p] for p in pairs]
    for _ in range(C.bit_length() - 2):
        pw_bf = [pw[p].astype(BF16) for p in pairs]
        pw = [_dot(pw_bf[p], pw_bf[p]) for p in pairs]
        t_inv = [t_inv[p] + _dot(t_inv[p].astype(BF16), pw[p].astype(BF16)) for p in pairs]

    m0 = [st_ref[p] for p in pairs]
    m0_bf = [m0[p].astype(BF16) for p in pairs]
    lm = [_dot(lhs[p], m0_bf[p]) for p in pairs]
    x_rhs = [lm[p][:h2] + _dot(a_ak[p], v_sts[p]) for p in pairs]
    u_bf = [_dot(t_inv[p].astype(BF16), x_rhs[p].astype(BF16)).astype(BF16) for p in pairs]
    uv = [jnp.concatenate([u_bf[p], v_sts[p]], axis=0) for p in pairs]
    y_st = [lm[p][h2:] + _dot(a_rbk[p], uv[p]) for p in pairs]

    for p in pairs:
        total_col = jnp.sum(jnp.where(eye2, total_all[:, sls[p]], 0.0), axis=-1, keepdims=True)
        st_ref[p] = jnp.exp(total_col) * m0[p] + _dot_tn(ends[p], uv[p])

    for p in pairs:
        sl = sls[p]
        y = y_st[p][:C] + y_st[p][C:]
        mu = head_sum(y) * (1.0 / N)
        yc = y - mu
        var = head_sum(yc * yc) * (1.0 / N)
        yn = yc * lax.rsqrt(var + RWKV_LN_EPS) * lnw_ref[:, sl] + lnb_ref[:, sl]
        bonus = head_sum(rk2_all[:, sl])
        o_ref[:, sl] = ((yn + bonus * v_ref[:, sl]) * g_all[:, sl]).astype(o_ref.dtype)


def _rwkv_mixer(x, norm_gain, mix, w_rkv, w0, w_la, w_lb, a0, a_la, a_lb, g_la, g_lb, k_k, k_a, r_k,
                ln_w, ln_b, w_o, layer, *, batch, seq):
    tokens, d = x.shape
    C = RWKV_CHUNK
    pad = RWKV_LORA_PAD
    tm = _row_tile(seq, 1024)
    tn = 512
    per_proj = d // tn
    n_tiles = tokens // tm
    lora_w = w_la.shape[1], a_la.shape[1], g_la.shape[1]
    lora_cols = 2 * pad + lora_w[2]
    assert lora_w[0] <= pad and lora_w[1] <= pad and 2 * C == LANES
    w_lora = jnp.concatenate([_pad_cols(w_la, pad), _pad_cols(a_la, pad), g_la], axis=1).astype(BF16)
    proj, lora = pl.pallas_call(
        functools.partial(_rwkv_proj_kernel, per_proj=per_proj, seq=seq),
        grid=(n_tiles, 3 * per_proj),
        in_specs=[
            pl.BlockSpec((tm, d), lambda i, j: (i, 0)),
            pl.BlockSpec((SUBLANES, d), lambda i, j: (jnp.maximum(i * (tm // SUBLANES) - 1, 0), 0)),
            pl.BlockSpec((1, d), lambda i, j: (0, 0)),
            pl.BlockSpec((6, d), lambda i, j: (0, 0)),
            pl.BlockSpec((d, lora_cols), lambda i, j: (0, 0)),
            pl.BlockSpec((None, d, tn), lambda i, j: (j // per_proj, 0, j % per_proj)),
        ],
        out_specs=[
            pl.BlockSpec((tm, tn), lambda i, j: (i, j)),
            pl.BlockSpec((tm, lora_cols), lambda i, j: (i, 0)),
        ],
        out_shape=[
            jax.ShapeDtypeStruct((tokens, 3 * d), F32),
            jax.ShapeDtypeStruct((tokens, lora_cols), F32),
        ],
        scratch_shapes=[pltpu.VMEM((3, tm, d), BF16)],
        compiler_params=_compiler_params(("parallel", "arbitrary")),
        name="rwkv_proj",
    )(x, x, norm_gain.reshape(1, d), mix, w_lora, w_rkv.astype(BF16))

    nc = seq // C
    row1 = lambda t: t.reshape(1, d)
    pad_rows = lambda w: jnp.pad(w, ((0, pad - w.shape[0]), (0, 0))).astype(BF16)
    chunk = lambda blk: (lambda b, c: (b * nc + c, blk))
    const = lambda b, c: (0, 0)
    vec = pl.BlockSpec((1, d), const)
    mixed = pl.pallas_call(
        _rwkv_scan_kernel,
        grid=(batch, nc),
        in_specs=[
            pl.BlockSpec((C, d), chunk(0)),
            pl.BlockSpec((C, d), chunk(1)),
            pl.BlockSpec((C, d), chunk(2)),
            pl.BlockSpec((C, lora_cols), chunk(0)),
            vec,
            pl.BlockSpec((pad, d), const),
            vec,
            pl.BlockSpec((pad, d), const),
            pl.BlockSpec((lora_w[2], d), const),
            vec, vec, vec, vec, vec,
        ],
        out_specs=pl.BlockSpec((C, d), chunk(0)),
        out_shape=jax.ShapeDtypeStruct((tokens, d), BF16),
        scratch_shapes=[pltpu.VMEM((d // LANES, LANES, LANES), F32)],
        compiler_params=_compiler_params(("parallel", "arbitrary")),
        name="rwkv_scan",
    )(proj, proj, proj, lora, row1(w0), pad_rows(w_lb), row1(a0), pad_rows(a_lb), g_lb.astype(BF16),
      row1(k_k), row1(k_a), row1(r_k), row1(ln_w), row1(ln_b))
    return _proj(mixed, w_o, layer, seq=seq, residual=x)


def kernel(x, positions, ffn1_norm, ffn1_w_gu, ffn1_w_down, mixer_norm, ffn2_norm, ffn2_w_gu, ffn2_w_down,
           mlstm_w_in, mlstm_b_gate, mlstm_head_gain, mlstm_w_out,
           attn_w_qkv, attn_q_gain, attn_k_gain, attn_sinks, attn_w_o,
           rwkv_mix, rwkv_w_rkv, rwkv_w0, rwkv_w_lora_a, rwkv_w_lora_b, rwkv_a0, rwkv_a_lora_a, rwkv_a_lora_b,
           rwkv_g_lora_a, rwkv_g_lora_b, rwkv_k_k, rwkv_k_a, rwkv_r_k, rwkv_ln_w, rwkv_ln_b, rwkv_w_o):
    batch, seq, d = x.shape
    depth = ffn1_norm.shape[0]
    h = x.reshape(batch * seq, d)
    ffn1_gu, ffn1_down = ffn1_w_gu.astype(BF16), ffn1_w_down.astype(BF16)
    ffn2_gu, ffn2_down = ffn2_w_gu.astype(BF16), ffn2_w_down.astype(BF16)
    mlstm_in = jnp.pad(mlstm_w_in, ((0, 0), (0, 0), (0, MLSTM_GATE_PAD - 2 * MLSTM_HEADS))).astype(BF16)
    mlstm_out = mlstm_w_out.astype(BF16)
    attn_qkv, attn_o = attn_w_qkv.astype(BF16), attn_w_o.astype(BF16)
    rwkv_o = rwkv_w_o.astype(BF16)
    for i in range(depth):
        h = _ffn(h, ffn1_norm[i], ffn1_gu, ffn1_down, i, seq=seq)
        kind, j = i % 3, i // 3
        if kind == 0:
            h = _mlstm_mixer(h, mixer_norm[i], mlstm_in, mlstm_b_gate[j], mlstm_head_gain[j], mlstm_out, j,
                             batch=batch, seq=seq)
        elif kind == 1:
            h = _swa_mixer(h, positions, mixer_norm[i], attn_qkv, attn_q_gain[j], attn_k_gain[j],
                           attn_sinks[j], attn_o, j, batch=batch, seq=seq)
        else:
            h = _rwkv_mixer(h, mixer_norm[i], rwkv_mix[j], rwkv_w_rkv[j], rwkv_w0[j], rwkv_w_lora_a[j],
                            rwkv_w_lora_b[j], rwkv_a0[j], rwkv_a_lora_a[j], rwkv_a_lora_b[j], rwkv_g_lora_a[j],
                            rwkv_g_lora_b[j], rwkv_k_k[j], rwkv_k_a[j], rwkv_r_k[j], rwkv_ln_w[j], rwkv_ln_b[j],
                            rwkv_o, j, batch=batch, seq=seq)
        h = _ffn(h, ffn2_norm[i], ffn2_gu, ffn2_down, i, seq=seq)
    return h.reshape(batch, seq, d)
```

```python
import functools
import math

import jax
import jax.numpy as jnp
from jax import lax
from jax.experimental import pallas as pl
from jax.experimental.pallas import tpu as pltpu

F32 = jnp.float32
BF16 = jnp.bfloat16

NORM_EPS = 1e-6

MLSTM_HEADS = 8
MLSTM_DQK = 128
MLSTM_DV = 256
MLSTM_CHUNK = 64
MLSTM_GATE_PAD = 512
MLSTM_ROWS = 4

ATTN_HEAD_DIM = 64
ATTN_GROUP = 8
ATTN_WINDOW = 128
ROPE_DIM = 16
ROPE_THETA = 500000.0

RWKV_HEAD = 64
RWKV_CHUNK = 64
RWKV_LN_EPS = 64e-5
RWKV_LORA_PAD = 128
RWKV_PAIR_GROUP = 16
RWKV_PROLOGUE_ROWS = 256

LANES = 128
SUBLANES = 8
VMEM_LIMIT_BYTES = 56 * 1024 * 1024


def _compiler_params(semantics):
    return pltpu.CompilerParams(dimension_semantics=semantics, vmem_limit_bytes=VMEM_LIMIT_BYTES)


def _row_tile(seq, preferred):
    tile = preferred
    while seq % tile:
        tile //= 2
    return tile


def _dot(a, b):
    return jnp.dot(a, b, preferred_element_type=F32)


def _dot_nt(a, b):
    return lax.dot_general(a, b, (((1,), (1,)), ((), ())), preferred_element_type=F32)


def _dot_tn(a, b):
    return lax.dot_general(a, b, (((0,), (0,)), ((), ())), preferred_element_type=F32)


def _sigmoid(x):
    return 1.0 / (1.0 + jnp.exp(-x))


def _rmsnorm_rows(x, gain):
    ms = jnp.mean(x * x, axis=-1, keepdims=True)
    return x * lax.rsqrt(ms + NORM_EPS) * gain


def _split3(x):
    hi = x.astype(BF16)
    r1 = x - hi.astype(F32)
    mid = r1.astype(BF16)
    lo = (r1 - mid.astype(F32)).astype(BF16)
    return hi, mid, lo


def _cumsum_time(tril_bf, x):
    hi, mid, lo = _split3(x)
    return _dot(tril_bf, hi) + _dot(tril_bf, mid) + _dot(tril_bf, lo)


def _ffn_kernel(x_ref, gain_ref, wg_ref, wu_ref, wd_ref, o_ref, xn_ref):
    @pl.when(pl.program_id(1) == 0)
    def _():
        x = x_ref[...]
        xn_ref[...] = _rmsnorm_rows(x, gain_ref[...]).astype(BF16)
        o_ref[...] = x

    xn = xn_ref[...]
    gate = _dot(xn, wg_ref[...])
    up = _dot(xn, wu_ref[...])
    act = (gate * _sigmoid(gate) * (0.5 * up)).astype(BF16)
    o_ref[...] += _dot(act, wd_ref[...])


def _ffn(x, gain, w_gu, w_down, layer, *, seq):
    tokens, d = x.shape
    f = w_down.shape[1]
    tm = _row_tile(seq, 1024)
    tf = _col_tile(f, 768)
    nf = f // tf
    return pl.pallas_call(
        _ffn_kernel,
        grid=(tokens // tm, nf),
        in_specs=[
            pl.BlockSpec((tm, d), lambda i, j: (i, 0)),
            pl.BlockSpec((1, d), lambda i, j: (0, 0)),
            pl.BlockSpec((None, d, tf), lambda i, j: (layer, 0, j)),
            pl.BlockSpec((None, d, tf), lambda i, j: (layer, 0, j + nf)),
            pl.BlockSpec((None, tf, d), lambda i, j: (layer, j, 0)),
        ],
        out_specs=pl.BlockSpec((tm, d), lambda i, j: (i, 0)),
        out_shape=jax.ShapeDtypeStruct((tokens, d), F32),
        scratch_shapes=[pltpu.VMEM((tm, d), BF16)],
        compiler_params=_compiler_params(("parallel", "arbitrary")),
        name="ffn",
    )(x, gain.reshape(1, d), w_gu, w_gu, w_down)


def _proj_kernel(*refs, norm, residual):
    refs = list(refs)
    x_ref = refs.pop(0)
    gain_ref = refs.pop(0) if norm else None
    w_ref = refs.pop(0)
    res_ref = refs.pop(0) if residual else None
    o_ref = refs.pop(0)

    if norm:
        xn_ref = refs.pop(0)

        @pl.when(pl.program_id(1) == 0)
        def _():
            xn_ref[...] = _rmsnorm_rows(x_ref[...], gain_ref[...]).astype(BF16)

        lhs = xn_ref[...]
    else:
        lhs = x_ref[...]

    acc = _dot(lhs, w_ref[...])
    if residual:
        acc = acc + res_ref[...]
    o_ref[...] = acc


def _col_tile(n, cap):
    best = LANES
    for t in range(LANES, cap + 1, LANES):
        if n % t == 0:
            best = t
    return best


def _proj(x, w, layer, *, seq, gain=None, residual=None):
    tokens, k = x.shape
    n = w.shape[2]
    norm = gain is not None
    assert norm or x.dtype == BF16
    tm = _row_tile(seq, 1024 if norm else 512)
    tn = _col_tile(n, 1024 if norm else 2048)
    in_specs = [pl.BlockSpec((tm, k), lambda i, j: (i, 0))]
    args = [x]
    if norm:
        in_specs.append(pl.BlockSpec((1, k), lambda i, j: (0, 0)))
        args.append(gain.reshape(1, k))
    in_specs.append(pl.BlockSpec((None, k, tn), lambda i, j: (layer, 0, j)))
    args.append(w)
    if residual is not None:
        in_specs.append(pl.BlockSpec((tm, tn), lambda i, j: (i, j)))
        args.append(residual)
    return pl.pallas_call(
        functools.partial(_proj_kernel, norm=norm, residual=residual is not None),
        grid=(tokens // tm, n // tn),
        in_specs=in_specs,
        out_specs=pl.BlockSpec((tm, tn), lambda i, j: (i, j)),
        out_shape=jax.ShapeDtypeStruct((tokens, n), F32),
        scratch_shapes=[pltpu.VMEM((tm, k), BF16)] if norm else [],
        compiler_params=_compiler_params(("parallel", "arbitrary")),
        name="proj",
    )(*args)


def _pad_cols(w, width):
    return jnp.pad(w, ((0, 0), (0, width - w.shape[1])))


def _mlstm_kernel(q_ref, k_ref, v_ref, og_ref, g_ref, bias_ref, hg_ref, o_ref, ct_ref, n_ref, m_ref):
    L, H, DK, DV = MLSTM_CHUNK, MLSTM_HEADS, MLSTM_DQK, MLSTM_DV
    R = q_ref.shape[0]

    @pl.when(pl.program_id(1) == 0)
    def _():
        ct_ref[...] = jnp.zeros_like(ct_ref)
        n_ref[...] = jnp.zeros_like(n_ref)
        m_ref[...] = jnp.zeros_like(m_ref)

    row = lax.broadcasted_iota(jnp.int32, (L, L), 0)
    col = lax.broadcasted_iota(jnp.int32, (L, L), 1)
    causal = row >= col
    tril_bf = causal.astype(BF16)

    z = [g_ref[r] + bias_ref[...] for r in range(R)]
    log_f = [jnp.minimum(t, 0.0) - jnp.log(1.0 + jnp.exp(-jnp.abs(t))) for t in z]
    bcum = [_cumsum_time(tril_bf, t) for t in log_f]
    z_t = [t.T for t in z]
    bcum_t = [t.T for t in bcum]

    units = range(R * H)
    rh = [divmod(u, H) for u in units]
    q = [q_ref[r, :, h * DK:(h + 1) * DK] for r, h in rh]
    k = [k_ref[r, :, h * DK:(h + 1) * DK] * (DK ** -0.5) for r, h in rh]
    q_bf = [t.astype(BF16) for t in q]
    k_bf = [t.astype(BF16) for t in k]
    v_bf = [v_ref[r, :, h * DV:(h + 1) * DV].astype(BF16) for r, h in rh]
    m_prev = [m_ref[u] for u in units]

    b_col = [bcum[r][:, H + h:H + h + 1] for r, h in rh]
    g = [bcum[r][L - 1:L, H + h:H + h + 1] for r, h in rh]
    qk = [_dot_nt(q_bf[u], k_bf[u]) for u in units]
    inter = [_dot(q_bf[u], ct_ref[u].astype(BF16)) for u in units]
    qn = [jnp.sum(q[u] * n_ref[u], axis=-1, keepdims=True) for u in units]

    log_d = [jnp.where(causal, b_col[u] - bcum_t[r][H + h:H + h + 1, :] + z_t[r][h:h + 1, :], -jnp.inf)
             for u, (r, h) in enumerate(rh)]
    row_max = [jnp.max(log_d[u], axis=-1, keepdims=True) for u in units]
    log_w = [g[u] - b_col[u] + z[r][:, h:h + 1] for u, (r, h) in enumerate(rh)]
    m_new = [jnp.maximum(g[u] + m_prev[u], jnp.max(log_w[u], axis=0, keepdims=True)) for u in units]
    kw = [k[u] * jnp.exp(log_w[u] - m_new[u]) for u in units]
    outer = [_dot_tn(kw[u].astype(BF16), v_bf[u]) for u in units]

    log_inter = [b_col[u] + m_prev[u] for u in units]
    m_out = [jnp.maximum(log_inter[u], row_max[u]) for u in units]
    s = [qk[u] * jnp.exp(log_d[u] - m_out[u]) for u in units]
    intra = [_dot(s[u].astype(BF16), v_bf[u]) for u in units]
    s_sum = [jnp.sum(s[u], axis=-1, keepdims=True) for u in units]
    w_inter = [jnp.exp(log_inter[u] - m_out[u]) for u in units]
    den = [s_sum[u] + w_inter[u] * qn[u] for u in units]
    scale = [1.0 / jnp.maximum(jnp.abs(den[u]), jnp.exp(-m_out[u])) for u in units]
    hout = [(intra[u] + w_inter[u] * inter[u]) * scale[u] for u in units]
    ms = [jnp.mean(hout[u] * hout[u], axis=-1, keepdims=True) for u in units]
    gate = [_sigmoid(og_ref[r, :, h * DV:(h + 1) * DV]) * hg_ref[:, h * DV:(h + 1) * DV] for r, h in rh]
    for u, (r, h) in enumerate(rh):
        o_ref[r, :, h * DV:(h + 1) * DV] = (hout[u] * lax.rsqrt(ms[u] + NORM_EPS) * gate[u]).astype(o_ref.dtype)

    for u in units:
        decay = jnp.exp(g[u] + m_prev[u] - m_new[u])
        ct_ref[u] = decay * ct_ref[u] + outer[u]
        n_ref[u] = decay * n_ref[u] + jnp.sum(kw[u], axis=0, keepdims=True)
        m_ref[u] = m_new[u]


def _mlstm_mixer(x, norm_gain, w_in_padded, b_gate, head_gain, w_out, layer, *, batch, seq):
    tokens, d = x.shape
    L, H, DK, DV = MLSTM_CHUNK, MLSTM_HEADS, MLSTM_DQK, MLSTM_DV
    R = MLSTM_ROWS if batch % MLSTM_ROWS == 0 else 1
    nc = seq // L
    main = 2 * H * DK + 2 * H * DV
    assert w_in_padded.shape[2] == main + MLSTM_GATE_PAD
    zin = _proj(x, w_in_padded, layer, seq=seq, gain=norm_gain).reshape(batch, seq, main + MLSTM_GATE_PAD)
    bias = jnp.pad(b_gate.reshape(1, 2 * H), ((0, 0), (0, LANES - 2 * H)))
    qk_w, v_w = H * DK, H * DV
    gated = pl.pallas_call(
        _mlstm_kernel,
        grid=(batch // R, nc),
        in_specs=[
            pl.BlockSpec((R, L, qk_w), lambda b, c: (b, c, 0)),
            pl.BlockSpec((R, L, qk_w), lambda b, c: (b, c, 1)),
            pl.BlockSpec((R, L, v_w), lambda b, c: (b, c, 2 * qk_w // v_w)),
            pl.BlockSpec((R, L, v_w), lambda b, c: (b, c, 2 * qk_w // v_w + 1)),
            pl.BlockSpec((R, L, LANES), lambda b, c: (b, c, main // LANES)),
            pl.BlockSpec((1, LANES), lambda b, c: (0, 0)),
            pl.BlockSpec((1, v_w), lambda b, c: (0, 0)),
        ],
        out_specs=pl.BlockSpec((R, L, v_w), lambda b, c: (b, c, 0)),
        out_shape=jax.ShapeDtypeStruct((batch, seq, v_w), BF16),
        scratch_shapes=[
            pltpu.VMEM((R * H, DK, DV), F32),
            pltpu.VMEM((R * H, 1, DK), F32),
            pltpu.VMEM((R * H, 1, 1), F32),
        ],
        compiler_params=_compiler_params(("parallel", "arbitrary")),
        name="mlstm_scan",
    )(zin, zin, zin, zin, zin, bias, head_gain.reshape(1, v_w))
    return _proj(gated.reshape(tokens, v_w), w_out, layer, seq=seq, residual=x)


def _qk_norm_rope(slabs, gain2, tables, lane):
    dh = ATTN_HEAD_DIM
    half = ROPE_DIM // 2
    low = lane % dh < half
    idx = range(len(slabs))
    r_i = lax.broadcasted_iota(jnp.int32, (LANES, LANES), 0)
    c_i = lax.broadcasted_iota(jnp.int32, (LANES, LANES), 1)
    same_head = ((r_i // dh) == (c_i // dh)).astype(BF16)
    sq = [t * t for t in slabs]
    sq_hi = [t.astype(BF16) for t in sq]
    sq_lo = [(sq[i] - sq_hi[i].astype(F32)).astype(BF16) for i in idx]
    ssq = [_dot(sq_hi[i], same_head) + _dot(sq_lo[i], same_head) for i in idx]
    tn = [slabs[i] * lax.rsqrt(ssq[i] * (1.0 / dh) + NORM_EPS) * gain2 for i in idx]
    up = [pltpu.roll(tn[i], LANES - half, 1) for i in idx]
    down = [pltpu.roll(tn[i], half, 1) for i in idx]
    return [tn[i] * tables[i][0] + jnp.where(low, up[i], down[i]) * tables[i][1] for i in idx]


def _swa_kernel(q_ref, kc_ref, kp_ref, vc_ref, vp_ref, pc_ref, pp_ref, freq_ref, qg_ref, kg_ref, sink_ref, o_ref):
    blk, dh, grp = ATTN_WINDOW, ATTN_HEAD_DIM, ATTN_GROUP
    half = ROPE_DIM // 2
    n_kv = kc_ref.shape[1] // dh
    has_prev = pl.program_id(1) > 0

    lane = lax.broadcasted_iota(jnp.int32, (blk, LANES), 1)
    in_rot = lane % dh < ROPE_DIM

    def rope_tables(pos_ref):
        ang = pos_ref[...].astype(F32) * freq_ref[...]
        cos_t = jnp.where(in_rot, jnp.cos(ang), 1.0)
        sin = jnp.sin(ang)
        sin_t = jnp.where(in_rot, jnp.where(lane % dh < half, -sin, sin), 0.0)
        return cos_t, sin_t

    cos_c, sin_c = rope_tables(pc_ref)
    cos_p, sin_p = rope_tables(pp_ref)

    qi = lax.broadcasted_iota(jnp.int32, (blk, 2 * blk), 0) + blk
    kj = lax.broadcasted_iota(jnp.int32, (blk, 2 * blk), 1)
    first_key = jnp.where(has_prev, 0, blk)
    ok = (qi >= kj) & (qi - kj < ATTN_WINDOW) & (kj >= first_key)
    scale = dh ** -0.5
    assert LANES == 2 * dh and scale == 0.125
    first = lane < dh
    first2 = lax.broadcasted_iota(jnp.int32, (2 * blk, LANES), 1) < dh
    key_row = lax.broadcasted_iota(jnp.int32, (2 * blk, LANES), 0)
    half_ones = ((lax.broadcasted_iota(jnp.int32, (4 * blk, LANES), 0) < 2 * blk)
                 == (lax.broadcasted_iota(jnp.int32, (4 * blk, LANES), 1) < dh)).astype(BF16)
    slabs_per_kv = grp * dh // LANES

    k_dup, v_split = [], []
    kv_slabs = n_kv * dh // LANES
    k_in = [kp_ref[:, c * LANES:(c + 1) * LANES] for c in range(kv_slabs)]
    k_in += [kc_ref[:, c * LANES:(c + 1) * LANES] for c in range(kv_slabs)]
    k_roped = _qk_norm_rope(k_in, kg_ref[...], [(cos_p, sin_p)] * kv_slabs + [(cos_c, sin_c)] * kv_slabs, lane)
    q_slabs = q_ref.shape[1] // LANES
    q_roped = _qk_norm_rope([q_ref[:, c * LANES:(c + 1) * LANES] for c in range(q_slabs)], qg_ref[...],
                            [(cos_c, sin_c)] * q_slabs, lane)
    for c in range(kv_slabs):
        sl = slice(c * LANES, (c + 1) * LANES)
        kcat = jnp.concatenate([k_roped[c], k_roped[kv_slabs + c]], axis=0)
        vcat = jnp.concatenate([vp_ref[:, sl], vc_ref[:, sl]], axis=0)
        vcat = jnp.where(key_row == 0, 0.0, vcat)
        k_rot = pltpu.roll(kcat, dh, 1)
        v_rot = pltpu.roll(vcat, dh, 1)
        k_dup.append(jnp.where(first2, kcat, k_rot).astype(BF16))
        k_dup.append(jnp.where(first2, k_rot, kcat).astype(BF16))
        v_split.append(jnp.concatenate([jnp.where(first2, vcat, 0.0), jnp.where(first2, 0.0, v_rot)],
                                       axis=0).astype(BF16))
        v_split.append(jnp.concatenate([jnp.where(first2, v_rot, 0.0), jnp.where(first2, 0.0, vcat)],
                                       axis=0).astype(BF16))

    s_all = []
    for kv in range(n_kv):
        parts = []
        for c in range(kv * slabs_per_kv, (kv + 1) * slabs_per_kv):
            q2 = q_roped[c] * scale
            parts += [jnp.where(first, q2, 0.0), jnp.where(first, 0.0, q2)]
        q_st = jnp.concatenate(parts, axis=0).astype(BF16)
        s_all.append(_dot_nt(q_st, k_dup[kv]))

    heads = range(n_kv * grp)
    s = [jnp.where(ok, s_all[h // grp][(h % grp) * blk:(h % grp + 1) * blk], sink_ref[h:h + 1, :]) for h in heads]
    m = [jnp.max(s[h], axis=-1, keepdims=True) for h in heads]
    probs = [jnp.exp(s[h] - m[h]).astype(BF16) for h in heads]
    slabs = range(q_slabs)
    p_cat = [jnp.concatenate([probs[2 * c], probs[2 * c + 1]], axis=1) for c in slabs]
    outs = [_dot(p_cat[c], v_split[c // slabs_per_kv]) for c in slabs]
    dens = [_dot(p_cat[c], half_ones) for c in slabs]
    for c in slabs:
        o_ref[:, c * LANES:(c + 1) * LANES] = (outs[c] / dens[c]).astype(o_ref.dtype)


def _swa_mixer(x, positions, norm_gain, w_qkv, q_gain, k_gain, sinks, w_o, layer, *, batch, seq):
    tokens, d = x.shape
    blk, dh = ATTN_WINDOW, ATTN_HEAD_DIM
    nb = seq // blk
    n_q = sinks.shape[0]
    q_w = n_q * dh
    kv_w = (w_qkv.shape[2] - q_w) // 2
    qkv = _proj(x, w_qkv, layer, seq=seq, gain=norm_gain)
    pos = positions.reshape(tokens, 1)
    inv_freq = ROPE_THETA ** (-jnp.arange(0, ROPE_DIM, 2, dtype=F32) / ROPE_DIM)
    freq_head = jnp.concatenate([inv_freq, inv_freq, jnp.zeros((dh - ROPE_DIM,), F32)])
    freq = jnp.tile(freq_head, LANES // dh).reshape(1, LANES)
    tile2 = lambda g: jnp.tile(g, LANES // dh).reshape(1, LANES)
    sink_rows = jnp.concatenate([sinks.reshape(n_q, 1), jnp.full((n_q, 2 * blk - 1), -jnp.inf, F32)], axis=1)
    cur = lambda b, i: b * nb + i
    prev = lambda b, i: b * nb + jnp.maximum(i - 1, 0)
    k_blk, v_blk = q_w // kv_w, q_w // kv_w + 1
    small = lambda b, i: (0, 0)
    attn = pl.pallas_call(
        _swa_kernel,
        grid=(batch, nb),
        in_specs=[
            pl.BlockSpec((blk, q_w), lambda b, i: (cur(b, i), 0)),
            pl.BlockSpec((blk, kv_w), lambda b, i: (cur(b, i), k_blk)),
            pl.BlockSpec((blk, kv_w), lambda b, i: (prev(b, i), k_blk)),
            pl.BlockSpec((blk, kv_w), lambda b, i: (cur(b, i), v_blk)),
            pl.BlockSpec((blk, kv_w), lambda b, i: (prev(b, i), v_blk)),
            pl.BlockSpec((blk, 1), lambda b, i: (cur(b, i), 0)),
            pl.BlockSpec((blk, 1), lambda b, i: (prev(b, i), 0)),
            pl.BlockSpec((1, LANES), small),
            pl.BlockSpec((1, LANES), small),
            pl.BlockSpec((1, LANES), small),
            pl.BlockSpec((n_q, 2 * blk), small),
        ],
        out_specs=pl.BlockSpec((blk, q_w), lambda b, i: (cur(b, i), 0)),
        out_shape=jax.ShapeDtypeStruct((tokens, q_w), BF16),
        compiler_params=_compiler_params(("parallel", "arbitrary")),
        name="swa",
    )(qkv, qkv, qkv, qkv, qkv, pos, pos, freq, tile2(q_gain), tile2(k_gain), sink_rows)
    return _proj(attn, w_o, layer, seq=seq, residual=x)


def _rwkv_proj_kernel(x_ref, xb_ref, gain_ref, mix_ref, wl_ref, w_ref, o_ref, lo_ref, xm_ref, *, per_proj, seq):
    j = pl.program_id(1)
    tm, d = x_ref.shape
    pad = RWKV_LORA_PAD

    @pl.when(j == 0)
    def _():
        gain = gain_ref[...]
        rc = min(RWKV_PROLOGUE_ROWS, tm)
        row = lax.broadcasted_iota(jnp.int32, (rc, d), 0)
        starts_sequence = (pl.program_id(0) * tm) % seq == 0
        for c in range(tm // rc):
            rows = slice(c * rc, (c + 1) * rc)
            hn = _rmsnorm_rows(x_ref[rows, :], gain)
            if c == 0:
                before = jnp.where(starts_sequence, 0.0, xb_ref[SUBLANES - 1:SUBLANES, :])
            else:
                before = x_ref[c * rc - 1:c * rc, :]
            dx = jnp.where(row == 0, _rmsnorm_rows(before, gain), pltpu.roll(hn, 1, 0)) - hn

            def mixed(i):
                return (hn + dx * mix_ref[i:i + 1, :]).astype(BF16)

            xm_ref[0, rows, :] = mixed(0)
            xm_ref[1, rows, :] = mixed(2)
            xm_ref[2, rows, :] = mixed(3)
            lo_ref[rows, 0:pad] = _dot(mixed(1), wl_ref[:, 0:pad])
            lo_ref[rows, pad:2 * pad] = _dot(mixed(4), wl_ref[:, pad:2 * pad])
            lo_ref[rows, 2 * pad:] = _dot(mixed(5), wl_ref[:, 2 * pad:])

    o_ref[...] = _dot(xm_ref[j // per_proj], w_ref[...])


def _rwkv_scan_kernel(r_ref, k_ref, v_ref, lo_ref, w0_ref, wlb_ref, a0_ref, alb_ref, glb_ref,
                      kk_ref, ka_ref, rk_ref, lnw_ref, lnb_ref, o_ref, st_ref):
    C, N = RWKV_CHUNK, RWKV_HEAD
    pad = RWKV_LORA_PAD
    n_pairs = r_ref.shape[1] // LANES

    @pl.when(pl.program_id(1) == 0)
    def _():
        st_ref[...] = jnp.zeros_like(st_ref)

    lane = lax.broadcasted_iota(jnp.int32, (C, LANES), 1)
    first = lane < N
    row_c = lax.broadcasted_iota(jnp.int32, (C, C), 0)
    col_c = lax.broadcasted_iota(jnp.int32, (C, C), 1)
    tril_bf = (row_c >= col_c).astype(BF16)
    row2 = lax.broadcasted_iota(jnp.int32, (2 * C, 2 * C), 0)
    col2 = lax.broadcasted_iota(jnp.int32, (2 * C, 2 * C), 1)
    same_head = (row2 // C) == (col2 // C)
    eye2 = row2 == col2
    row_s = lax.broadcasted_iota(jnp.int32, (C, 2 * C), 0)
    col_s = lax.broadcasted_iota(jnp.int32, (C, 2 * C), 1) % C
    strict_lower = row_s > col_s
    lower = row_s >= col_s
    eye_side = row_s == col_s

    def head_sum(t):
        s0 = jnp.sum(jnp.where(first, t, 0.0), axis=-1, keepdims=True)
        s1 = jnp.sum(jnp.where(first, 0.0, t), axis=-1, keepdims=True)
        return jnp.where(first, s0, s1)

    def stack(t):
        return jnp.concatenate([jnp.where(first, t, 0.0), jnp.where(first, 0.0, t)], axis=0)

    lw1 = jnp.tanh(lo_ref[:, 0:pad]).astype(BF16)
    la1 = lo_ref[:, pad:2 * pad].astype(BF16)
    lg1 = _sigmoid(lo_ref[:, 2 * pad:]).astype(BF16)

    r_all = r_ref[...]
    k_all = k_ref[...]
    zw = w0_ref[...] + _dot(lw1, wlb_ref[...])
    log_decay = -math.exp(-0.5) * _sigmoid(zw)
    a_all = _sigmoid(a0_ref[...] + _dot(la1, alb_ref[...]))
    g_all = _dot(lg1, glb_ref[...])
    kk_all = k_all * kk_ref[...]
    k2_all = k_all * (1.0 + (a_all - 1.0) * ka_ref[...])
    cum_all = _cumsum_time(tril_bf, log_decay)
    total_all = cum_all[C - 1:C, :]
    p_incl = jnp.exp(cum_all)
    p_excl = jnp.exp(cum_all - log_decay)
    inv_p = jnp.exp(-cum_all)
    to_end = jnp.exp(total_all - cum_all)
    rt_all = r_all * p_incl
    rk2_all = r_all * k2_all * rk_ref[...]

    h2 = 2 * C

    def block_diag(side):
        return jnp.where(same_head, jnp.concatenate([side, side], axis=0), 0.0).astype(BF16)

    def run_pairs(pairs):
        sls = {p: slice(p * LANES, (p + 1) * LANES) for p in pairs}
        lhs, rhs, v_sts, ends = {}, {}, {}, {}
        for p in pairs:
            sl = sls[p]
            kk = kk_all[:, sl]
            kk = kk * lax.rsqrt(jnp.maximum(head_sum(kk * kk), 1e-24))
            kka = kk * a_all[:, sl]
            a_t = -kk * p_excl[:, sl]
            b_t = kka * inv_p[:, sl]
            k_t = k2_all[:, sl] * inv_p[:, sl]
            b_e = kka * to_end[:, sl]
            k_e = k2_all[:, sl] * to_end[:, sl]
            lhs[p] = jnp.concatenate([a_t, rt_all[:, sl]], axis=0).astype(BF16)
            rhs[p] = jnp.concatenate([stack(b_t), stack(k_t)], axis=0).astype(BF16)
            ends[p] = jnp.concatenate([b_e, k_e], axis=0).astype(BF16)
            v_sts[p] = stack(v_ref[:, sl]).astype(BF16)

        grams = {p: _dot_nt(lhs[p], rhs[p]) for p in pairs}
        a_ak = {p: jnp.where(strict_lower, grams[p][:C, h2:], 0.0).astype(BF16) for p in pairs}
        a_rbk = {p: jnp.concatenate([jnp.where(lower, grams[p][C:, :h2], 0.0),
                                     jnp.where(lower, grams[p][C:, h2:], 0.0)], axis=1).astype(BF16)
                 for p in pairs}

        pw = {p: jnp.where(strict_lower, grams[p][:C, :h2], 0.0) for p in pairs}
        pw_bd = {p: block_diag(pw[p]) for p in pairs}
        t_side = {p: jnp.where(eye_side, 1.0, 0.0) + pw[p] for p in pairs}
        for _ in range(C.bit_length() - 2):
            pw = {p: _dot(pw[p].astype(BF16), pw_bd[p]) for p in pairs}
            pw_bd = {p: block_diag(pw[p]) for p in pairs}
            t_side = {p: t_side[p] + _dot(t_side[p].astype(BF16), pw_bd[p]) for p in pairs}

        m0 = {p: st_ref[p] for p in pairs}
        m0_bf = {p: m0[p].astype(BF16) for p in pairs}
        lm = {p: _dot(lhs[p], m0_bf[p]) for p in pairs}
        x_rhs = {p: lm[p][:C] + _dot(a_ak[p], v_sts[p]) for p in pairs}
        u = {p: _dot(t_side[p].astype(BF16), stack(x_rhs[p]).astype(BF16)) for p in pairs}
        uv_st = {p: jnp.concatenate([stack(u[p]).astype(BF16), v_sts[p]], axis=0) for p in pairs}
        y_all = {p: lm[p][C:] + _dot(a_rbk[p], uv_st[p]) for p in pairs}

        for p in pairs:
            total_col = jnp.sum(jnp.where(eye2, total_all[:, sls[p]], 0.0), axis=-1, keepdims=True)
            uv = jnp.concatenate([u[p], v_ref[:, sls[p]]], axis=0).astype(BF16)
            st_ref[p] = jnp.exp(total_col) * m0[p] + jnp.where(same_head, _dot_tn(ends[p], uv), 0.0)

        for p in pairs:
            sl = sls[p]
            y = y_all[p]
            mu = head_sum(y) * (1.0 / N)
            yc = y - mu
            var = head_sum(yc * yc) * (1.0 / N)
            yn = yc * lax.rsqrt(var + RWKV_LN_EPS) * lnw_ref[:, sl] + lnb_ref[:, sl]
            bonus = head_sum(rk2_all[:, sl])
            o_ref[:, sl] = ((yn + bonus * v_ref[:, sl]) * g_all[:, sl]).astype(o_ref.dtype)

    group = min(RWKV_PAIR_GROUP, n_pairs)
    for first_pair in range(0, n_pairs, group):
        run_pairs(range(first_pair, first_pair + group))


def _rwkv_mixer(x, norm_gain, mix, w_rkv, w0, w_la, w_lb, a0, a_la, a_lb, g_la, g_lb, k_k, k_a, r_k,
                ln_w, ln_b, w_o, layer, *, batch, seq):
    tokens, d = x.shape
    C = RWKV_CHUNK
    pad = RWKV_LORA_PAD
    tm = _row_tile(seq, 1024)
    tn = 512
    per_proj = d // tn
    n_tiles = tokens // tm
    lora_w = w_la.shape[1], a_la.shape[1], g_la.shape[1]
    lora_cols = 2 * pad + lora_w[2]
    assert lora_w[0] <= pad and lora_w[1] <= pad and 2 * C == LANES
    w_lora = jnp.concatenate([_pad_cols(w_la, pad), _pad_cols(a_la, pad), g_la], axis=1).astype(BF16)
    proj, lora = pl.pallas_call(
        functools.partial(_rwkv_proj_kernel, per_proj=per_proj, seq=seq),
        grid=(n_tiles, 3 * per_proj),
        in_specs=[
            pl.BlockSpec((tm, d), lambda i, j: (i, 0)),
            pl.BlockSpec((SUBLANES, d), lambda i, j: (jnp.maximum(i * (tm // SUBLANES) - 1, 0), 0)),
            pl.BlockSpec((1, d), lambda i, j: (0, 0)),
            pl.BlockSpec((6, d), lambda i, j: (0, 0)),
            pl.BlockSpec((d, lora_cols), lambda i, j: (0, 0)),
            pl.BlockSpec((None, d, tn), lambda i, j: (j // per_proj, 0, j % per_proj)),
        ],
        out_specs=[
            pl.BlockSpec((tm, tn), lambda i, j: (i, j)),
            pl.BlockSpec((tm, lora_cols), lambda i, j: (i, 0)),
        ],
        out_shape=[
            jax.ShapeDtypeStruct((tokens, 3 * d), F32),
            jax.ShapeDtypeStruct((tokens, lora_cols), F32),
        ],
        scratch_shapes=[pltpu.VMEM((3, tm, d), BF16)],
        compiler_params=_compiler_params(("parallel", "arbitrary")),
        name="rwkv_proj",
    )(x, x, norm_gain.reshape(1, d), mix, w_lora, w_rkv.astype(BF16))

    nc = seq // C
    row1 = lambda t: t.reshape(1, d)
    pad_rows = lambda w: jnp.pad(w, ((0, pad - w.shape[0]), (0, 0))).astype(BF16)
    chunk = lambda blk: (lambda b, c: (b * nc + c, blk))
    const = lambda b, c: (0, 0)
    vec = pl.BlockSpec((1, d), const)
    mixed = pl.pallas_call(
        _rwkv_scan_kernel,
        grid=(batch, nc),
        in_specs=[
            pl.BlockSpec((C, d), chunk(0)),
            pl.BlockSpec((C, d), chunk(1)),
            pl.BlockSpec((C, d), chunk(2)),
            pl.BlockSpec((C, lora_cols), chunk(0)),
            vec,
            pl.BlockSpec((pad, d), const),
            vec,
            pl.BlockSpec((pad, d), const),
            pl.BlockSpec((lora_w[2], d), const),
            vec, vec, vec, vec, vec,
        ],
        out_specs=pl.BlockSpec((C, d), chunk(0)),
        out_shape=jax.ShapeDtypeStruct((tokens, d), BF16),
        scratch_shapes=[pltpu.VMEM((d // LANES, LANES, LANES), F32)],
        compiler_params=_compiler_params(("parallel", "arbitrary")),
        name="rwkv_scan",
    )(proj, proj, proj, lora, row1(w0), pad_rows(w_lb), row1(a0), pad_rows(a_lb), g_lb.astype(BF16),
      row1(k_k), row1(k_a), row1(r_k), row1(ln_w), row1(ln_b))
    return _proj(mixed, w_o, layer, seq=seq, residual=x)


def kernel(x, positions, ffn1_norm, ffn1_w_gu, ffn1_w_down, mixer_norm, ffn2_norm, ffn2_w_gu, ffn2_w_down,
           mlstm_w_in, mlstm_b_gate, mlstm_head_gain, mlstm_w_out,
           attn_w_qkv, attn_q_gain, attn_k_gain, attn_sinks, attn_w_o,
           rwkv_mix, rwkv_w_rkv, rwkv_w0, rwkv_w_lora_a, rwkv_w_lora_b, rwkv_a0, rwkv_a_lora_a, rwkv_a_lora_b,
           rwkv_g_lora_a, rwkv_g_lora_b, rwkv_k_k, rwkv_k_a, rwkv_r_k, rwkv_ln_w, rwkv_ln_b, rwkv_w_o):
    batch, seq, d = x.shape
    depth = ffn1_norm.shape[0]
    h = x.reshape(batch * seq, d)
    ffn1_gu, ffn1_down = ffn1_w_gu.astype(BF16), ffn1_w_down.astype(BF16)
    ffn2_gu, ffn2_down = ffn2_w_gu.astype(BF16), ffn2_w_down.astype(BF16)
    mlstm_in = jnp.pad(mlstm_w_in, ((0, 0), (0, 0), (0, MLSTM_GATE_PAD - 2 * MLSTM_HEADS))).astype(BF16)
    mlstm_out = mlstm_w_out.astype(BF16)
    attn_qkv, attn_o = attn_w_qkv.astype(BF16), attn_w_o.astype(BF16)
    rwkv_o = rwkv_w_o.astype(BF16)
    for i in range(depth):
        h = _ffn(h, ffn1_norm[i], ffn1_gu, ffn1_down, i, seq=seq)
        kind, j = i % 3, i // 3
        if kind == 0:
            h = _mlstm_mixer(h, mixer_norm[i], mlstm_in, mlstm_b_gate[j], mlstm_head_gain[j], mlstm_out, j,
                             batch=batch, seq=seq)
        elif kind == 1:
            h = _swa_mixer(h, positions, mixer_norm[i], attn_qkv, attn_q_gain[j], attn_k_gain[j],
                           attn_sinks[j], attn_o, j, batch=batch, seq=seq)
        else:
            h = _rwkv_mixer(h, mixer_norm[i], rwkv_mix[j], rwkv_w_rkv[j], rwkv_w0[j], rwkv_w_lora_a[j],
                            rwkv_w_lora_b[j], rwkv_a0[j], rwkv_a_lora_a[j], rwkv_a_lora_b[j], rwkv_g_lora_a[j],
                            rwkv_g_lora_b[j], rwkv_k_k[j], rwkv_k_a[j], rwkv_r_k[j], rwkv_ln_w[j], rwkv_ln_b[j],
                            rwkv_o, j, batch=batch, seq=seq)
        h = _ffn(h, ffn2_norm[i], ffn2_gu, ffn2_down, i, seq=seq)
    return h.reshape(batch, seq, d)
```

```python
import functools
import math

import jax
import jax.numpy as jnp
from jax import lax
from jax.experimental import pallas as pl
from jax.experimental.pallas import tpu as pltpu

F32 = jnp.float32
BF16 = jnp.bfloat16

NORM_EPS = 1e-6

MLSTM_HEADS = 8
MLSTM_DQK = 128
MLSTM_DV = 256
MLSTM_CHUNK = 64
MLSTM_GATE_PAD = 512
MLSTM_ROWS = 4

ATTN_HEAD_DIM = 64
ATTN_GROUP = 8
ATTN_WINDOW = 128
ROPE_DIM = 16
ROPE_THETA = 500000.0

RWKV_HEAD = 64
RWKV_CHUNK = 64
RWKV_LN_EPS = 64e-5
RWKV_LORA_PAD = 128
RWKV_PAIR_GROUP = 16
RWKV_PROLOGUE_ROWS = 256

LANES = 128
SUBLANES = 8
VMEM_LIMIT_BYTES = 56 * 1024 * 1024


def _compiler_params(semantics):
    return pltpu.CompilerParams(dimension_semantics=semantics, vmem_limit_bytes=VMEM_LIMIT_BYTES)


def _row_tile(seq, preferred):
    tile = preferred
    while seq % tile:
        tile //= 2
    return tile


def _dot(a, b):
    return jnp.dot(a, b, preferred_element_type=F32)


def _dot_nt(a, b):
    return lax.dot_general(a, b, (((1,), (1,)), ((), ())), preferred_element_type=F32)


def _dot_tn(a, b):
    return lax.dot_general(a, b, (((0,), (0,)), ((), ())), preferred_element_type=F32)


def _sigmoid(x):
    return 1.0 / (1.0 + jnp.exp(-x))


def _rmsnorm_rows(x, gain):
    ms = jnp.mean(x * x, axis=-1, keepdims=True)
    return x * lax.rsqrt(ms + NORM_EPS) * gain


def _split3(x):
    hi = x.astype(BF16)
    r1 = x - hi.astype(F32)
    mid = r1.astype(BF16)
    lo = (r1 - mid.astype(F32)).astype(BF16)
    return hi, mid, lo


def _cumsum_time(tril_bf, x):
    hi, mid, lo = _split3(x)
    return _dot(tril_bf, hi) + _dot(tril_bf, mid) + _dot(tril_bf, lo)


def _ffn_kernel(x_ref, gain_ref, wgu_hbm, wd_hbm, o_ref, xn_ref, wg_buf, wu_buf, wd_buf, sem, *, layer, nf, tf):
    i = pl.program_id(0)

    def chunk_copies(chunk, slot):
        return (
            pltpu.make_async_copy(wgu_hbm.at[layer, :, pl.ds(chunk * tf, tf)], wg_buf.at[slot], sem.at[0, slot]),
            pltpu.make_async_copy(wgu_hbm.at[layer, :, pl.ds((nf + chunk) * tf, tf)], wu_buf.at[slot],
                                  sem.at[1, slot]),
            pltpu.make_async_copy(wd_hbm.at[layer, pl.ds(chunk * tf, tf), :], wd_buf.at[slot], sem.at[2, slot]),
        )

    def start(chunk, slot):
        for copy in chunk_copies(chunk, slot):
            copy.start()

    def wait(chunk, slot):
        for copy in chunk_copies(chunk, slot):
            copy.wait()

    @pl.when(i == 0)
    def _():
        start(0, 0)

    x = x_ref[...]
    xn_ref[...] = _rmsnorm_rows(x, gain_ref[...]).astype(BF16)
    o_ref[...] = x

    for chunk in range(nf):
        slot = (i * nf + chunk) % 2
        if chunk + 1 < nf:
            start(chunk + 1, 1 - slot)
        else:
            @pl.when(i + 1 < pl.num_programs(0))
            def _():
                start(0, 1 - slot)
        wait(chunk, slot)
        xn = xn_ref[...]
        gate = _dot(xn, wg_buf[slot])
        up = _dot(xn, wu_buf[slot])
        act = (gate * _sigmoid(gate) * (0.5 * up)).astype(BF16)
        o_ref[...] += _dot(act, wd_buf[slot])


def _ffn(x, gain, w_gu, w_down, layer, *, seq):
    tokens, d = x.shape
    f = w_down.shape[1]
    tm = _row_tile(seq, 1024)
    tf = _col_tile(f, 768)
    nf = f // tf
    return pl.pallas_call(
        functools.partial(_ffn_kernel, layer=layer, nf=nf, tf=tf),
        grid=(tokens // tm,),
        in_specs=[
            pl.BlockSpec((tm, d), lambda i: (i, 0)),
            pl.BlockSpec((1, d), lambda i: (0, 0)),
            pl.BlockSpec(memory_space=pl.ANY),
            pl.BlockSpec(memory_space=pl.ANY),
        ],
        out_specs=pl.BlockSpec((tm, d), lambda i: (i, 0)),
        out_shape=jax.ShapeDtypeStruct((tokens, d), F32),
        scratch_shapes=[
            pltpu.VMEM((tm, d), BF16),
            pltpu.VMEM((2, d, tf), BF16),
            pltpu.VMEM((2, d, tf), BF16),
            pltpu.VMEM((2, tf, d), BF16),
            pltpu.SemaphoreType.DMA((3, 2)),
        ],
        compiler_params=_compiler_params(("arbitrary",)),
        name="ffn",
    )(x, gain.reshape(1, d), w_gu, w_down)


def _proj_kernel(*refs, norm, residual):
    refs = list(refs)
    x_ref = refs.pop(0)
    gain_ref = refs.pop(0) if norm else None
    w_ref = refs.pop(0)
    res_ref = refs.pop(0) if residual else None
    o_ref = refs.pop(0)

    if norm:
        xn_ref = refs.pop(0)

        @pl.when(pl.program_id(1) == 0)
        def _():
            xn_ref[...] = _rmsnorm_rows(x_ref[...], gain_ref[...]).astype(BF16)

        lhs = xn_ref[...]
    else:
        lhs = x_ref[...]

    acc = _dot(lhs, w_ref[...])
    if residual:
        acc = acc + res_ref[...]
    o_ref[...] = acc


def _col_tile(n, cap):
    best = LANES
    for t in range(LANES, cap + 1, LANES):
        if n % t == 0:
            best = t
    return best


def _proj(x, w, layer, *, seq, gain=None, residual=None):
    tokens, k = x.shape
    n = w.shape[2]
    norm = gain is not None
    assert norm or x.dtype == BF16
    tm = _row_tile(seq, 1024 if norm else 512)
    tn = _col_tile(n, 1024 if norm else 2048)
    in_specs = [pl.BlockSpec((tm, k), lambda i, j: (i, 0))]
    args = [x]
    if norm:
        in_specs.append(pl.BlockSpec((1, k), lambda i, j: (0, 0)))
        args.append(gain.reshape(1, k))
    in_specs.append(pl.BlockSpec((None, k, tn), lambda i, j: (layer, 0, j)))
    args.append(w)
    if residual is not None:
        in_specs.append(pl.BlockSpec((tm, tn), lambda i, j: (i, j)))
        args.append(residual)
    return pl.pallas_call(
        functools.partial(_proj_kernel, norm=norm, residual=residual is not None),
        grid=(tokens // tm, n // tn),
        in_specs=in_specs,
        out_specs=pl.BlockSpec((tm, tn), lambda i, j: (i, j)),
        out_shape=jax.ShapeDtypeStruct((tokens, n), F32),
        scratch_shapes=[pltpu.VMEM((tm, k), BF16)] if norm else [],
        compiler_params=_compiler_params(("parallel", "arbitrary")),
        name="proj",
    )(*args)


def _pad_cols(w, width):
    return jnp.pad(w, ((0, 0), (0, width - w.shape[1])))


def _mlstm_kernel(q_ref, k_ref, v_ref, og_ref, g_ref, bias_ref, hg_ref, o_ref, ct_ref, n_ref, m_ref):
    L, H, DK, DV = MLSTM_CHUNK, MLSTM_HEADS, MLSTM_DQK, MLSTM_DV
    R = q_ref.shape[0]

    @pl.when(pl.program_id(1) == 0)
    def _():
        ct_ref[...] = jnp.zeros_like(ct_ref)
        n_ref[...] = jnp.zeros_like(n_ref)
        m_ref[...] = jnp.zeros_like(m_ref)

    row = lax.broadcasted_iota(jnp.int32, (L, L), 0)
    col = lax.broadcasted_iota(jnp.int32, (L, L), 1)
    causal = row >= col
    tril_bf = causal.astype(BF16)

    z = [g_ref[r] + bias_ref[...] for r in range(R)]
    log_f = [jnp.minimum(t, 0.0) - jnp.log(1.0 + jnp.exp(-jnp.abs(t))) for t in z]
    bcum = [_cumsum_time(tril_bf, t) for t in log_f]
    z_t = [t.T for t in z]
    bcum_t = [t.T for t in bcum]

    units = range(R * H)
    rh = [divmod(u, H) for u in units]
    q = [q_ref[r, :, h * DK:(h + 1) * DK] for r, h in rh]
    k = [k_ref[r, :, h * DK:(h + 1) * DK] * (DK ** -0.5) for r, h in rh]
    q_bf = [t.astype(BF16) for t in q]
    k_bf = [t.astype(BF16) for t in k]
    v_bf = [v_ref[r, :, h * DV:(h + 1) * DV].astype(BF16) for r, h in rh]
    m_prev = [m_ref[u] for u in units]

    b_col = [bcum[r][:, H + h:H + h + 1] for r, h in rh]
    g = [bcum[r][L - 1:L, H + h:H + h + 1] for r, h in rh]
    qk = [_dot_nt(q_bf[u], k_bf[u]) for u in units]
    inter = [_dot(q_bf[u], ct_ref[u].astype(BF16)) for u in units]
    qn = [jnp.sum(q[u] * n_ref[u], axis=-1, keepdims=True) for u in units]

    log_d = [jnp.where(causal, b_col[u] - bcum_t[r][H + h:H + h + 1, :] + z_t[r][h:h + 1, :], -jnp.inf)
             for u, (r, h) in enumerate(rh)]
    row_max = [jnp.max(log_d[u], axis=-1, keepdims=True) for u in units]
    log_w = [g[u] - b_col[u] + z[r][:, h:h + 1] for u, (r, h) in enumerate(rh)]
    m_new = [jnp.maximum(g[u] + m_prev[u], jnp.max(log_w[u], axis=0, keepdims=True)) for u in units]
    kw = [k[u] * jnp.exp(log_w[u] - m_new[u]) for u in units]
    outer = [_dot_tn(kw[u].astype(BF16), v_bf[u]) for u in units]

    log_inter = [b_col[u] + m_prev[u] for u in units]
    m_out = [jnp.maximum(log_inter[u], row_max[u]) for u in units]
    s = [qk[u] * jnp.exp(log_d[u] - m_out[u]) for u in units]
    intra = [_dot(s[u].astype(BF16), v_bf[u]) for u in units]
    s_sum = [jnp.sum(s[u], axis=-1, keepdims=True) for u in units]
    w_inter = [jnp.exp(log_inter[u] - m_out[u]) for u in units]
    den = [s_sum[u] + w_inter[u] * qn[u] for u in units]
    scale = [1.0 / jnp.maximum(jnp.abs(den[u]), jnp.exp(-m_out[u])) for u in units]
    hout = [(intra[u] + w_inter[u] * inter[u]) * scale[u] for u in units]
    ms = [jnp.mean(hout[u] * hout[u], axis=-1, keepdims=True) for u in units]
    gate = [_sigmoid(og_ref[r, :, h * DV:(h + 1) * DV]) * hg_ref[:, h * DV:(h + 1) * DV] for r, h in rh]
    for u, (r, h) in enumerate(rh):
        o_ref[r, :, h * DV:(h + 1) * DV] = (hout[u] * lax.rsqrt(ms[u] + NORM_EPS) * gate[u]).astype(o_ref.dtype)

    for u in units:
        decay = jnp.exp(g[u] + m_prev[u] - m_new[u])
        ct_ref[u] = decay * ct_ref[u] + outer[u]
        n_ref[u] = decay * n_ref[u] + jnp.sum(kw[u], axis=0, keepdims=True)
        m_ref[u] = m_new[u]


def _mlstm_mixer(x, norm_gain, w_in_padded, b_gate, head_gain, w_out, layer, *, batch, seq):
    tokens, d = x.shape
    L, H, DK, DV = MLSTM_CHUNK, MLSTM_HEADS, MLSTM_DQK, MLSTM_DV
    R = MLSTM_ROWS if batch % MLSTM_ROWS == 0 else 1
    nc = seq // L
    main = 2 * H * DK + 2 * H * DV
    assert w_in_padded.shape[2] == main + MLSTM_GATE_PAD
    zin = _proj(x, w_in_padded, layer, seq=seq, gain=norm_gain).reshape(batch, seq, main + MLSTM_GATE_PAD)
    bias = jnp.pad(b_gate.reshape(1, 2 * H), ((0, 0), (0, LANES - 2 * H)))
    qk_w, v_w = H * DK, H * DV
    gated = pl.pallas_call(
        _mlstm_kernel,
        grid=(batch // R, nc),
        in_specs=[
            pl.BlockSpec((R, L, qk_w), lambda b, c: (b, c, 0)),
            pl.BlockSpec((R, L, qk_w), lambda b, c: (b, c, 1)),
            pl.BlockSpec((R, L, v_w), lambda b, c: (b, c, 2 * qk_w // v_w)),
            pl.BlockSpec((R, L, v_w), lambda b, c: (b, c, 2 * qk_w // v_w + 1)),
            pl.BlockSpec((R, L, LANES), lambda b, c: (b, c, main // LANES)),
            pl.BlockSpec((1, LANES), lambda b, c: (0, 0)),
            pl.BlockSpec((1, v_w), lambda b, c: (0, 0)),
        ],
        out_specs=pl.BlockSpec((R, L, v_w), lambda b, c: (b, c, 0)),
        out_shape=jax.ShapeDtypeStruct((batch, seq, v_w), BF16),
        scratch_shapes=[
            pltpu.VMEM((R * H, DK, DV), F32),
            pltpu.VMEM((R * H, 1, DK), F32),
            pltpu.VMEM((R * H, 1, 1), F32),
        ],
        compiler_params=_compiler_params(("parallel", "arbitrary")),
        name="mlstm_scan",
    )(zin, zin, zin, zin, zin, bias, head_gain.reshape(1, v_w))
    return _proj(gated.reshape(tokens, v_w), w_out, layer, seq=seq, residual=x)


def _qk_norm_rope(slabs, gain2, tables, lane):
    dh = ATTN_HEAD_DIM
    half = ROPE_DIM // 2
    low = lane % dh < half
    idx = range(len(slabs))
    r_i = lax.broadcasted_iota(jnp.int32, (LANES, LANES), 0)
    c_i = lax.broadcasted_iota(jnp.int32, (LANES, LANES), 1)
    same_head = ((r_i // dh) == (c_i // dh)).astype(BF16)
    sq = [t * t for t in slabs]
    sq_hi = [t.astype(BF16) for t in sq]
    sq_lo = [(sq[i] - sq_hi[i].astype(F32)).astype(BF16) for i in idx]
    ssq = [_dot(sq_hi[i], same_head) + _dot(sq_lo[i], same_head) for i in idx]
    tn = [slabs[i] * lax.rsqrt(ssq[i] * (1.0 / dh) + NORM_EPS) * gain2 for i in idx]
    up = [pltpu.roll(tn[i], LANES - half, 1) for i in idx]
    down = [pltpu.roll(tn[i], half, 1) for i in idx]
    return [tn[i] * tables[i][0] + jnp.where(low, up[i], down[i]) * tables[i][1] for i in idx]


def _swa_kernel(q_ref, kc_ref, kp_ref, vc_ref, vp_ref, pc_ref, pp_ref, freq_ref, qg_ref, kg_ref, sink_ref, o_ref):
    blk, dh, grp = ATTN_WINDOW, ATTN_HEAD_DIM, ATTN_GROUP
    half = ROPE_DIM // 2
    n_kv = kc_ref.shape[1] // dh
    has_prev = pl.program_id(1) > 0

    lane = lax.broadcasted_iota(jnp.int32, (blk, LANES), 1)
    in_rot = lane % dh < ROPE_DIM

    def rope_tables(pos_ref):
        ang = pos_ref[...].astype(F32) * freq_ref[...]
        cos_t = jnp.where(in_rot, jnp.cos(ang), 1.0)
        sin = jnp.sin(ang)
        sin_t = jnp.where(in_rot, jnp.where(lane % dh < half, -sin, sin), 0.0)
        return cos_t, sin_t

    cos_c, sin_c = rope_tables(pc_ref)
    cos_p, sin_p = rope_tables(pp_ref)

    qi = lax.broadcasted_iota(jnp.int32, (blk, 2 * blk), 0) + blk
    kj = lax.broadcasted_iota(jnp.int32, (blk, 2 * blk), 1)
    first_key = jnp.where(has_prev, 0, blk)
    ok = (qi >= kj) & (qi - kj < ATTN_WINDOW) & (kj >= first_key)
    scale = dh ** -0.5
    assert LANES == 2 * dh and scale == 0.125
    first = lane < dh
    first2 = lax.broadcasted_iota(jnp.int32, (2 * blk, LANES), 1) < dh
    key_row = lax.broadcasted_iota(jnp.int32, (2 * blk, LANES), 0)
    half_ones = ((lax.broadcasted_iota(jnp.int32, (4 * blk, LANES), 0) < 2 * blk)
                 == (lax.broadcasted_iota(jnp.int32, (4 * blk, LANES), 1) < dh)).astype(BF16)
    slabs_per_kv = grp * dh // LANES

    k_dup, v_split = [], []
    kv_slabs = n_kv * dh // LANES
    k_in = [kp_ref[:, c * LANES:(c + 1) * LANES] for c in range(kv_slabs)]
    k_in += [kc_ref[:, c * LANES:(c + 1) * LANES] for c in range(kv_slabs)]
    k_roped = _qk_norm_rope(k_in, kg_ref[...], [(cos_p, sin_p)] * kv_slabs + [(cos_c, sin_c)] * kv_slabs, lane)
    q_slabs = q_ref.shape[1] // LANES
    q_roped = _qk_norm_rope([q_ref[:, c * LANES:(c + 1) * LANES] for c in range(q_slabs)], qg_ref[...],
                            [(cos_c, sin_c)] * q_slabs, lane)
    for c in range(kv_slabs):
        sl = slice(c * LANES, (c + 1) * LANES)
        kcat = jnp.concatenate([k_roped[c], k_roped[kv_slabs + c]], axis=0)
        vcat = jnp.concatenate([vp_ref[:, sl], vc_ref[:, sl]], axis=0)
        vcat = jnp.where(key_row == 0, 0.0, vcat)
        k_rot = pltpu.roll(kcat, dh, 1)
        v_rot = pltpu.roll(vcat, dh, 1)
        k_dup.append(jnp.where(first2, kcat, k_rot).astype(BF16))
        k_dup.append(jnp.where(first2, k_rot, kcat).astype(BF16))
        v_split.append(jnp.concatenate([jnp.where(first2, vcat, 0.0), jnp.where(first2, 0.0, v_rot)],
                                       axis=0).astype(BF16))
        v_split.append(jnp.concatenate([jnp.where(first2, v_rot, 0.0), jnp.where(first2, 0.0, vcat)],
                                       axis=0).astype(BF16))

    s_all = []
    for kv in range(n_kv):
        parts = []
        for c in range(kv * slabs_per_kv, (kv + 1) * slabs_per_kv):
            q2 = q_roped[c] * scale
            parts += [jnp.where(first, q2, 0.0), jnp.where(first, 0.0, q2)]
        q_st = jnp.concatenate(parts, axis=0).astype(BF16)
        s_all.append(_dot_nt(q_st, k_dup[kv]))

    heads = range(n_kv * grp)
    s = [jnp.where(ok, s_all[h // grp][(h % grp) * blk:(h % grp + 1) * blk], sink_ref[h:h + 1, :]) for h in heads]
    m = [jnp.max(s[h], axis=-1, keepdims=True) for h in heads]
    probs = [jnp.exp(s[h] - m[h]).astype(BF16) for h in heads]
    slabs = range(q_slabs)
    p_cat = [jnp.concatenate([probs[2 * c], probs[2 * c + 1]], axis=1) for c in slabs]
    outs = [_dot(p_cat[c], v_split[c // slabs_per_kv]) for c in slabs]
    dens = [_dot(p_cat[c], half_ones) for c in slabs]
    for c in slabs:
        o_ref[:, c * LANES:(c + 1) * LANES] = (outs[c] / dens[c]).astype(o_ref.dtype)


def _swa_mixer(x, positions, norm_gain, w_qkv, q_gain, k_gain, sinks, w_o, layer, *, batch, seq):
    tokens, d = x.shape
    blk, dh = ATTN_WINDOW, ATTN_HEAD_DIM
    nb = seq // blk
    n_q = sinks.shape[0]
    q_w = n_q * dh
    kv_w = (w_qkv.shape[2] - q_w) // 2
    qkv = _proj(x, w_qkv, layer, seq=seq, gain=norm_gain)
    pos = positions.reshape(tokens, 1)
    inv_freq = ROPE_THETA ** (-jnp.arange(0, ROPE_DIM, 2, dtype=F32) / ROPE_DIM)
    freq_head = jnp.concatenate([inv_freq, inv_freq, jnp.zeros((dh - ROPE_DIM,), F32)])
    freq = jnp.tile(freq_head, LANES // dh).reshape(1, LANES)
    tile2 = lambda g: jnp.tile(g, LANES // dh).reshape(1, LANES)
    sink_rows = jnp.concatenate([sinks.reshape(n_q, 1), jnp.full((n_q, 2 * blk - 1), -jnp.inf, F32)], axis=1)
    cur = lambda b, i: b * nb + i
    prev = lambda b, i: b * nb + jnp.maximum(i - 1, 0)
    k_blk, v_blk = q_w // kv_w, q_w // kv_w + 1
    small = lambda b, i: (0, 0)
    attn = pl.pallas_call(
        _swa_kernel,
        grid=(batch, nb),
        in_specs=[
            pl.BlockSpec((blk, q_w), lambda b, i: (cur(b, i), 0)),
            pl.BlockSpec((blk, kv_w), lambda b, i: (cur(b, i), k_blk)),
            pl.BlockSpec((blk, kv_w), lambda b, i: (prev(b, i), k_blk)),
            pl.BlockSpec((blk, kv_w), lambda b, i: (cur(b, i), v_blk)),
            pl.BlockSpec((blk, kv_w), lambda b, i: (prev(b, i), v_blk)),
            pl.BlockSpec((blk, 1), lambda b, i: (cur(b, i), 0)),
            pl.BlockSpec((blk, 1), lambda b, i: (prev(b, i), 0)),
            pl.BlockSpec((1, LANES), small),
            pl.BlockSpec((1, LANES), small),
            pl.BlockSpec((1, LANES), small),
            pl.BlockSpec((n_q, 2 * blk), small),
        ],
        out_specs=pl.BlockSpec((blk, q_w), lambda b, i: (cur(b, i), 0)),
        out_shape=jax.ShapeDtypeStruct((tokens, q_w), BF16),
        compiler_params=_compiler_params(("parallel", "arbitrary")),
        name="swa",
    )(qkv, qkv, qkv, qkv, qkv, pos, pos, freq, tile2(q_gain), tile2(k_gain), sink_rows)
    return _proj(attn, w_o, layer, seq=seq, residual=x)


def _rwkv_proj_kernel(x_ref, xb_ref, gain_ref, mix_ref, wl_ref, w_ref, o_ref, lo_ref, xm_ref, *, per_proj, seq):
    j = pl.program_id(1)
    tm, d = x_ref.shape
    pad = RWKV_LORA_PAD

    @pl.when(j == 0)
    def _():
        gain = gain_ref[...]
        rc = min(RWKV_PROLOGUE_ROWS, tm)
        row = lax.broadcasted_iota(jnp.int32, (rc, d), 0)
        starts_sequence = (pl.program_id(0) * tm) % seq == 0
        for c in range(tm // rc):
            rows = slice(c * rc, (c + 1) * rc)
            hn = _rmsnorm_rows(x_ref[rows, :], gain)
            if c == 0:
                before = jnp.where(starts_sequence, 0.0, xb_ref[SUBLANES - 1:SUBLANES, :])
            else:
                before = x_ref[c * rc - 1:c * rc, :]
            dx = jnp.where(row == 0, _rmsnorm_rows(before, gain), pltpu.roll(hn, 1, 0)) - hn

            def mixed(i):
                return (hn + dx * mix_ref[i:i + 1, :]).astype(BF16)

            xm_ref[0, rows, :] = mixed(0)
            xm_ref[1, rows, :] = mixed(2)
            xm_ref[2, rows, :] = mixed(3)
            lo_ref[rows, 0:pad] = _dot(mixed(1), wl_ref[:, 0:pad])
            lo_ref[rows, pad:2 * pad] = _dot(mixed(4), wl_ref[:, pad:2 * pad])
            lo_ref[rows, 2 * pad:] = _dot(mixed(5), wl_ref[:, 2 * pad:])

    o_ref[...] = _dot(xm_ref[j // per_proj], w_ref[...])


def _rwkv_scan_kernel(r_ref, k_ref, v_ref, lo_ref, w0_ref, wlb_ref, a0_ref, alb_ref, glb_ref,
                      kk_ref, ka_ref, rk_ref, lnw_ref, lnb_ref, o_ref, st_ref):
    C, N = RWKV_CHUNK, RWKV_HEAD
    pad = RWKV_LORA_PAD
    n_pairs = r_ref.shape[1] // LANES

    @pl.when(pl.program_id(1) == 0)
    def _():
        st_ref[...] = jnp.zeros_like(st_ref)

    lane = lax.broadcasted_iota(jnp.int32, (C, LANES), 1)
    first = lane < N
    row_c = lax.broadcasted_iota(jnp.int32, (C, C), 0)
    col_c = lax.broadcasted_iota(jnp.int32, (C, C), 1)
    tril_bf = (row_c >= col_c).astype(BF16)
    row2 = lax.broadcasted_iota(jnp.int32, (2 * C, 2 * C), 0)
    col2 = lax.broadcasted_iota(jnp.int32, (2 * C, 2 * C), 1)
    same_head = (row2 // C) == (col2 // C)
    eye2 = row2 == col2
    row_s = lax.broadcasted_iota(jnp.int32, (C, 2 * C), 0)
    col_s = lax.broadcasted_iota(jnp.int32, (C, 2 * C), 1) % C
    strict_lower = row_s > col_s
    lower = row_s >= col_s
    eye_side = row_s == col_s

    def head_sum(t):
        s0 = jnp.sum(jnp.where(first, t, 0.0), axis=-1, keepdims=True)
        s1 = jnp.sum(jnp.where(first, 0.0, t), axis=-1, keepdims=True)
        return jnp.where(first, s0, s1)

    def stack(t):
        return jnp.concatenate([jnp.where(first, t, 0.0), jnp.where(first, 0.0, t)], axis=0)

    lw1 = jnp.tanh(lo_ref[:, 0:pad]).astype(BF16)
    la1 = lo_ref[:, pad:2 * pad].astype(BF16)
    lg1 = _sigmoid(lo_ref[:, 2 * pad:]).astype(BF16)

    r_all = r_ref[...]
    k_all = k_ref[...]
    zw = w0_ref[...] + _dot(lw1, wlb_ref[...])
    log_decay = -math.exp(-0.5) * _sigmoid(zw)
    a_all = _sigmoid(a0_ref[...] + _dot(la1, alb_ref[...]))
    g_all = _dot(lg1, glb_ref[...])
    kk_all = k_all * kk_ref[...]
    k2_all = k_all * (1.0 + (a_all - 1.0) * ka_ref[...])
    cum_all = _cumsum_time(tril_bf, log_decay)
    total_all = cum_all[C - 1:C, :]
    p_incl = jnp.exp(cum_all)
    p_excl = jnp.exp(cum_all - log_decay)
    inv_p = jnp.exp(-cum_all)
    to_end = jnp.exp(total_all - cum_all)
    rt_all = r_all * p_incl
    rk2_all = r_all * k2_all * rk_ref[...]

    h2 = 2 * C

    def block_diag(side):
        return jnp.where(same_head, jnp.concatenate([side, side], axis=0), 0.0).astype(BF16)

    def run_pairs(pairs):
        sls = {p: slice(p * LANES, (p + 1) * LANES) for p in pairs}
        lhs, rhs, v_sts, ends = {}, {}, {}, {}
        for p in pairs:
            sl = sls[p]
            kk = kk_all[:, sl]
            kk = kk * lax.rsqrt(jnp.maximum(head_sum(kk * kk), 1e-24))
            kka = kk * a_all[:, sl]
            a_t = -kk * p_excl[:, sl]
            b_t = kka * inv_p[:, sl]
            k_t = k2_all[:, sl] * inv_p[:, sl]
            b_e = kka * to_end[:, sl]
            k_e = k2_all[:, sl] * to_end[:, sl]
            lhs[p] = jnp.concatenate([a_t, rt_all[:, sl]], axis=0).astype(BF16)
            rhs[p] = jnp.concatenate([stack(b_t), stack(k_t)], axis=0).astype(BF16)
            ends[p] = jnp.concatenate([b_e, k_e], axis=0).astype(BF16)
            v_sts[p] = stack(v_ref[:, sl]).astype(BF16)

        grams = {p: _dot_nt(lhs[p], rhs[p]) for p in pairs}
        a_ak = {p: jnp.where(strict_lower, grams[p][:C, h2:], 0.0).astype(BF16) for p in pairs}
        a_rbk = {p: jnp.concatenate([jnp.where(lower, grams[p][C:, :h2], 0.0),
                                     jnp.where(lower, grams[p][C:, h2:], 0.0)], axis=1).astype(BF16)
                 for p in pairs}

        pw = {p: jnp.where(strict_lower, grams[p][:C, :h2], 0.0) for p in pairs}
        pw_bd = {p: block_diag(pw[p]) for p in pairs}
        t_side = {p: jnp.where(eye_side, 1.0, 0.0) + pw[p] for p in pairs}
        for _ in range(C.bit_length() - 2):
            pw = {p: _dot(pw[p].astype(BF16), pw_bd[p]) for p in pairs}
            pw_bd = {p: block_diag(pw[p]) for p in pairs}
            t_side = {p: t_side[p] + _dot(t_side[p].astype(BF16), pw_bd[p]) for p in pairs}

        m0 = {p: st_ref[p] for p in pairs}
        m0_bf = {p: m0[p].astype(BF16) for p in pairs}
        lm = {p: _dot(lhs[p], m0_bf[p]) for p in pairs}
        x_rhs = {p: lm[p][:C] + _dot(a_ak[p], v_sts[p]) for p in pairs}
        u = {p: _dot(t_side[p].astype(BF16), stack(x_rhs[p]).astype(BF16)) for p in pairs}
        uv_st = {p: jnp.concatenate([stack(u[p]).astype(BF16), v_sts[p]], axis=0) for p in pairs}
        y_all = {p: lm[p][C:] + _dot(a_rbk[p], uv_st[p]) for p in pairs}

        for p in pairs:
            total_col = jnp.sum(jnp.where(eye2, total_all[:, sls[p]], 0.0), axis=-1, keepdims=True)
            uv = jnp.concatenate([u[p], v_ref[:, sls[p]]], axis=0).astype(BF16)
            st_ref[p] = jnp.exp(total_col) * m0[p] + jnp.where(same_head, _dot_tn(ends[p], uv), 0.0)

        for p in pairs:
            sl = sls[p]
            y = y_all[p]
            mu = head_sum(y) * (1.0 / N)
            yc = y - mu
            var = head_sum(yc * yc) * (1.0 / N)
            yn = yc * lax.rsqrt(var + RWKV_LN_EPS) * lnw_ref[:, sl] + lnb_ref[:, sl]
            bonus = head_sum(rk2_all[:, sl])
            o_ref[:, sl] = ((yn + bonus * v_ref[:, sl]) * g_all[:, sl]).astype(o_ref.dtype)

    group = min(RWKV_PAIR_GROUP, n_pairs)
    for first_pair in range(0, n_pairs, group):
        run_pairs(range(first_pair, first_pair + group))


def _rwkv_mixer(x, norm_gain, mix, w_rkv, w0, w_la, w_lb, a0, a_la, a_lb, g_la, g_lb, k_k, k_a, r_k,
                ln_w, ln_b, w_o, layer, *, batch, seq):
    tokens, d = x.shape
    C = RWKV_CHUNK
    pad = RWKV_LORA_PAD
    tm = _row_tile(seq, 1024)
    tn = 512
    per_proj = d // tn
    n_tiles = tokens // tm
    lora_w = w_la.shape[1], a_la.shape[1], g_la.shape[1]
    lora_cols = 2 * pad + lora_w[2]
    assert lora_w[0] <= pad and lora_w[1] <= pad and 2 * C == LANES
    w_lora = jnp.concatenate([_pad_cols(w_la, pad), _pad_cols(a_la, pad), g_la], axis=1).astype(BF16)
    proj, lora = pl.pallas_call(
        functools.partial(_rwkv_proj_kernel, per_proj=per_proj, seq=seq),
        grid=(n_tiles, 3 * per_proj),
        in_specs=[
            pl.BlockSpec((tm, d), lambda i, j: (i, 0)),
            pl.BlockSpec((SUBLANES, d), lambda i, j: (jnp.maximum(i * (tm // SUBLANES) - 1, 0), 0)),
            pl.BlockSpec((1, d), lambda i, j: (0, 0)),
            pl.BlockSpec((6, d), lambda i, j: (0, 0)),
            pl.BlockSpec((d, lora_cols), lambda i, j: (0, 0)),
            pl.BlockSpec((None, d, tn), lambda i, j: (j // per_proj, 0, j % per_proj)),
        ],
        out_specs=[
            pl.BlockSpec((tm, tn), lambda i, j: (i, j)),
            pl.BlockSpec((tm, lora_cols), lambda i, j: (i, 0)),
        ],
        out_shape=[
            jax.ShapeDtypeStruct((tokens, 3 * d), F32),
            jax.ShapeDtypeStruct((tokens, lora_cols), F32),
        ],
        scratch_shapes=[pltpu.VMEM((3, tm, d), BF16)],
        compiler_params=_compiler_params(("parallel", "arbitrary")),
        name="rwkv_proj",
    )(x, x, norm_gain.reshape(1, d), mix, w_lora, w_rkv.astype(BF16))

    nc = seq // C
    row1 = lambda t: t.reshape(1, d)
    pad_rows = lambda w: jnp.pad(w, ((0, pad - w.shape[0]), (0, 0))).astype(BF16)
    chunk = lambda blk: (lambda b, c: (b * nc + c, blk))
    const = lambda b, c: (0, 0)
    vec = pl.BlockSpec((1, d), const)
    mixed = pl.pallas_call(
        _rwkv_scan_kernel,
        grid=(batch, nc),
        in_specs=[
            pl.BlockSpec((C, d), chunk(0)),
            pl.BlockSpec((C, d), chunk(1)),
            pl.BlockSpec((C, d), chunk(2)),
            pl.BlockSpec((C, lora_cols), chunk(0)),
            vec,
            pl.BlockSpec((pad, d), const),
            vec,
            pl.BlockSpec((pad, d), const),
            pl.BlockSpec((lora_w[2], d), const),
            vec, vec, vec, vec, vec,
        ],
        out_specs=pl.BlockSpec((C, d), chunk(0)),
        out_shape=jax.ShapeDtypeStruct((tokens, d), BF16),
        scratch_shapes=[pltpu.VMEM((d // LANES, LANES, LANES), F32)],
        compiler_params=_compiler_params(("parallel", "arbitrary")),
        name="rwkv_scan",
    )(proj, proj, proj, lora, row1(w0), pad_rows(w_lb), row1(a0), pad_rows(a_lb), g_lb.astype(BF16),
      row1(k_k), row1(k_a), row1(r_k), row1(ln_w), row1(ln_b))
    return _proj(mixed, w_o, layer, seq=seq, residual=x)


def kernel(x, positions, ffn1_norm, ffn1_w_gu, ffn1_w_down, mixer_norm, ffn2_norm, ffn2_w_gu, ffn2_w_down,
           mlstm_w_in, mlstm_b_gate, mlstm_head_gain, mlstm_w_out,
           attn_w_qkv, attn_q_gain, attn_k_gain, attn_sinks, attn_w_o,
           rwkv_mix, rwkv_w_rkv, rwkv_w0, rwkv_w_lora_a, rwkv_w_lora_b, rwkv_a0, rwkv_a_lora_a, rwkv_a_lora_b,
           rwkv_g_lora_a, rwkv_g_lora_b, rwkv_k_k, rwkv_k_a, rwkv_r_k, rwkv_ln_w, rwkv_ln_b, rwkv_w_o):
    batch, seq, d = x.shape
    depth = ffn1_norm.shape[0]
    h = x.reshape(batch * seq, d)
    ffn1_gu, ffn1_down = ffn1_w_gu.astype(BF16), ffn1_w_down.astype(BF16)
    ffn2_gu, ffn2_down = ffn2_w_gu.astype(BF16), ffn2_w_down.astype(BF16)
    mlstm_in = jnp.pad(mlstm_w_in, ((0, 0), (0, 0), (0, MLSTM_GATE_PAD - 2 * MLSTM_HEADS))).astype(BF16)
    mlstm_out = mlstm_w_out.astype(BF16)
    attn_qkv, attn_o = attn_w_qkv.astype(BF16), attn_w_o.astype(BF16)
    rwkv_o = rwkv_w_o.astype(BF16)
    for i in range(depth):
        h = _ffn(h, ffn1_norm[i], ffn1_gu, ffn1_down, i, seq=seq)
        kind, j = i % 3, i // 3
        if kind == 0:
            h = _mlstm_mixer(h, mixer_norm[i], mlstm_in, mlstm_b_gate[j], mlstm_head_gain[j], mlstm_out, j,
                             batch=batch, seq=seq)
        elif kind == 1:
            h = _swa_mixer(h, positions, mixer_norm[i], attn_qkv, attn_q_gain[j], attn_k_gain[j],
                           attn_sinks[j], attn_o, j, batch=batch, seq=seq)
        else:
            h = _rwkv_mixer(h, mixer_norm[i], rwkv_mix[j], rwkv_w_rkv[j], rwkv_w0[j], rwkv_w_lora_a[j],
                            rwkv_w_lora_b[j], rwkv_a0[j], rwkv_a_lora_a[j], rwkv_a_lora_b[j], rwkv_g_lora_a[j],
                            rwkv_g_lora_b[j], rwkv_k_k[j], rwkv_k_a[j], rwkv_r_k[j], rwkv_ln_w[j], rwkv_ln_b[j],
                            rwkv_o, j, batch=batch, seq=seq)
        h = _ffn(h, ffn2_norm[i], ffn2_gu, ffn2_down, i, seq=seq)
    return h.reshape(batch, seq, d)
```

```python
import functools
import math

import jax
import jax.numpy as jnp
from jax import lax
from jax.experimental import pallas as pl
from jax.experimental.pallas import tpu as pltpu

F32 = jnp.float32
BF16 = jnp.bfloat16

NORM_EPS = 1e-6

MLSTM_HEADS = 8
MLSTM_DQK = 128
MLSTM_DV = 256
MLSTM_CHUNK = 64
MLSTM_GATE_PAD = 512
MLSTM_ROWS = 4

ATTN_HEAD_DIM = 64
ATTN_GROUP = 8
ATTN_WINDOW = 128
ROPE_DIM = 16
ROPE_THETA = 500000.0

RWKV_HEAD = 64
RWKV_CHUNK = 64
RWKV_LN_EPS = 64e-5
RWKV_LORA_PAD = 128
RWKV_PAIR_GROUP = 16
RWKV_PROLOGUE_ROWS = 256

LANES = 128
SUBLANES = 8
VMEM_LIMIT_BYTES = 56 * 1024 * 1024


def _compiler_params(semantics):
    return pltpu.CompilerParams(dimension_semantics=semantics, vmem_limit_bytes=VMEM_LIMIT_BYTES)


def _row_tile(seq, preferred):
    tile = preferred
    while seq % tile:
        tile //= 2
    return tile


def _dot(a, b):
    return jnp.dot(a, b, preferred_element_type=F32)


def _dot_nt(a, b):
    return lax.dot_general(a, b, (((1,), (1,)), ((), ())), preferred_element_type=F32)


def _dot_tn(a, b):
    return lax.dot_general(a, b, (((0,), (0,)), ((), ())), preferred_element_type=F32)


def _sigmoid(x):
    return 1.0 / (1.0 + jnp.exp(-x))


def _rmsnorm_rows(x, gain):
    ms = jnp.mean(x * x, axis=-1, keepdims=True)
    return x * lax.rsqrt(ms + NORM_EPS) * gain


def _split3(x):
    hi = x.astype(BF16)
    r1 = x - hi.astype(F32)
    mid = r1.astype(BF16)
    lo = (r1 - mid.astype(F32)).astype(BF16)
    return hi, mid, lo


def _cumsum_time(tril_bf, x):
    hi, mid, lo = _split3(x)
    return _dot(tril_bf, hi) + _dot(tril_bf, mid) + _dot(tril_bf, lo)


def _ffn_kernel(x_ref, gain_ref, wgu_hbm, wd_hbm, o_ref, xn_ref, wg_buf, wu_buf, wd_buf, sem, *, layer, nf, tf):
    i = pl.program_id(0)

    def chunk_copies(chunk, slot):
        gate_col, up_col = chunk * tf, (nf + chunk) * tf
        if not isinstance(chunk, int):
            gate_col, up_col = pl.multiple_of(gate_col, tf), pl.multiple_of(up_col, tf)
        return (
            pltpu.make_async_copy(wgu_hbm.at[layer, :, pl.ds(gate_col, tf)], wg_buf.at[slot], sem.at[0, slot]),
            pltpu.make_async_copy(wgu_hbm.at[layer, :, pl.ds(up_col, tf)], wu_buf.at[slot], sem.at[1, slot]),
            pltpu.make_async_copy(wd_hbm.at[layer, pl.ds(gate_col, tf), :], wd_buf.at[slot], sem.at[2, slot]),
        )

    def start(chunk, slot):
        for copy in chunk_copies(chunk, slot):
            copy.start()

    def wait(chunk, slot):
        for copy in chunk_copies(chunk, slot):
            copy.wait()

    @pl.when(i == 0)
    def _():
        start(0, 0)

    x = x_ref[...]
    xn_ref[...] = _rmsnorm_rows(x, gain_ref[...]).astype(BF16)
    o_ref[...] = x

    last_tile = i + 1 == pl.num_programs(0)

    def chunk_step(chunk, carry):
        slot = (i * nf + chunk) % 2
        more_here = chunk + 1 < nf

        @pl.when(more_here | jnp.logical_not(last_tile))
        def _():
            start(jnp.where(more_here, chunk + 1, 0), 1 - slot)

        wait(chunk, slot)
        xn = xn_ref[...]
        gate = _dot(xn, wg_buf[slot])
        up = _dot(xn, wu_buf[slot])
        act = (gate * _sigmoid(gate) * (0.5 * up)).astype(BF16)
        o_ref[...] += _dot(act, wd_buf[slot])
        return carry

    lax.fori_loop(0, nf, chunk_step, 0)


def _ffn(x, gain, w_gu, w_down, layer, *, seq):
    tokens, d = x.shape
    f = w_down.shape[1]
    tm = _row_tile(seq, 1024)
    tf = _col_tile(f, 768)
    nf = f // tf
    return pl.pallas_call(
        functools.partial(_ffn_kernel, layer=layer, nf=nf, tf=tf),
        grid=(tokens // tm,),
        in_specs=[
            pl.BlockSpec((tm, d), lambda i: (i, 0)),
            pl.BlockSpec((1, d), lambda i: (0, 0)),
            pl.BlockSpec(memory_space=pl.ANY),
            pl.BlockSpec(memory_space=pl.ANY),
        ],
        out_specs=pl.BlockSpec((tm, d), lambda i: (i, 0)),
        out_shape=jax.ShapeDtypeStruct((tokens, d), F32),
        scratch_shapes=[
            pltpu.VMEM((tm, d), BF16),
            pltpu.VMEM((2, d, tf), BF16),
            pltpu.VMEM((2, d, tf), BF16),
            pltpu.VMEM((2, tf, d), BF16),
            pltpu.SemaphoreType.DMA((3, 2)),
        ],
        compiler_params=_compiler_params(("arbitrary",)),
        name="ffn",
    )(x, gain.reshape(1, d), w_gu, w_down)


def _proj_kernel(*refs, norm, residual):
    refs = list(refs)
    x_ref = refs.pop(0)
    gain_ref = refs.pop(0) if norm else None
    w_ref = refs.pop(0)
    res_ref = refs.pop(0) if residual else None
    o_ref = refs.pop(0)

    if norm:
        xn_ref = refs.pop(0)

        @pl.when(pl.program_id(1) == 0)
        def _():
            xn_ref[...] = _rmsnorm_rows(x_ref[...], gain_ref[...]).astype(BF16)

        lhs = xn_ref[...]
    else:
        lhs = x_ref[...]

    acc = _dot(lhs, w_ref[...])
    if residual:
        acc = acc + res_ref[...]
    o_ref[...] = acc


def _col_tile(n, cap):
    best = LANES
    for t in range(LANES, cap + 1, LANES):
        if n % t == 0:
            best = t
    return best


def _proj(x, w, layer, *, seq, gain=None, residual=None):
    tokens, k = x.shape
    n = w.shape[2]
    norm = gain is not None
    assert norm or x.dtype == BF16
    tm = _row_tile(seq, 1024 if norm else 512)
    tn = _col_tile(n, 1024 if norm else 2048)
    in_specs = [pl.BlockSpec((tm, k), lambda i, j: (i, 0))]
    args = [x]
    if norm:
        in_specs.append(pl.BlockSpec((1, k), lambda i, j: (0, 0)))
        args.append(gain.reshape(1, k))
    in_specs.append(pl.BlockSpec((None, k, tn), lambda i, j: (layer, 0, j)))
    args.append(w)
    if residual is not None:
        in_specs.append(pl.BlockSpec((tm, tn), lambda i, j: (i, j)))
        args.append(residual)
    return pl.pallas_call(
        functools.partial(_proj_kernel, norm=norm, residual=residual is not None),
        grid=(tokens // tm, n // tn),
        in_specs=in_specs,
        out_specs=pl.BlockSpec((tm, tn), lambda i, j: (i, j)),
        out_shape=jax.ShapeDtypeStruct((tokens, n), F32),
        scratch_shapes=[pltpu.VMEM((tm, k), BF16)] if norm else [],
        compiler_params=_compiler_params(("parallel", "arbitrary")),
        name="proj",
    )(*args)


def _pad_cols(w, width):
    return jnp.pad(w, ((0, 0), (0, width - w.shape[1])))


def _mlstm_kernel(q_ref, k_ref, v_ref, og_ref, g_ref, bias_ref, hg_ref, o_ref, ct_ref, n_ref, m_ref):
    L, H, DK, DV = MLSTM_CHUNK, MLSTM_HEADS, MLSTM_DQK, MLSTM_DV
    R = q_ref.shape[0]

    @pl.when(pl.program_id(1) == 0)
    def _():
        ct_ref[...] = jnp.zeros_like(ct_ref)
        n_ref[...] = jnp.zeros_like(n_ref)
        m_ref[...] = jnp.zeros_like(m_ref)

    row = lax.broadcasted_iota(jnp.int32, (L, L), 0)
    col = lax.broadcasted_iota(jnp.int32, (L, L), 1)
    causal = row >= col
    tril_bf = causal.astype(BF16)

    z = [g_ref[r] + bias_ref[...] for r in range(R)]
    log_f = [jnp.minimum(t, 0.0) - jnp.log(1.0 + jnp.exp(-jnp.abs(t))) for t in z]
    bcum = [_cumsum_time(tril_bf, t) for t in log_f]
    z_t = [t.T for t in z]
    bcum_t = [t.T for t in bcum]

    units = range(R * H)
    rh = [divmod(u, H) for u in units]
    q = [q_ref[r, :, h * DK:(h + 1) * DK] for r, h in rh]
    k = [k_ref[r, :, h * DK:(h + 1) * DK] * (DK ** -0.5) for r, h in rh]
    q_bf = [t.astype(BF16) for t in q]
    k_bf = [t.astype(BF16) for t in k]
    v_bf = [v_ref[r, :, h * DV:(h + 1) * DV].astype(BF16) for r, h in rh]
    m_prev = [m_ref[u] for u in units]

    b_col = [bcum[r][:, H + h:H + h + 1] for r, h in rh]
    g = [bcum[r][L - 1:L, H + h:H + h + 1] for r, h in rh]
    qk = [_dot_nt(q_bf[u], k_bf[u]) for u in units]
    inter = [_dot(q_bf[u], ct_ref[u].astype(BF16)) for u in units]
    qn = [jnp.sum(q[u] * n_ref[u], axis=-1, keepdims=True) for u in units]

    log_d = [jnp.where(causal, b_col[u] - bcum_t[r][H + h:H + h + 1, :] + z_t[r][h:h + 1, :], -jnp.inf)
             for u, (r, h) in enumerate(rh)]
    row_max = [jnp.max(log_d[u], axis=-1, keepdims=True) for u in units]
    log_w = [g[u] - b_col[u] + z[r][:, h:h + 1] for u, (r, h) in enumerate(rh)]
    m_new = [jnp.maximum(g[u] + m_prev[u], jnp.max(log_w[u], axis=0, keepdims=True)) for u in units]
    kw = [k[u] * jnp.exp(log_w[u] - m_new[u]) for u in units]
    outer = [_dot_tn(kw[u].astype(BF16), v_bf[u]) for u in units]

    log_inter = [b_col[u] + m_prev[u] for u in units]
    m_out = [jnp.maximum(log_inter[u], row_max[u]) for u in units]
    s = [qk[u] * jnp.exp(log_d[u] - m_out[u]) for u in units]
    intra = [_dot(s[u].astype(BF16), v_bf[u]) for u in units]
    s_sum = [jnp.sum(s[u], axis=-1, keepdims=True) for u in units]
    w_inter = [jnp.exp(log_inter[u] - m_out[u]) for u in units]
    den = [s_sum[u] + w_inter[u] * qn[u] for u in units]
    scale = [1.0 / jnp.maximum(jnp.abs(den[u]), jnp.exp(-m_out[u])) for u in units]
    hout = [(intra[u] + w_inter[u] * inter[u]) * scale[u] for u in units]
    ms = [jnp.mean(hout[u] * hout[u], axis=-1, keepdims=True) for u in units]
    gate = [_sigmoid(og_ref[r, :, h * DV:(h + 1) * DV]) * hg_ref[:, h * DV:(h + 1) * DV] for r, h in rh]
    for u, (r, h) in enumerate(rh):
        o_ref[r, :, h * DV:(h + 1) * DV] = (hout[u] * lax.rsqrt(ms[u] + NORM_EPS) * gate[u]).astype(o_ref.dtype)

    for u in units:
        decay = jnp.exp(g[u] + m_prev[u] - m_new[u])
        ct_ref[u] = decay * ct_ref[u] + outer[u]
        n_ref[u] = decay * n_ref[u] + jnp.sum(kw[u], axis=0, keepdims=True)
        m_ref[u] = m_new[u]


def _mlstm_mixer(x, norm_gain, w_in_padded, b_gate, head_gain, w_out, layer, *, batch, seq):
    tokens, d = x.shape
    L, H, DK, DV = MLSTM_CHUNK, MLSTM_HEADS, MLSTM_DQK, MLSTM_DV
    R = MLSTM_ROWS if batch % MLSTM_ROWS == 0 else 1
    nc = seq // L
    main = 2 * H * DK + 2 * H * DV
    assert w_in_padded.shape[2] == main + MLSTM_GATE_PAD
    zin = _proj(x, w_in_padded, layer, seq=seq, gain=norm_gain).reshape(batch, seq, main + MLSTM_GATE_PAD)
    bias = jnp.pad(b_gate.reshape(1, 2 * H), ((0, 0), (0, LANES - 2 * H)))
    qk_w, v_w = H * DK, H * DV
    gated = pl.pallas_call(
        _mlstm_kernel,
        grid=(batch // R, nc),
        in_specs=[
            pl.BlockSpec((R, L, qk_w), lambda b, c: (b, c, 0)),
            pl.BlockSpec((R, L, qk_w), lambda b, c: (b, c, 1)),
            pl.BlockSpec((R, L, v_w), lambda b, c: (b, c, 2 * qk_w // v_w)),
            pl.BlockSpec((R, L, v_w), lambda b, c: (b, c, 2 * qk_w // v_w + 1)),
            pl.BlockSpec((R, L, LANES), lambda b, c: (b, c, main // LANES)),
            pl.BlockSpec((1, LANES), lambda b, c: (0, 0)),
            pl.BlockSpec((1, v_w), lambda b, c: (0, 0)),
        ],
        out_specs=pl.BlockSpec((R, L, v_w), lambda b, c: (b, c, 0)),
        out_shape=jax.ShapeDtypeStruct((batch, seq, v_w), BF16),
        scratch_shapes=[
            pltpu.VMEM((R * H, DK, DV), F32),
            pltpu.VMEM((R * H, 1, DK), F32),
            pltpu.VMEM((R * H, 1, 1), F32),
        ],
        compiler_params=_compiler_params(("parallel", "arbitrary")),
        name="mlstm_scan",
    )(zin, zin, zin, zin, zin, bias, head_gain.reshape(1, v_w))
    return _proj(gated.reshape(tokens, v_w), w_out, layer, seq=seq, residual=x)


def _qk_norm_rope(slabs, gain2, tables, lane):
    dh = ATTN_HEAD_DIM
    half = ROPE_DIM // 2
    low = lane % dh < half
    idx = range(len(slabs))
    r_i = lax.broadcasted_iota(jnp.int32, (LANES, LANES), 0)
    c_i = lax.broadcasted_iota(jnp.int32, (LANES, LANES), 1)
    same_head = ((r_i // dh) == (c_i // dh)).astype(BF16)
    sq = [t * t for t in slabs]
    sq_hi = [t.astype(BF16) for t in sq]
    sq_lo = [(sq[i] - sq_hi[i].astype(F32)).astype(BF16) for i in idx]
    ssq = [_dot(sq_hi[i], same_head) + _dot(sq_lo[i], same_head) for i in idx]
    tn = [slabs[i] * lax.rsqrt(ssq[i] * (1.0 / dh) + NORM_EPS) * gain2 for i in idx]
    up = [pltpu.roll(tn[i], LANES - half, 1) for i in idx]
    down = [pltpu.roll(tn[i], half, 1) for i in idx]
    return [tn[i] * tables[i][0] + jnp.where(low, up[i], down[i]) * tables[i][1] for i in idx]


def _swa_kernel(q_ref, kc_ref, kp_ref, vc_ref, vp_ref, pc_ref, pp_ref, freq_ref, qg_ref, kg_ref, sink_ref, o_ref):
    blk, dh, grp = ATTN_WINDOW, ATTN_HEAD_DIM, ATTN_GROUP
    half = ROPE_DIM // 2
    n_kv = kc_ref.shape[1] // dh
    has_prev = pl.program_id(1) > 0

    lane = lax.broadcasted_iota(jnp.int32, (blk, LANES), 1)
    in_rot = lane % dh < ROPE_DIM

    def rope_tables(pos_ref):
        ang = pos_ref[...].astype(F32) * freq_ref[...]
        cos_t = jnp.where(in_rot, jnp.cos(ang), 1.0)
        sin = jnp.sin(ang)
        sin_t = jnp.where(in_rot, jnp.where(lane % dh < half, -sin, sin), 0.0)
        return cos_t, sin_t

    cos_c, sin_c = rope_tables(pc_ref)
    cos_p, sin_p = rope_tables(pp_ref)

    qi = lax.broadcasted_iota(jnp.int32, (blk, 2 * blk), 0) + blk
    kj = lax.broadcasted_iota(jnp.int32, (blk, 2 * blk), 1)
    first_key = jnp.where(has_prev, 0, blk)
    ok = (qi >= kj) & (qi - kj < ATTN_WINDOW) & (kj >= first_key)
    scale = dh ** -0.5
    assert LANES == 2 * dh and scale == 0.125
    first = lane < dh
    first2 = lax.broadcasted_iota(jnp.int32, (2 * blk, LANES), 1) < dh
    key_row = lax.broadcasted_iota(jnp.int32, (2 * blk, LANES), 0)
    half_ones = ((lax.broadcasted_iota(jnp.int32, (4 * blk, LANES), 0) < 2 * blk)
                 == (lax.broadcasted_iota(jnp.int32, (4 * blk, LANES), 1) < dh)).astype(BF16)
    slabs_per_kv = grp * dh // LANES

    k_dup, v_split = [], []
    kv_slabs = n_kv * dh // LANES
    k_in = [kp_ref[:, c * LANES:(c + 1) * LANES] for c in range(kv_slabs)]
    k_in += [kc_ref[:, c * LANES:(c + 1) * LANES] for c in range(kv_slabs)]
    k_roped = _qk_norm_rope(k_in, kg_ref[...], [(cos_p, sin_p)] * kv_slabs + [(cos_c, sin_c)] * kv_slabs, lane)
    q_slabs = q_ref.shape[1] // LANES
    q_roped = _qk_norm_rope([q_ref[:, c * LANES:(c + 1) * LANES] for c in range(q_slabs)], qg_ref[...],
                            [(cos_c, sin_c)] * q_slabs, lane)
    for c in range(kv_slabs):
        sl = slice(c * LANES, (c + 1) * LANES)
        kcat = jnp.concatenate([k_roped[c], k_roped[kv_slabs + c]], axis=0)
        vcat = jnp.concatenate([vp_ref[:, sl], vc_ref[:, sl]], axis=0)
        vcat = jnp.where(key_row == 0, 0.0, vcat)
        k_rot = pltpu.roll(kcat, dh, 1)
        v_rot = pltpu.roll(vcat, dh, 1)
        k_dup.append(jnp.where(first2, kcat, k_rot).astype(BF16))
        k_dup.append(jnp.where(first2, k_rot, kcat).astype(BF16))
        v_split.append(jnp.concatenate([jnp.where(first2, vcat, 0.0), jnp.where(first2, 0.0, v_rot)],
                                       axis=0).astype(BF16))
        v_split.append(jnp.concatenate([jnp.where(first2, v_rot, 0.0), jnp.where(first2, 0.0, vcat)],
                                       axis=0).astype(BF16))

    s_all = []
    for kv in range(n_kv):
        parts = []
        for c in range(kv * slabs_per_kv, (kv + 1) * slabs_per_kv):
            q2 = q_roped[c] * scale
            parts += [jnp.where(first, q2, 0.0), jnp.where(first, 0.0, q2)]
        q_st = jnp.concatenate(parts, axis=0).astype(BF16)
        s_all.append(_dot_nt(q_st, k_dup[kv]))

    heads = range(n_kv * grp)
    s = [jnp.where(ok, s_all[h // grp][(h % grp) * blk:(h % grp + 1) * blk], sink_ref[h:h + 1, :]) for h in heads]
    m = [jnp.max(s[h], axis=-1, keepdims=True) for h in heads]
    probs = [jnp.exp(s[h] - m[h]).astype(BF16) for h in heads]
    slabs = range(q_slabs)
    p_cat = [jnp.concatenate([probs[2 * c], probs[2 * c + 1]], axis=1) for c in slabs]
    outs = [_dot(p_cat[c], v_split[c // slabs_per_kv]) for c in slabs]
    dens = [_dot(p_cat[c], half_ones) for c in slabs]
    for c in slabs:
        o_ref[:, c * LANES:(c + 1) * LANES] = (outs[c] / dens[c]).astype(o_ref.dtype)


def _swa_mixer(x, positions, norm_gain, w_qkv, q_gain, k_gain, sinks, w_o, layer, *, batch, seq):
    tokens, d = x.shape
    blk, dh = ATTN_WINDOW, ATTN_HEAD_DIM
    nb = seq // blk
    n_q = sinks.shape[0]
    q_w = n_q * dh
    kv_w = (w_qkv.shape[2] - q_w) // 2
    qkv = _proj(x, w_qkv, layer, seq=seq, gain=norm_gain)
    pos = positions.reshape(tokens, 1)
    inv_freq = ROPE_THETA ** (-jnp.arange(0, ROPE_DIM, 2, dtype=F32) / ROPE_DIM)
    freq_head = jnp.concatenate([inv_freq, inv_freq, jnp.zeros((dh - ROPE_DIM,), F32)])
    freq = jnp.tile(freq_head, LANES // dh).reshape(1, LANES)
    tile2 = lambda g: jnp.tile(g, LANES // dh).reshape(1, LANES)
    sink_rows = jnp.concatenate([sinks.reshape(n_q, 1), jnp.full((n_q, 2 * blk - 1), -jnp.inf, F32)], axis=1)
    cur = lambda b, i: b * nb + i
    prev = lambda b, i: b * nb + jnp.maximum(i - 1, 0)
    k_blk, v_blk = q_w // kv_w, q_w // kv_w + 1
    small = lambda b, i: (0, 0)
    attn = pl.pallas_call(
        _swa_kernel,
        grid=(batch, nb),
        in_specs=[
            pl.BlockSpec((blk, q_w), lambda b, i: (cur(b, i), 0)),
            pl.BlockSpec((blk, kv_w), lambda b, i: (cur(b, i), k_blk)),
            pl.BlockSpec((blk, kv_w), lambda b, i: (prev(b, i), k_blk)),
            pl.BlockSpec((blk, kv_w), lambda b, i: (cur(b, i), v_blk)),
            pl.BlockSpec((blk, kv_w), lambda b, i: (prev(b, i), v_blk)),
            pl.BlockSpec((blk, 1), lambda b, i: (cur(b, i), 0)),
            pl.BlockSpec((blk, 1), lambda b, i: (prev(b, i), 0)),
            pl.BlockSpec((1, LANES), small),
            pl.BlockSpec((1, LANES), small),
            pl.BlockSpec((1, LANES), small),
            pl.BlockSpec((n_q, 2 * blk), small),
        ],
        out_specs=pl.BlockSpec((blk, q_w), lambda b, i: (cur(b, i), 0)),
        out_shape=jax.ShapeDtypeStruct((tokens, q_w), BF16),
        compiler_params=_compiler_params(("parallel", "arbitrary")),
        name="swa",
    )(qkv, qkv, qkv, qkv, qkv, pos, pos, freq, tile2(q_gain), tile2(k_gain), sink_rows)
    return _proj(attn, w_o, layer, seq=seq, residual=x)


def _rwkv_proj_kernel(x_ref, xb_ref, gain_ref, mix_ref, wl_ref, w_ref, o_ref, lo_ref, xm_ref, *, per_proj, seq):
    j = pl.program_id(1)
    tm, d = x_ref.shape
    pad = RWKV_LORA_PAD

    @pl.when(j == 0)
    def _():
        gain = gain_ref[...]
        rc = min(RWKV_PROLOGUE_ROWS, tm)
        row = lax.broadcasted_iota(jnp.int32, (rc, d), 0)
        starts_sequence = (pl.program_id(0) * tm) % seq == 0
        for c in range(tm // rc):
            rows = slice(c * rc, (c + 1) * rc)
            hn = _rmsnorm_rows(x_ref[rows, :], gain)
            if c == 0:
                before = jnp.where(starts_sequence, 0.0, xb_ref[SUBLANES - 1:SUBLANES, :])
            else:
                before = x_ref[c * rc - 1:c * rc, :]
            dx = jnp.where(row == 0, _rmsnorm_rows(before, gain), pltpu.roll(hn, 1, 0)) - hn

            def mixed(i):
                return (hn + dx * mix_ref[i:i + 1, :]).astype(BF16)

            xm_ref[0, rows, :] = mixed(0)
            xm_ref[1, rows, :] = mixed(2)
            xm_ref[2, rows, :] = mixed(3)
            lo_ref[rows, 0:pad] = _dot(mixed(1), wl_ref[:, 0:pad])
            lo_ref[rows, pad:2 * pad] = _dot(mixed(4), wl_ref[:, pad:2 * pad])
            lo_ref[rows, 2 * pad:] = _dot(mixed(5), wl_ref[:, 2 * pad:])

    o_ref[...] = _dot(xm_ref[j // per_proj], w_ref[...])


def _rwkv_scan_kernel(r_ref, k_ref, v_ref, lo_ref, w0_ref, wlb_ref, a0_ref, alb_ref, glb_ref,
                      kk_ref, ka_ref, rk_ref, lnw_ref, lnb_ref, o_ref, st_ref):
    C, N = RWKV_CHUNK, RWKV_HEAD
    pad = RWKV_LORA_PAD
    n_pairs = r_ref.shape[1] // LANES

    @pl.when(pl.program_id(1) == 0)
    def _():
        st_ref[...] = jnp.zeros_like(st_ref)

    lane = lax.broadcasted_iota(jnp.int32, (C, LANES), 1)
    first = lane < N
    row_c = lax.broadcasted_iota(jnp.int32, (C, C), 0)
    col_c = lax.broadcasted_iota(jnp.int32, (C, C), 1)
    tril_bf = (row_c >= col_c).astype(BF16)
    row2 = lax.broadcasted_iota(jnp.int32, (2 * C, 2 * C), 0)
    col2 = lax.broadcasted_iota(jnp.int32, (2 * C, 2 * C), 1)
    same_head = (row2 // C) == (col2 // C)
    eye2 = row2 == col2
    row_s = lax.broadcasted_iota(jnp.int32, (C, 2 * C), 0)
    col_s = lax.broadcasted_iota(jnp.int32, (C, 2 * C), 1) % C
    strict_lower = row_s > col_s
    lower = row_s >= col_s
    eye_side = row_s == col_s

    def head_sum(t):
        s0 = jnp.sum(jnp.where(first, t, 0.0), axis=-1, keepdims=True)
        s1 = jnp.sum(jnp.where(first, 0.0, t), axis=-1, keepdims=True)
        return jnp.where(first, s0, s1)

    def stack(t):
        return jnp.concatenate([jnp.where(first, t, 0.0), jnp.where(first, 0.0, t)], axis=0)

    lw1 = jnp.tanh(lo_ref[:, 0:pad]).astype(BF16)
    la1 = lo_ref[:, pad:2 * pad].astype(BF16)
    lg1 = _sigmoid(lo_ref[:, 2 * pad:]).astype(BF16)

    r_all = r_ref[...]
    k_all = k_ref[...]
    zw = w0_ref[...] + _dot(lw1, wlb_ref[...])
    log_decay = -math.exp(-0.5) * _sigmoid(zw)
    a_all = _sigmoid(a0_ref[...] + _dot(la1, alb_ref[...]))
    g_all = _dot(lg1, glb_ref[...])
    kk_all = k_all * kk_ref[...]
    k2_all = k_all * (1.0 + (a_all - 1.0) * ka_ref[...])
    cum_all = _cumsum_time(tril_bf, log_decay)
    total_all = cum_all[C - 1:C, :]
    p_incl = jnp.exp(cum_all)
    p_excl = jnp.exp(cum_all - log_decay)
    inv_p = jnp.exp(-cum_all)
    to_end = jnp.exp(total_all - cum_all)
    rt_all = r_all * p_incl
    rk2_all = r_all * k2_all * rk_ref[...]

    h2 = 2 * C

    def block_diag(side):
        return jnp.where(same_head, jnp.concatenate([side, side], axis=0), 0.0).astype(BF16)

    def run_pairs(pairs):
        sls = {p: slice(p * LANES, (p + 1) * LANES) for p in pairs}
        lhs, rhs, v_sts, ends = {}, {}, {}, {}
        for p in pairs:
            sl = sls[p]
            kk = kk_all[:, sl]
            kk = kk * lax.rsqrt(jnp.maximum(head_sum(kk * kk), 1e-24))
            kka = kk * a_all[:, sl]
            a_t = -kk * p_excl[:, sl]
            b_t = kka * inv_p[:, sl]
            k_t = k2_all[:, sl] * inv_p[:, sl]
            b_e = kka * to_end[:, sl]
            k_e = k2_all[:, sl] * to_end[:, sl]
            lhs[p] = jnp.concatenate([a_t, rt_all[:, sl]], axis=0).astype(BF16)
            rhs[p] = jnp.concatenate([stack(b_t), stack(k_t)], axis=0).astype(BF16)
            ends[p] = jnp.concatenate([b_e, k_e], axis=0).astype(BF16)
            v_sts[p] = stack(v_ref[:, sl]).astype(BF16)

        grams = {p: _dot_nt(lhs[p], rhs[p]) for p in pairs}
        a_ak = {p: jnp.where(strict_lower, grams[p][:C, h2:], 0.0).astype(BF16) for p in pairs}
        a_rbk = {p: jnp.concatenate([jnp.where(lower, grams[p][C:, :h2], 0.0),
                                     jnp.where(lower, grams[p][C:, h2:], 0.0)], axis=1).astype(BF16)
                 for p in pairs}

        pw = {p: jnp.where(strict_lower, grams[p][:C, :h2], 0.0) for p in pairs}
        pw_bd = {p: block_diag(pw[p]) for p in pairs}
        t_side = {p: jnp.where(eye_side, 1.0, 0.0) + pw[p] for p in pairs}
        for _ in range(C.bit_length() - 2):
            pw = {p: _dot(pw[p].astype(BF16), pw_bd[p]) for p in pairs}
            pw_bd = {p: block_diag(pw[p]) for p in pairs}
            t_side = {p: t_side[p] + _dot(t_side[p].astype(BF16), pw_bd[p]) for p in pairs}

        m0 = {p: st_ref[p] for p in pairs}
        m0_bf = {p: m0[p].astype(BF16) for p in pairs}
        lm = {p: _dot(lhs[p], m0_bf[p]) for p in pairs}
        x_rhs = {p: lm[p][:C] + _dot(a_ak[p], v_sts[p]) for p in pairs}
        u = {p: _dot(t_side[p].astype(BF16), stack(x_rhs[p]).astype(BF16)) for p in pairs}
        uv_st = {p: jnp.concatenate([stack(u[p]).astype(BF16), v_sts[p]], axis=0) for p in pairs}
        y_all = {p: lm[p][C:] + _dot(a_rbk[p], uv_st[p]) for p in pairs}

        for p in pairs:
            total_col = jnp.sum(jnp.where(eye2, total_all[:, sls[p]], 0.0), axis=-1, keepdims=True)
            uv = jnp.concatenate([u[p], v_ref[:, sls[p]]], axis=0).astype(BF16)
            st_ref[p] = jnp.exp(total_col) * m0[p] + jnp.where(same_head, _dot_tn(ends[p], uv), 0.0)

        for p in pairs:
            sl = sls[p]
            y = y_all[p]
            mu = head_sum(y) * (1.0 / N)
            yc = y - mu
            var = head_sum(yc * yc) * (1.0 / N)
            yn = yc * lax.rsqrt(var + RWKV_LN_EPS) * lnw_ref[:, sl] + lnb_ref[:, sl]
            bonus = head_sum(rk2_all[:, sl])
            o_ref[:, sl] = ((yn + bonus * v_ref[:, sl]) * g_all[:, sl]).astype(o_ref.dtype)

    group = min(RWKV_PAIR_GROUP, n_pairs)
    for first_pair in range(0, n_pairs, group):
        run_pairs(range(first_pair, first_pair + group))


def _rwkv_mixer(x, norm_gain, mix, w_rkv, w0, w_la, w_lb, a0, a_la, a_lb, g_la, g_lb, k_k, k_a, r_k,
                ln_w, ln_b, w_o, layer, *, batch, seq):
    tokens, d = x.shape
    C = RWKV_CHUNK
    pad = RWKV_LORA_PAD
    tm = _row_tile(seq, 1024)
    tn = 512
    per_proj = d // tn
    n_tiles = tokens // tm
    lora_w = w_la.shape[1], a_la.shape[1], g_la.shape[1]
    lora_cols = 2 * pad + lora_w[2]
    assert lora_w[0] <= pad and lora_w[1] <= pad and 2 * C == LANES
    w_lora = jnp.concatenate([_pad_cols(w_la, pad), _pad_cols(a_la, pad), g_la], axis=1).astype(BF16)
    proj, lora = pl.pallas_call(
        functools.partial(_rwkv_proj_kernel, per_proj=per_proj, seq=seq),
        grid=(n_tiles, 3 * per_proj),
        in_specs=[
            pl.BlockSpec((tm, d), lambda i, j: (i, 0)),
            pl.BlockSpec((SUBLANES, d), lambda i, j: (jnp.maximum(i * (tm // SUBLANES) - 1, 0), 0)),
            pl.BlockSpec((1, d), lambda i, j: (0, 0)),
            pl.BlockSpec((6, d), lambda i, j: (0, 0)),
            pl.BlockSpec((d, lora_cols), lambda i, j: (0, 0)),
            pl.BlockSpec((None, d, tn), lambda i, j: (j // per_proj, 0, j % per_proj)),
        ],
        out_specs=[
            pl.BlockSpec((tm, tn), lambda i, j: (i, j)),
            pl.BlockSpec((tm, lora_cols), lambda i, j: (i, 0)),
        ],
        out_shape=[
            jax.ShapeDtypeStruct((tokens, 3 * d), F32),
            jax.ShapeDtypeStruct((tokens, lora_cols), F32),
        ],
        scratch_shapes=[pltpu.VMEM((3, tm, d), BF16)],
        compiler_params=_compiler_params(("parallel", "arbitrary")),
        name="rwkv_proj",
    )(x, x, norm_gain.reshape(1, d), mix, w_lora, w_rkv.astype(BF16))

    nc = seq // C
    row1 = lambda t: t.reshape(1, d)
    pad_rows = lambda w: jnp.pad(w, ((0, pad - w.shape[0]), (0, 0))).astype(BF16)
    chunk = lambda blk: (lambda b, c: (b * nc + c, blk))
    const = lambda b, c: (0, 0)
    vec = pl.BlockSpec((1, d), const)
    mixed = pl.pallas_call(
        _rwkv_scan_kernel,
        grid=(batch, nc),
        in_specs=[
            pl.BlockSpec((C, d), chunk(0)),
            pl.BlockSpec((C, d), chunk(1)),
            pl.BlockSpec((C, d), chunk(2)),
            pl.BlockSpec((C, lora_cols), chunk(0)),
            vec,
            pl.BlockSpec((pad, d), const),
            vec,
            pl.BlockSpec((pad, d), const),
            pl.BlockSpec((lora_w[2], d), const),
            vec, vec, vec, vec, vec,
        ],
        out_specs=pl.BlockSpec((C, d), chunk(0)),
        out_shape=jax.ShapeDtypeStruct((tokens, d), BF16),
        scratch_shapes=[pltpu.VMEM((d // LANES, LANES, LANES), F32)],
        compiler_params=_compiler_params(("parallel", "arbitrary")),
        name="rwkv_scan",
    )(proj, proj, proj, lora, row1(w0), pad_rows(w_lb), row1(a0), pad_rows(a_lb), g_lb.astype(BF16),
      row1(k_k), row1(k_a), row1(r_k), row1(ln_w), row1(ln_b))
    return _proj(mixed, w_o, layer, seq=seq, residual=x)


def kernel(x, positions, ffn1_norm, ffn1_w_gu, ffn1_w_down, mixer_norm, ffn2_norm, ffn2_w_gu, ffn2_w_down,
           mlstm_w_in, mlstm_b_gate, mlstm_head_gain, mlstm_w_out,
           attn_w_qkv, attn_q_gain, attn_k_gain, attn_sinks, attn_w_o,
           rwkv_mix, rwkv_w_rkv, rwkv_w0, rwkv_w_lora_a, rwkv_w_lora_b, rwkv_a0, rwkv_a_lora_a, rwkv_a_lora_b,
           rwkv_g_lora_a, rwkv_g_lora_b, rwkv_k_k, rwkv_k_a, rwkv_r_k, rwkv_ln_w, rwkv_ln_b, rwkv_w_o):
    batch, seq, d = x.shape
    depth = ffn1_norm.shape[0]
    h = x.reshape(batch * seq, d)
    ffn1_gu, ffn1_down = ffn1_w_gu.astype(BF16), ffn1_w_down.astype(BF16)
    ffn2_gu, ffn2_down = ffn2_w_gu.astype(BF16), ffn2_w_down.astype(BF16)
    mlstm_in = jnp.pad(mlstm_w_in, ((0, 0), (0, 0), (0, MLSTM_GATE_PAD - 2 * MLSTM_HEADS))).astype(BF16)
    mlstm_out = mlstm_w_out.astype(BF16)
    attn_qkv, attn_o = attn_w_qkv.astype(BF16), attn_w_o.astype(BF16)
    rwkv_o = rwkv_w_o.astype(BF16)
    for i in range(depth):
        h = _ffn(h, ffn1_norm[i], ffn1_gu, ffn1_down, i, seq=seq)
        kind, j = i % 3, i // 3
        if kind == 0:
            h = _mlstm_mixer(h, mixer_norm[i], mlstm_in, mlstm_b_gate[j], mlstm_head_gain[j], mlstm_out, j,
                             batch=batch, seq=seq)
        elif kind == 1:
            h = _swa_mixer(h, positions, mixer_norm[i], attn_qkv, attn_q_gain[j], attn_k_gain[j],
                           attn_sinks[j], attn_o, j, batch=batch, seq=seq)
        else:
            h = _rwkv_mixer(h, mixer_norm[i], rwkv_mix[j], rwkv_w_rkv[j], rwkv_w0[j], rwkv_w_lora_a[j],
                            rwkv_w_lora_b[j], rwkv_a0[j], rwkv_a_lora_a[j], rwkv_a_lora_b[j], rwkv_g_lora_a[j],
                            rwkv_g_lora_b[j], rwkv_k_k[j], rwkv_k_a[j], rwkv_r_k[j], rwkv_ln_w[j], rwkv_ln_b[j],
                            rwkv_o, j, batch=batch, seq=seq)
        h = _ffn(h, ffn2_norm[i], ffn2_gu, ffn2_down, i, seq=seq)
    return h.reshape(batch, seq, d)
```

```python
import functools
import math

import jax
import jax.numpy as jnp
from jax import lax
from jax.experimental import pallas as pl
from jax.experimental.pallas import tpu as pltpu

F32 = jnp.float32
BF16 = jnp.bfloat16

NORM_EPS = 1e-6

MLSTM_HEADS = 8
MLSTM_DQK = 128
MLSTM_DV = 256
MLSTM_CHUNK = 64
MLSTM_GATE_PAD = 512
MLSTM_ROWS = 4

ATTN_HEAD_DIM = 64
ATTN_GROUP = 8
ATTN_WINDOW = 128
ROPE_DIM = 16
ROPE_THETA = 500000.0

RWKV_HEAD = 64
RWKV_CHUNK = 64
RWKV_LN_EPS = 64e-5
RWKV_LORA_PAD = 128
RWKV_PAIR_GROUP = 16
RWKV_PROLOGUE_ROWS = 256

LANES = 128
SUBLANES = 8
VMEM_LIMIT_BYTES = 56 * 1024 * 1024


def _compiler_params(semantics):
    return pltpu.CompilerParams(dimension_semantics=semantics, vmem_limit_bytes=VMEM_LIMIT_BYTES)


def _row_tile(seq, preferred):
    tile = preferred
    while seq % tile:
        tile //= 2
    return tile


def _dot(a, b):
    return jnp.dot(a, b, preferred_element_type=F32)


def _dot_nt(a, b):
    return lax.dot_general(a, b, (((1,), (1,)), ((), ())), preferred_element_type=F32)


def _dot_tn(a, b):
    return lax.dot_general(a, b, (((0,), (0,)), ((), ())), preferred_element_type=F32)


def _sigmoid(x):
    return 1.0 / (1.0 + jnp.exp(-x))


def _rmsnorm_rows(x, gain):
    ms = jnp.mean(x * x, axis=-1, keepdims=True)
    return x * lax.rsqrt(ms + NORM_EPS) * gain


def _split3(x):
    hi = x.astype(BF16)
    r1 = x - hi.astype(F32)
    mid = r1.astype(BF16)
    lo = (r1 - mid.astype(F32)).astype(BF16)
    return hi, mid, lo


def _cumsum_time(tril_bf, x):
    hi, mid, lo = _split3(x)
    return _dot(tril_bf, hi) + _dot(tril_bf, mid) + _dot(tril_bf, lo)


def _ffn_kernel(x_ref, gain_ref, wgu_hbm, wd_hbm, o_ref, xn_ref, wg_buf, wu_buf, wd_buf, sem, *, layer, nf, tf):
    i = pl.program_id(0)

    def chunk_copies(chunk, slot):
        gate_col, up_col = chunk * tf, (nf + chunk) * tf
        if not isinstance(chunk, int):
            gate_col, up_col = pl.multiple_of(gate_col, tf), pl.multiple_of(up_col, tf)
        return (
            pltpu.make_async_copy(wgu_hbm.at[layer, :, pl.ds(gate_col, tf)], wg_buf.at[slot], sem.at[0, slot]),
            pltpu.make_async_copy(wgu_hbm.at[layer, :, pl.ds(up_col, tf)], wu_buf.at[slot], sem.at[1, slot]),
            pltpu.make_async_copy(wd_hbm.at[layer, pl.ds(gate_col, tf), :], wd_buf.at[slot], sem.at[2, slot]),
        )

    def start(chunk, slot):
        for copy in chunk_copies(chunk, slot):
            copy.start()

    def wait(chunk, slot):
        for copy in chunk_copies(chunk, slot):
            copy.wait()

    @pl.when(i == 0)
    def _():
        start(0, 0)

    x = x_ref[...]
    xn_ref[...] = _rmsnorm_rows(x, gain_ref[...]).astype(BF16)
    o_ref[...] = x

    last_tile = i + 1 == pl.num_programs(0)

    def chunk_step(chunk, carry):
        slot = (i * nf + chunk) % 2
        more_here = chunk + 1 < nf

        @pl.when(more_here | jnp.logical_not(last_tile))
        def _():
            start(jnp.where(more_here, chunk + 1, 0), 1 - slot)

        wait(chunk, slot)
        xn = xn_ref[...]
        gate = _dot(xn, wg_buf[slot])
        up = _dot(xn, wu_buf[slot])
        act = (gate * _sigmoid(gate) * (0.5 * up)).astype(BF16)
        o_ref[...] += _dot(act, wd_buf[slot])
        return carry

    lax.fori_loop(0, nf, chunk_step, 0)


def _ffn(x, gain, w_gu, w_down, layer, *, seq):
    tokens, d = x.shape
    f = w_down.shape[1]
    tm = _row_tile(seq, 1024)
    tf = _col_tile(f, 768)
    nf = f // tf
    return pl.pallas_call(
        functools.partial(_ffn_kernel, layer=layer, nf=nf, tf=tf),
        grid=(tokens // tm,),
        in_specs=[
            pl.BlockSpec((tm, d), lambda i: (i, 0)),
            pl.BlockSpec((1, d), lambda i: (0, 0)),
            pl.BlockSpec(memory_space=pl.ANY),
            pl.BlockSpec(memory_space=pl.ANY),
        ],
        out_specs=pl.BlockSpec((tm, d), lambda i: (i, 0)),
        out_shape=jax.ShapeDtypeStruct((tokens, d), F32),
        scratch_shapes=[
            pltpu.VMEM((tm, d), BF16),
            pltpu.VMEM((2, d, tf), BF16),
            pltpu.VMEM((2, d, tf), BF16),
            pltpu.VMEM((2, tf, d), BF16),
            pltpu.SemaphoreType.DMA((3, 2)),
        ],
        compiler_params=_compiler_params(("arbitrary",)),
        name="ffn",
    )(x, gain.reshape(1, d), w_gu, w_down)


def _proj_kernel(*refs, norm, residual):
    refs = list(refs)
    x_ref = refs.pop(0)
    gain_ref = refs.pop(0) if norm else None
    w_ref = refs.pop(0)
    res_ref = refs.pop(0) if residual else None
    o_ref = refs.pop(0)

    if norm:
        xn_ref = refs.pop(0)

        @pl.when(pl.program_id(1) == 0)
        def _():
            xn_ref[...] = _rmsnorm_rows(x_ref[...], gain_ref[...]).astype(BF16)

        lhs = xn_ref[...]
    else:
        lhs = x_ref[...]

    acc = _dot(lhs, w_ref[...])
    if residual:
        acc = acc + res_ref[...]
    o_ref[...] = acc


def _col_tile(n, cap):
    best = LANES
    for t in range(LANES, cap + 1, LANES):
        if n % t == 0:
            best = t
    return best


def _proj(x, w, layer, *, seq, gain=None, residual=None):
    tokens, k = x.shape
    n = w.shape[2]
    norm = gain is not None
    assert norm or x.dtype == BF16
    tm = _row_tile(seq, 1024 if norm else 512)
    tn = _col_tile(n, 1280 if norm else 2048)
    in_specs = [pl.BlockSpec((tm, k), lambda i, j: (i, 0))]
    args = [x]
    if norm:
        in_specs.append(pl.BlockSpec((1, k), lambda i, j: (0, 0)))
        args.append(gain.reshape(1, k))
    in_specs.append(pl.BlockSpec((None, k, tn), lambda i, j: (layer, 0, j)))
    args.append(w)
    if residual is not None:
        in_specs.append(pl.BlockSpec((tm, tn), lambda i, j: (i, j)))
        args.append(residual)
    return pl.pallas_call(
        functools.partial(_proj_kernel, norm=norm, residual=residual is not None),
        grid=(tokens // tm, n // tn),
        in_specs=in_specs,
        out_specs=pl.BlockSpec((tm, tn), lambda i, j: (i, j)),
        out_shape=jax.ShapeDtypeStruct((tokens, n), F32),
        scratch_shapes=[pltpu.VMEM((tm, k), BF16)] if norm else [],
        compiler_params=_compiler_params(("parallel", "arbitrary")),
        name="proj",
    )(*args)


def _pad_cols(w, width):
    return jnp.pad(w, ((0, 0), (0, width - w.shape[1])))


def _mlstm_kernel(q_ref, k_ref, v_ref, og_ref, g_ref, bias_ref, hg_ref, o_ref, ct_ref, n_ref, m_ref):
    L, H, DK, DV = MLSTM_CHUNK, MLSTM_HEADS, MLSTM_DQK, MLSTM_DV
    R = q_ref.shape[0]

    @pl.when(pl.program_id(1) == 0)
    def _():
        ct_ref[...] = jnp.zeros_like(ct_ref)
        n_ref[...] = jnp.zeros_like(n_ref)
        m_ref[...] = jnp.zeros_like(m_ref)

    row = lax.broadcasted_iota(jnp.int32, (L, L), 0)
    col = lax.broadcasted_iota(jnp.int32, (L, L), 1)
    causal = row >= col
    tril_bf = causal.astype(BF16)

    z = [g_ref[r] + bias_ref[...] for r in range(R)]
    log_f = [jnp.minimum(t, 0.0) - jnp.log(1.0 + jnp.exp(-jnp.abs(t))) for t in z]
    bcum = [_cumsum_time(tril_bf, t) for t in log_f]
    z_t = [t.T for t in z]
    bcum_t = [t.T for t in bcum]

    units = range(R * H)
    rh = [divmod(u, H) for u in units]
    q = [q_ref[r, :, h * DK:(h + 1) * DK] for r, h in rh]
    k = [k_ref[r, :, h * DK:(h + 1) * DK] * (DK ** -0.5) for r, h in rh]
    q_bf = [t.astype(BF16) for t in q]
    k_bf = [t.astype(BF16) for t in k]
    v_bf = [v_ref[r, :, h * DV:(h + 1) * DV].astype(BF16) for r, h in rh]
    m_prev = [m_ref[u] for u in units]

    b_col = [bcum[r][:, H + h:H + h + 1] for r, h in rh]
    g = [bcum[r][L - 1:L, H + h:H + h + 1] for r, h in rh]
    qk = [_dot_nt(q_bf[u], k_bf[u]) for u in units]
    inter = [_dot(q_bf[u], ct_ref[u].astype(BF16)) for u in units]
    qn = [jnp.sum(q[u] * n_ref[u], axis=-1, keepdims=True) for u in units]

    log_d = [jnp.where(causal, b_col[u] - bcum_t[r][H + h:H + h + 1, :] + z_t[r][h:h + 1, :], -jnp.inf)
             for u, (r, h) in enumerate(rh)]
    row_max = [jnp.max(log_d[u], axis=-1, keepdims=True) for u in units]
    log_w = [g[u] - b_col[u] + z[r][:, h:h + 1] for u, (r, h) in enumerate(rh)]
    m_new = [jnp.maximum(g[u] + m_prev[u], jnp.max(log_w[u], axis=0, keepdims=True)) for u in units]
    kw = [k[u] * jnp.exp(log_w[u] - m_new[u]) for u in units]
    outer = [_dot_tn(kw[u].astype(BF16), v_bf[u]) for u in units]

    log_inter = [b_col[u] + m_prev[u] for u in units]
    m_out = [jnp.maximum(log_inter[u], row_max[u]) for u in units]
    s = [qk[u] * jnp.exp(log_d[u] - m_out[u]) for u in units]
    intra = [_dot(s[u].astype(BF16), v_bf[u]) for u in units]
    s_sum = [jnp.sum(s[u], axis=-1, keepdims=True) for u in units]
    w_inter = [jnp.exp(log_inter[u] - m_out[u]) for u in units]
    den = [s_sum[u] + w_inter[u] * qn[u] for u in units]
    scale = [1.0 / jnp.maximum(jnp.abs(den[u]), jnp.exp(-m_out[u])) for u in units]
    hout = [(intra[u] + w_inter[u] * inter[u]) * scale[u] for u in units]
    ms = [jnp.mean(hout[u] * hout[u], axis=-1, keepdims=True) for u in units]
    gate = [_sigmoid(og_ref[r, :, h * DV:(h + 1) * DV]) * hg_ref[:, h * DV:(h + 1) * DV] for r, h in rh]
    for u, (r, h) in enumerate(rh):
        o_ref[r, :, h * DV:(h + 1) * DV] = (hout[u] * lax.rsqrt(ms[u] + NORM_EPS) * gate[u]).astype(o_ref.dtype)

    for u in units:
        decay = jnp.exp(g[u] + m_prev[u] - m_new[u])
        ct_ref[u] = decay * ct_ref[u] + outer[u]
        n_ref[u] = decay * n_ref[u] + jnp.sum(kw[u], axis=0, keepdims=True)
        m_ref[u] = m_new[u]


def _mlstm_mixer(x, norm_gain, w_in_padded, b_gate, head_gain, w_out, layer, *, batch, seq):
    tokens, d = x.shape
    L, H, DK, DV = MLSTM_CHUNK, MLSTM_HEADS, MLSTM_DQK, MLSTM_DV
    R = MLSTM_ROWS if batch % MLSTM_ROWS == 0 else 1
    nc = seq // L
    main = 2 * H * DK + 2 * H * DV
    assert w_in_padded.shape[2] == main + MLSTM_GATE_PAD
    zin = _proj(x, w_in_padded, layer, seq=seq, gain=norm_gain).reshape(batch, seq, main + MLSTM_GATE_PAD)
    bias = jnp.pad(b_gate.reshape(1, 2 * H), ((0, 0), (0, LANES - 2 * H)))
    qk_w, v_w = H * DK, H * DV
    gated = pl.pallas_call(
        _mlstm_kernel,
        grid=(batch // R, nc),
        in_specs=[
            pl.BlockSpec((R, L, qk_w), lambda b, c: (b, c, 0)),
            pl.BlockSpec((R, L, qk_w), lambda b, c: (b, c, 1)),
            pl.BlockSpec((R, L, v_w), lambda b, c: (b, c, 2 * qk_w // v_w)),
            pl.BlockSpec((R, L, v_w), lambda b, c: (b, c, 2 * qk_w // v_w + 1)),
            pl.BlockSpec((R, L, LANES), lambda b, c: (b, c, main // LANES)),
            pl.BlockSpec((1, LANES), lambda b, c: (0, 0)),
            pl.BlockSpec((1, v_w), lambda b, c: (0, 0)),
        ],
        out_specs=pl.BlockSpec((R, L, v_w), lambda b, c: (b, c, 0)),
        out_shape=jax.ShapeDtypeStruct((batch, seq, v_w), BF16),
        scratch_shapes=[
            pltpu.VMEM((R * H, DK, DV), F32),
            pltpu.VMEM((R * H, 1, DK), F32),
            pltpu.VMEM((R * H, 1, 1), F32),
        ],
        compiler_params=_compiler_params(("parallel", "arbitrary")),
        name="mlstm_scan",
    )(zin, zin, zin, zin, zin, bias, head_gain.reshape(1, v_w))
    return _proj(gated.reshape(tokens, v_w), w_out, layer, seq=seq, residual=x)


def _qk_norm_rope(slabs, gain2, tables, lane):
    dh = ATTN_HEAD_DIM
    half = ROPE_DIM // 2
    low = lane % dh < half
    idx = range(len(slabs))
    r_i = lax.broadcasted_iota(jnp.int32, (LANES, LANES), 0)
    c_i = lax.broadcasted_iota(jnp.int32, (LANES, LANES), 1)
    same_head = ((r_i // dh) == (c_i // dh)).astype(BF16)
    sq = [t * t for t in slabs]
    sq_hi = [t.astype(BF16) for t in sq]
    sq_lo = [(sq[i] - sq_hi[i].astype(F32)).astype(BF16) for i in idx]
    ssq = [_dot(sq_hi[i], same_head) + _dot(sq_lo[i], same_head) for i in idx]
    tn = [slabs[i] * lax.rsqrt(ssq[i] * (1.0 / dh) + NORM_EPS) * gain2 for i in idx]
    up = [pltpu.roll(tn[i], LANES - half, 1) for i in idx]
    down = [pltpu.roll(tn[i], half, 1) for i in idx]
    return [tn[i] * tables[i][0] + jnp.where(low, up[i], down[i]) * tables[i][1] for i in idx]


def _swa_kernel(q_ref, kc_ref, kp_ref, vc_ref, vp_ref, pc_ref, pp_ref, freq_ref, qg_ref, kg_ref, sink_ref, o_ref):
    blk, dh, grp = ATTN_WINDOW, ATTN_HEAD_DIM, ATTN_GROUP
    half = ROPE_DIM // 2
    n_kv = kc_ref.shape[1] // dh
    has_prev = pl.program_id(1) > 0

    lane = lax.broadcasted_iota(jnp.int32, (blk, LANES), 1)
    in_rot = lane % dh < ROPE_DIM

    src = lax.broadcasted_iota(jnp.int32, (LANES, LANES), 0)
    dst = lax.broadcasted_iota(jnp.int32, (LANES, LANES), 1) % dh
    hit = (src < half) & ((dst == src) | (dst == src + half))
    spread_cos = hit.astype(BF16)
    spread_sin = jnp.where(hit, jnp.where(dst < half, -1.0, 1.0), 0.0).astype(BF16)
    pad_rows = jnp.zeros((LANES - half, blk), F32)

    def spread(values, matrix):
        by_token = jnp.concatenate([values, pad_rows], axis=0).T
        hi, mid, lo = _split3(by_token)
        return _dot(hi, matrix) + _dot(mid, matrix) + _dot(lo, matrix)

    def rope_tables(pos_ref):
        ang = freq_ref[...] * pos_ref[...].astype(F32)
        cos_t = spread(jnp.cos(ang), spread_cos) + jnp.where(in_rot, 0.0, 1.0)
        return cos_t, spread(jnp.sin(ang), spread_sin)

    cos_c, sin_c = rope_tables(pc_ref)
    cos_p, sin_p = rope_tables(pp_ref)

    qi = lax.broadcasted_iota(jnp.int32, (blk, 2 * blk), 0) + blk
    kj = lax.broadcasted_iota(jnp.int32, (blk, 2 * blk), 1)
    first_key = jnp.where(has_prev, 0, blk)
    ok = (qi >= kj) & (qi - kj < ATTN_WINDOW) & (kj >= first_key)
    scale = dh ** -0.5
    assert LANES == 2 * dh and scale == 0.125
    first = lane < dh
    first2 = lax.broadcasted_iota(jnp.int32, (2 * blk, LANES), 1) < dh
    key_row = lax.broadcasted_iota(jnp.int32, (2 * blk, LANES), 0)
    half_ones = ((lax.broadcasted_iota(jnp.int32, (4 * blk, LANES), 0) < 2 * blk)
                 == (lax.broadcasted_iota(jnp.int32, (4 * blk, LANES), 1) < dh)).astype(BF16)
    slabs_per_kv = grp * dh // LANES

    k_dup, v_split = [], []
    kv_slabs = n_kv * dh // LANES
    k_in = [kp_ref[:, c * LANES:(c + 1) * LANES] for c in range(kv_slabs)]
    k_in += [kc_ref[:, c * LANES:(c + 1) * LANES] for c in range(kv_slabs)]
    k_roped = _qk_norm_rope(k_in, kg_ref[...], [(cos_p, sin_p)] * kv_slabs + [(cos_c, sin_c)] * kv_slabs, lane)
    q_slabs = q_ref.shape[1] // LANES
    q_roped = _qk_norm_rope([q_ref[:, c * LANES:(c + 1) * LANES] for c in range(q_slabs)], qg_ref[...],
                            [(cos_c, sin_c)] * q_slabs, lane)
    for c in range(kv_slabs):
        sl = slice(c * LANES, (c + 1) * LANES)
        kcat = jnp.concatenate([k_roped[c], k_roped[kv_slabs + c]], axis=0)
        vcat = jnp.concatenate([vp_ref[:, sl], vc_ref[:, sl]], axis=0)
        vcat = jnp.where(key_row == 0, 0.0, vcat)
        k_rot = pltpu.roll(kcat, dh, 1)
        v_rot = pltpu.roll(vcat, dh, 1)
        k_dup.append(jnp.where(first2, kcat, k_rot).astype(BF16))
        k_dup.append(jnp.where(first2, k_rot, kcat).astype(BF16))
        v_split.append(jnp.concatenate([jnp.where(first2, vcat, 0.0), jnp.where(first2, 0.0, v_rot)],
                                       axis=0).astype(BF16))
        v_split.append(jnp.concatenate([jnp.where(first2, v_rot, 0.0), jnp.where(first2, 0.0, vcat)],
                                       axis=0).astype(BF16))

    s_all = []
    for kv in range(n_kv):
        parts = []
        for c in range(kv * slabs_per_kv, (kv + 1) * slabs_per_kv):
            q2 = q_roped[c] * scale
            parts += [jnp.where(first, q2, 0.0), jnp.where(first, 0.0, q2)]
        q_st = jnp.concatenate(parts, axis=0).astype(BF16)
        s_all.append(_dot_nt(q_st, k_dup[kv]))

    heads = range(n_kv * grp)
    s = [jnp.where(ok, s_all[h // grp][(h % grp) * blk:(h % grp + 1) * blk], sink_ref[h:h + 1, :]) for h in heads]
    m = [jnp.max(s[h], axis=-1, keepdims=True) for h in heads]
    probs = [jnp.exp(s[h] - m[h]).astype(BF16) for h in heads]
    slabs = range(q_slabs)
    p_cat = [jnp.concatenate([probs[2 * c], probs[2 * c + 1]], axis=1) for c in slabs]
    outs = [_dot(p_cat[c], v_split[c // slabs_per_kv]) for c in slabs]
    dens = [_dot(p_cat[c], half_ones) for c in slabs]
    for c in slabs:
        o_ref[:, c * LANES:(c + 1) * LANES] = (outs[c] / dens[c]).astype(o_ref.dtype)


def _swa_mixer(x, positions, norm_gain, w_qkv, q_gain, k_gain, sinks, w_o, layer, *, batch, seq):
    tokens, d = x.shape
    blk, dh = ATTN_WINDOW, ATTN_HEAD_DIM
    nb = seq // blk
    n_q = sinks.shape[0]
    q_w = n_q * dh
    kv_w = (w_qkv.shape[2] - q_w) // 2
    qkv = _proj(x, w_qkv, layer, seq=seq, gain=norm_gain)
    assert blk == LANES
    pos = positions.reshape(tokens // blk, 1, blk)
    inv_freq = ROPE_THETA ** (-jnp.arange(0, ROPE_DIM, 2, dtype=F32) / ROPE_DIM)
    freq = inv_freq.reshape(ROPE_DIM // 2, 1)
    tile2 = lambda g: jnp.tile(g, LANES // dh).reshape(1, LANES)
    sink_rows = jnp.concatenate([sinks.reshape(n_q, 1), jnp.full((n_q, 2 * blk - 1), -jnp.inf, F32)], axis=1)
    cur = lambda b, i: b * nb + i
    prev = lambda b, i: b * nb + jnp.maximum(i - 1, 0)
    k_blk, v_blk = q_w // kv_w, q_w // kv_w + 1
    small = lambda b, i: (0, 0)
    attn = pl.pallas_call(
        _swa_kernel,
        grid=(batch, nb),
        in_specs=[
            pl.BlockSpec((blk, q_w), lambda b, i: (cur(b, i), 0)),
            pl.BlockSpec((blk, kv_w), lambda b, i: (cur(b, i), k_blk)),
            pl.BlockSpec((blk, kv_w), lambda b, i: (prev(b, i), k_blk)),
            pl.BlockSpec((blk, kv_w), lambda b, i: (cur(b, i), v_blk)),
            pl.BlockSpec((blk, kv_w), lambda b, i: (prev(b, i), v_blk)),
            pl.BlockSpec((None, 1, blk), lambda b, i: (cur(b, i), 0, 0)),
            pl.BlockSpec((None, 1, blk), lambda b, i: (prev(b, i), 0, 0)),
            pl.BlockSpec((ROPE_DIM // 2, 1), small),
            pl.BlockSpec((1, LANES), small),
            pl.BlockSpec((1, LANES), small),
            pl.BlockSpec((n_q, 2 * blk), small),
        ],
        out_specs=pl.BlockSpec((blk, q_w), lambda b, i: (cur(b, i), 0)),
        out_shape=jax.ShapeDtypeStruct((tokens, q_w), BF16),
        compiler_params=_compiler_params(("parallel", "arbitrary")),
        name="swa",
    )(qkv, qkv, qkv, qkv, qkv, pos, pos, freq, tile2(q_gain), tile2(k_gain), sink_rows)
    return _proj(attn, w_o, layer, seq=seq, residual=x)


def _rwkv_proj_kernel(x_ref, xb_ref, gain_ref, mix_ref, wl_ref, w_ref, o_ref, lo_ref, xm_ref, *, per_proj, seq):
    j = pl.program_id(1)
    tm, d = x_ref.shape
    pad = RWKV_LORA_PAD

    @pl.when(j == 0)
    def _():
        gain = gain_ref[...]
        rc = min(RWKV_PROLOGUE_ROWS, tm)
        row = lax.broadcasted_iota(jnp.int32, (rc, d), 0)
        starts_sequence = (pl.program_id(0) * tm) % seq == 0
        for c in range(tm // rc):
            rows = slice(c * rc, (c + 1) * rc)
            hn = _rmsnorm_rows(x_ref[rows, :], gain)
            if c == 0:
                before = jnp.where(starts_sequence, 0.0, xb_ref[SUBLANES - 1:SUBLANES, :])
            else:
                before = x_ref[c * rc - 1:c * rc, :]
            dx = jnp.where(row == 0, _rmsnorm_rows(before, gain), pltpu.roll(hn, 1, 0)) - hn

            def mixed(i):
                return (hn + dx * mix_ref[i:i + 1, :]).astype(BF16)

            xm_ref[0, rows, :] = mixed(0)
            xm_ref[1, rows, :] = mixed(2)
            xm_ref[2, rows, :] = mixed(3)
            lo_ref[rows, 0:pad] = _dot(mixed(1), wl_ref[:, 0:pad])
            lo_ref[rows, pad:2 * pad] = _dot(mixed(4), wl_ref[:, pad:2 * pad])
            lo_ref[rows, 2 * pad:] = _dot(mixed(5), wl_ref[:, 2 * pad:])

    o_ref[...] = _dot(xm_ref[j // per_proj], w_ref[...])


def _rwkv_scan_kernel(r_ref, k_ref, v_ref, lo_ref, w0_ref, wlb_ref, a0_ref, alb_ref, glb_ref,
                      kk_ref, ka_ref, rk_ref, lnw_ref, lnb_ref, o_ref, st_ref):
    C, N = RWKV_CHUNK, RWKV_HEAD
    pad = RWKV_LORA_PAD
    n_pairs = r_ref.shape[1] // LANES

    @pl.when(pl.program_id(1) == 0)
    def _():
        st_ref[...] = jnp.zeros_like(st_ref)

    lane = lax.broadcasted_iota(jnp.int32, (C, LANES), 1)
    first = lane < N
    row_c = lax.broadcasted_iota(jnp.int32, (C, C), 0)
    col_c = lax.broadcasted_iota(jnp.int32, (C, C), 1)
    tril_bf = (row_c >= col_c).astype(BF16)
    row2 = lax.broadcasted_iota(jnp.int32, (2 * C, 2 * C), 0)
    col2 = lax.broadcasted_iota(jnp.int32, (2 * C, 2 * C), 1)
    same_head = (row2 // C) == (col2 // C)
    eye2 = row2 == col2
    row_s = lax.broadcasted_iota(jnp.int32, (C, 2 * C), 0)
    col_s = lax.broadcasted_iota(jnp.int32, (C, 2 * C), 1) % C
    strict_lower = row_s > col_s
    lower = row_s >= col_s
    eye_side = row_s == col_s

    def head_sum(t):
        s0 = jnp.sum(jnp.where(first, t, 0.0), axis=-1, keepdims=True)
        s1 = jnp.sum(jnp.where(first, 0.0, t), axis=-1, keepdims=True)
        return jnp.where(first, s0, s1)

    def stack(t):
        return jnp.concatenate([jnp.where(first, t, 0.0), jnp.where(first, 0.0, t)], axis=0)

    lw1 = jnp.tanh(lo_ref[:, 0:pad]).astype(BF16)
    la1 = lo_ref[:, pad:2 * pad].astype(BF16)
    lg1 = _sigmoid(lo_ref[:, 2 * pad:]).astype(BF16)

    r_all = r_ref[...]
    k_all = k_ref[...]
    zw = w0_ref[...] + _dot(lw1, wlb_ref[...])
    log_decay = -math.exp(-0.5) * _sigmoid(zw)
    a_all = _sigmoid(a0_ref[...] + _dot(la1, alb_ref[...]))
    g_all = _dot(lg1, glb_ref[...])
    kk_all = k_all * kk_ref[...]
    k2_all = k_all * (1.0 + (a_all - 1.0) * ka_ref[...])
    cum_all = _cumsum_time(tril_bf, log_decay)
    total_all = cum_all[C - 1:C, :]
    p_incl = jnp.exp(cum_all)
    p_excl = jnp.exp(cum_all - log_decay)
    inv_p = jnp.exp(-cum_all)
    to_end = jnp.exp(total_all - cum_all)
    rt_all = r_all * p_incl
    rk2_all = r_all * k2_all * rk_ref[...]

    h2 = 2 * C

    def block_diag(side):
        return jnp.where(same_head, jnp.concatenate([side, side], axis=0), 0.0).astype(BF16)

    def run_pairs(pairs):
        sls = {p: slice(p * LANES, (p + 1) * LANES) for p in pairs}
        lhs, rhs, v_sts, ends = {}, {}, {}, {}
        for p in pairs:
            sl = sls[p]
            kk = kk_all[:, sl]
            kk = kk * lax.rsqrt(jnp.maximum(head_sum(kk * kk), 1e-24))
            kka = kk * a_all[:, sl]
            a_t = -kk * p_excl[:, sl]
            b_t = kka * inv_p[:, sl]
            k_t = k2_all[:, sl] * inv_p[:, sl]
            b_e = kka * to_end[:, sl]
            k_e = k2_all[:, sl] * to_end[:, sl]
            lhs[p] = jnp.concatenate([a_t, rt_all[:, sl]], axis=0).astype(BF16)
            rhs[p] = jnp.concatenate([stack(b_t), stack(k_t)], axis=0).astype(BF16)
            ends[p] = jnp.concatenate([b_e, k_e], axis=0).astype(BF16)
            v_sts[p] = stack(v_ref[:, sl]).astype(BF16)

        grams = {p: _dot_nt(lhs[p], rhs[p]) for p in pairs}
        a_ak = {p: jnp.where(strict_lower, grams[p][:C, h2:], 0.0).astype(BF16) for p in pairs}
        a_rbk = {p: jnp.concatenate([jnp.where(lower, grams[p][C:, :h2], 0.0),
                                     jnp.where(lower, grams[p][C:, h2:], 0.0)], axis=1).astype(BF16)
                 for p in pairs}

        pw = {p: jnp.where(strict_lower, grams[p][:C, :h2], 0.0) for p in pairs}
        pw_bd = {p: block_diag(pw[p]) for p in pairs}
        t_side = {p: jnp.where(eye_side, 1.0, 0.0) + pw[p] for p in pairs}
        for _ in range(C.bit_length() - 2):
            pw = {p: _dot(pw[p].astype(BF16), pw_bd[p]) for p in pairs}
            pw_bd = {p: block_diag(pw[p]) for p in pairs}
            t_side = {p: t_side[p] + _dot(t_side[p].astype(BF16), pw_bd[p]) for p in pairs}

        m0 = {p: st_ref[p] for p in pairs}
        m0_bf = {p: m0[p].astype(BF16) for p in pairs}
        lm = {p: _dot(lhs[p], m0_bf[p]) for p in pairs}
        x_rhs = {p: lm[p][:C] + _dot(a_ak[p], v_sts[p]) for p in pairs}
        u = {p: _dot(t_side[p].astype(BF16), stack(x_rhs[p]).astype(BF16)) for p in pairs}
        uv_st = {p: jnp.concatenate([stack(u[p]).astype(BF16), v_sts[p]], axis=0) for p in pairs}
        y_all = {p: lm[p][C:] + _dot(a_rbk[p], uv_st[p]) for p in pairs}

        for p in pairs:
            total_col = jnp.sum(jnp.where(eye2, total_all[:, sls[p]], 0.0), axis=-1, keepdims=True)
            uv = jnp.concatenate([u[p], v_ref[:, sls[p]]], axis=0).astype(BF16)
            st_ref[p] = jnp.exp(total_col) * m0[p] + jnp.where(same_head, _dot_tn(ends[p], uv), 0.0)

        for p in pairs:
            sl = sls[p]
            y = y_all[p]
            mu = head_sum(y) * (1.0 / N)
            yc = y - mu
            var = head_sum(yc * yc) * (1.0 / N)
            yn = yc * lax.rsqrt(var + RWKV_LN_EPS) * lnw_ref[:, sl] + lnb_ref[:, sl]
            bonus = head_sum(rk2_all[:, sl])
            o_ref[:, sl] = ((yn + bonus * v_ref[:, sl]) * g_all[:, sl]).astype(o_ref.dtype)

    group = min(RWKV_PAIR_GROUP, n_pairs)
    for first_pair in range(0, n_pairs, group):
        run_pairs(range(first_pair, first_pair + group))


def _rwkv_mixer(x, norm_gain, mix, w_rkv, w0, w_la, w_lb, a0, a_la, a_lb, g_la, g_lb, k_k, k_a, r_k,
                ln_w, ln_b, w_o, layer, *, batch, seq):
    tokens, d = x.shape
    C = RWKV_CHUNK
    pad = RWKV_LORA_PAD
    tm = _row_tile(seq, 1024)
    tn = 512
    per_proj = d // tn
    n_tiles = tokens // tm
    lora_w = w_la.shape[1], a_la.shape[1], g_la.shape[1]
    lora_cols = 2 * pad + lora_w[2]
    assert lora_w[0] <= pad and lora_w[1] <= pad and 2 * C == LANES
    w_lora = jnp.concatenate([_pad_cols(w_la, pad), _pad_cols(a_la, pad), g_la], axis=1).astype(BF16)
    proj, lora = pl.pallas_call(
        functools.partial(_rwkv_proj_kernel, per_proj=per_proj, seq=seq),
        grid=(n_tiles, 3 * per_proj),
        in_specs=[
            pl.BlockSpec((tm, d), lambda i, j: (i, 0)),
            pl.BlockSpec((SUBLANES, d), lambda i, j: (jnp.maximum(i * (tm // SUBLANES) - 1, 0), 0)),
            pl.BlockSpec((1, d), lambda i, j: (0, 0)),
            pl.BlockSpec((6, d), lambda i, j: (0, 0)),
            pl.BlockSpec((d, lora_cols), lambda i, j: (0, 0)),
            pl.BlockSpec((None, d, tn), lambda i, j: (j // per_proj, 0, j % per_proj)),
        ],
        out_specs=[
            pl.BlockSpec((tm, tn), lambda i, j: (i, j)),
            pl.BlockSpec((tm, lora_cols), lambda i, j: (i, 0)),
        ],
        out_shape=[
            jax.ShapeDtypeStruct((tokens, 3 * d), F32),
            jax.ShapeDtypeStruct((tokens, lora_cols), F32),
        ],
        scratch_shapes=[pltpu.VMEM((3, tm, d), BF16)],
        compiler_params=_compiler_params(("parallel", "arbitrary")),
        name="rwkv_proj",
    )(x, x, norm_gain.reshape(1, d), mix, w_lora, w_rkv.astype(BF16))

    nc = seq // C
    row1 = lambda t: t.reshape(1, d)
    pad_rows = lambda w: jnp.pad(w, ((0, pad - w.shape[0]), (0, 0))).astype(BF16)
    chunk = lambda blk: (lambda b, c: (b * nc + c, blk))
    const = lambda b, c: (0, 0)
    vec = pl.BlockSpec((1, d), const)
    mixed = pl.pallas_call(
        _rwkv_scan_kernel,
        grid=(batch, nc),
        in_specs=[
            pl.BlockSpec((C, d), chunk(0)),
            pl.BlockSpec((C, d), chunk(1)),
            pl.BlockSpec((C, d), chunk(2)),
            pl.BlockSpec((C, lora_cols), chunk(0)),
            vec,
            pl.BlockSpec((pad, d), const),
            vec,
            pl.BlockSpec((pad, d), const),
            pl.BlockSpec((lora_w[2], d), const),
            vec, vec, vec, vec, vec,
        ],
        out_specs=pl.BlockSpec((C, d), chunk(0)),
        out_shape=jax.ShapeDtypeStruct((tokens, d), BF16),
        scratch_shapes=[pltpu.VMEM((d // LANES, LANES, LANES), F32)],
        compiler_params=_compiler_params(("parallel", "arbitrary")),
        name="rwkv_scan",
    )(proj, proj, proj, lora, row1(w0), pad_rows(w_lb), row1(a0), pad_rows(a_lb), g_lb.astype(BF16),
      row1(k_k), row1(k_a), row1(r_k), row1(ln_w), row1(ln_b))
    return _proj(mixed, w_o, layer, seq=seq, residual=x)


def kernel(x, positions, ffn1_norm, ffn1_w_gu, ffn1_w_down, mixer_norm, ffn2_norm, ffn2_w_gu, ffn2_w_down,
           mlstm_w_in, mlstm_b_gate, mlstm_head_gain, mlstm_w_out,
           attn_w_qkv, attn_q_gain, attn_k_gain, attn_sinks, attn_w_o,
           rwkv_mix, rwkv_w_rkv, rwkv_w0, rwkv_w_lora_a, rwkv_w_lora_b, rwkv_a0, rwkv_a_lora_a, rwkv_a_lora_b,
           rwkv_g_lora_a, rwkv_g_lora_b, rwkv_k_k, rwkv_k_a, rwkv_r_k, rwkv_ln_w, rwkv_ln_b, rwkv_w_o):
    batch, seq, d = x.shape
    depth = ffn1_norm.shape[0]
    h = x.reshape(batch * seq, d)
    ffn1_gu, ffn1_down = ffn1_w_gu.astype(BF16), ffn1_w_down.astype(BF16)
    ffn2_gu, ffn2_down = ffn2_w_gu.astype(BF16), ffn2_w_down.astype(BF16)
    mlstm_in = jnp.pad(mlstm_w_in, ((0, 0), (0, 0), (0, MLSTM_GATE_PAD - 2 * MLSTM_HEADS))).astype(BF16)
    mlstm_out = mlstm_w_out.astype(BF16)
    attn_qkv, attn_o = attn_w_qkv.astype(BF16), attn_w_o.astype(BF16)
    rwkv_o = rwkv_w_o.astype(BF16)
    for i in range(depth):
        h = _ffn(h, ffn1_norm[i], ffn1_gu, ffn1_down, i, seq=seq)
        kind, j = i % 3, i // 3
        if kind == 0:
            h = _mlstm_mixer(h, mixer_norm[i], mlstm_in, mlstm_b_gate[j], mlstm_head_gain[j], mlstm_out, j,
                             batch=batch, seq=seq)
        elif kind == 1:
            h = _swa_mixer(h, positions, mixer_norm[i], attn_qkv, attn_q_gain[j], attn_k_gain[j],
                           attn_sinks[j], attn_o, j, batch=batch, seq=seq)
        else:
            h = _rwkv_mixer(h, mixer_norm[i], rwkv_mix[j], rwkv_w_rkv[j], rwkv_w0[j], rwkv_w_lora_a[j],
                            rwkv_w_lora_b[j], rwkv_a0[j], rwkv_a_lora_a[j], rwkv_a_lora_b[j], rwkv_g_lora_a[j],
                            rwkv_g_lora_b[j], rwkv_k_k[j], rwkv_k_a[j], rwkv_r_k[j], rwkv_ln_w[j], rwkv_ln_b[j],
                            rwkv_o, j, batch=batch, seq=seq)
        h = _ffn(h, ffn2_norm[i], ffn2_gu, ffn2_down, i, seq=seq)
    return h.reshape(batch, seq, d)
```

```python
import functools
import math

import jax
import jax.numpy as jnp
from jax import lax
from jax.experimental import pallas as pl
from jax.experimental.pallas import tpu as pltpu

F32 = jnp.float32
BF16 = jnp.bfloat16

NORM_EPS = 1e-6

MLSTM_HEADS = 8
MLSTM_DQK = 128
MLSTM_DV = 256
MLSTM_CHUNK = 64
MLSTM_GATE_PAD = 512
MLSTM_ROWS = 4

ATTN_HEAD_DIM = 64
ATTN_GROUP = 8
ATTN_WINDOW = 128
ROPE_DIM = 16
ROPE_THETA = 500000.0

RWKV_HEAD = 64
RWKV_CHUNK = 64
RWKV_LN_EPS = 64e-5
RWKV_LORA_PAD = 128
RWKV_PAIR_GROUP = 16
RWKV_PROLOGUE_ROWS = 256

LANES = 128
SUBLANES = 8
VMEM_LIMIT_BYTES = 56 * 1024 * 1024


def _compiler_params(semantics):
    return pltpu.CompilerParams(dimension_semantics=semantics, vmem_limit_bytes=VMEM_LIMIT_BYTES)


def _row_tile(seq, preferred):
    tile = preferred
    while seq % tile:
        tile //= 2
    return tile


def _dot(a, b):
    return jnp.dot(a, b, preferred_element_type=F32)


def _dot_nt(a, b):
    return lax.dot_general(a, b, (((1,), (1,)), ((), ())), preferred_element_type=F32)


def _dot_tn(a, b):
    return lax.dot_general(a, b, (((0,), (0,)), ((), ())), preferred_element_type=F32)


def _sigmoid(x):
    return 1.0 / (1.0 + jnp.exp(-x))


def _rmsnorm_rows(x, gain):
    ms = jnp.mean(x * x, axis=-1, keepdims=True)
    return x * lax.rsqrt(ms + NORM_EPS) * gain


def _split3(x):
    hi = x.astype(BF16)
    r1 = x - hi.astype(F32)
    mid = r1.astype(BF16)
    lo = (r1 - mid.astype(F32)).astype(BF16)
    return hi, mid, lo


def _cumsum_time(tril_bf, x):
    hi, mid, lo = _split3(x)
    return _dot(tril_bf, hi) + _dot(tril_bf, mid) + _dot(tril_bf, lo)


def _ffn_kernel(x_ref, gain_ref, wgu_hbm, wd_hbm, o_ref, xn_ref, wg_buf, wu_buf, wd_buf, sem, *, layer, nf, tf):
    i = pl.program_id(0)

    def chunk_copies(chunk, slot):
        gate_col, up_col = chunk * tf, (nf + chunk) * tf
        if not isinstance(chunk, int):
            gate_col, up_col = pl.multiple_of(gate_col, tf), pl.multiple_of(up_col, tf)
        return (
            pltpu.make_async_copy(wgu_hbm.at[layer, :, pl.ds(gate_col, tf)], wg_buf.at[slot], sem.at[0, slot]),
            pltpu.make_async_copy(wgu_hbm.at[layer, :, pl.ds(up_col, tf)], wu_buf.at[slot], sem.at[1, slot]),
            pltpu.make_async_copy(wd_hbm.at[layer, pl.ds(gate_col, tf), :], wd_buf.at[slot], sem.at[2, slot]),
        )

    def start(chunk, slot):
        for copy in chunk_copies(chunk, slot):
            copy.start()

    def wait(chunk, slot):
        for copy in chunk_copies(chunk, slot):
            copy.wait()

    @pl.when(i == 0)
    def _():
        start(0, 0)

    x = x_ref[...]
    xn_ref[...] = _rmsnorm_rows(x, gain_ref[...]).astype(BF16)
    o_ref[...] = x

    last_tile = i + 1 == pl.num_programs(0)

    def chunk_step(chunk, carry):
        slot = (i * nf + chunk) % 2
        more_here = chunk + 1 < nf

        @pl.when(more_here | jnp.logical_not(last_tile))
        def _():
            start(jnp.where(more_here, chunk + 1, 0), 1 - slot)

        wait(chunk, slot)
        xn = xn_ref[...]
        gate = _dot(xn, wg_buf[slot])
        up = _dot(xn, wu_buf[slot])
        act = (gate * _sigmoid(gate) * (0.5 * up)).astype(BF16)
        o_ref[...] += _dot(act, wd_buf[slot])
        return carry

    lax.fori_loop(0, nf, chunk_step, 0)


def _ffn(x, gain, w_gu, w_down, layer, *, seq):
    tokens, d = x.shape
    f = w_down.shape[1]
    tm = _row_tile(seq, 1024)
    tf = _col_tile(f, 768)
    nf = f // tf
    return pl.pallas_call(
        functools.partial(_ffn_kernel, layer=layer, nf=nf, tf=tf),
        grid=(tokens // tm,),
        in_specs=[
            pl.BlockSpec((tm, d), lambda i: (i, 0)),
            pl.BlockSpec((1, d), lambda i: (0, 0)),
            pl.BlockSpec(memory_space=pl.ANY),
            pl.BlockSpec(memory_space=pl.ANY),
        ],
        out_specs=pl.BlockSpec((tm, d), lambda i: (i, 0)),
        out_shape=jax.ShapeDtypeStruct((tokens, d), F32),
        scratch_shapes=[
            pltpu.VMEM((tm, d), BF16),
            pltpu.VMEM((2, d, tf), BF16),
            pltpu.VMEM((2, d, tf), BF16),
            pltpu.VMEM((2, tf, d), BF16),
            pltpu.SemaphoreType.DMA((3, 2)),
        ],
        compiler_params=_compiler_params(("arbitrary",)),
        name="ffn",
    )(x, gain.reshape(1, d), w_gu, w_down)


def _proj_kernel(*refs, norm, residual):
    refs = list(refs)
    x_ref = refs.pop(0)
    gain_ref = refs.pop(0) if norm else None
    w_ref = refs.pop(0)
    res_ref = refs.pop(0) if residual else None
    o_ref = refs.pop(0)

    if norm:
        xn_ref = refs.pop(0)

        @pl.when(pl.program_id(1) == 0)
        def _():
            xn_ref[...] = _rmsnorm_rows(x_ref[...], gain_ref[...]).astype(BF16)

        lhs = xn_ref[...]
    else:
        lhs = x_ref[...]

    acc = _dot(lhs, w_ref[...])
    if residual:
        acc = acc + res_ref[...]
    o_ref[...] = acc


def _col_tile(n, cap):
    best = LANES
    for t in range(LANES, cap + 1, LANES):
        if n % t == 0:
            best = t
    return best


def _proj(x, w, layer, *, seq, gain=None, residual=None):
    tokens, k = x.shape
    n = w.shape[2]
    norm = gain is not None
    assert norm or x.dtype == BF16
    tm = _row_tile(seq, 1024 if norm else 512)
    tn = _col_tile(n, 1664 if norm else 2048)
    in_specs = [pl.BlockSpec((tm, k), lambda i, j: (i, 0))]
    args = [x]
    if norm:
        in_specs.append(pl.BlockSpec((1, k), lambda i, j: (0, 0)))
        args.append(gain.reshape(1, k))
    in_specs.append(pl.BlockSpec((None, k, tn), lambda i, j: (layer, 0, j)))
    args.append(w)
    if residual is not None:
        in_specs.append(pl.BlockSpec((tm, tn), lambda i, j: (i, j)))
        args.append(residual)
    return pl.pallas_call(
        functools.partial(_proj_kernel, norm=norm, residual=residual is not None),
        grid=(tokens // tm, n // tn),
        in_specs=in_specs,
        out_specs=pl.BlockSpec((tm, tn), lambda i, j: (i, j)),
        out_shape=jax.ShapeDtypeStruct((tokens, n), F32),
        scratch_shapes=[pltpu.VMEM((tm, k), BF16)] if norm else [],
        compiler_params=_compiler_params(("parallel", "arbitrary")),
        name="proj",
    )(*args)


def _pad_cols(w, width):
    return jnp.pad(w, ((0, 0), (0, width - w.shape[1])))


def _mlstm_kernel(q_ref, k_ref, v_ref, og_ref, g_ref, bias_ref, hg_ref, o_ref, ct_ref, n_ref, m_ref):
    L, H, DK, DV = MLSTM_CHUNK, MLSTM_HEADS, MLSTM_DQK, MLSTM_DV
    R = q_ref.shape[0]

    @pl.when(pl.program_id(1) == 0)
    def _():
        ct_ref[...] = jnp.zeros_like(ct_ref)
        n_ref[...] = jnp.zeros_like(n_ref)
        m_ref[...] = jnp.zeros_like(m_ref)

    row = lax.broadcasted_iota(jnp.int32, (L, L), 0)
    col = lax.broadcasted_iota(jnp.int32, (L, L), 1)
    causal = row >= col
    tril_bf = causal.astype(BF16)

    z = [g_ref[r] + bias_ref[...] for r in range(R)]
    log_f = [jnp.minimum(t, 0.0) - jnp.log(1.0 + jnp.exp(-jnp.abs(t))) for t in z]
    bcum = [_cumsum_time(tril_bf, t) for t in log_f]
    z_t = [t.T for t in z]
    bcum_t = [t.T for t in bcum]

    units = range(R * H)
    rh = [divmod(u, H) for u in units]
    q = [q_ref[r, :, h * DK:(h + 1) * DK] for r, h in rh]
    k = [k_ref[r, :, h * DK:(h + 1) * DK] * (DK ** -0.5) for r, h in rh]
    q_bf = [t.astype(BF16) for t in q]
    k_bf = [t.astype(BF16) for t in k]
    v_bf = [v_ref[r, :, h * DV:(h + 1) * DV].astype(BF16) for r, h in rh]
    m_prev = [m_ref[u] for u in units]

    b_col = [bcum[r][:, H + h:H + h + 1] for r, h in rh]
    g = [bcum[r][L - 1:L, H + h:H + h + 1] for r, h in rh]
    qk = [_dot_nt(q_bf[u], k_bf[u]) for u in units]
    inter = [_dot(q_bf[u], ct_ref[u].astype(BF16)) for u in units]
    qn = [jnp.sum(q[u] * n_ref[u], axis=-1, keepdims=True) for u in units]

    log_d = [jnp.where(causal, b_col[u] - bcum_t[r][H + h:H + h + 1, :] + z_t[r][h:h + 1, :], -jnp.inf)
             for u, (r, h) in enumerate(rh)]
    row_max = [jnp.max(log_d[u], axis=-1, keepdims=True) for u in units]
    log_w = [g[u] - b_col[u] + z[r][:, h:h + 1] for u, (r, h) in enumerate(rh)]
    m_new = [jnp.maximum(g[u] + m_prev[u], jnp.max(log_w[u], axis=0, keepdims=True)) for u in units]
    kw = [k[u] * jnp.exp(log_w[u] - m_new[u]) for u in units]
    outer = [_dot_tn(kw[u].astype(BF16), v_bf[u]) for u in units]

    log_inter = [b_col[u] + m_prev[u] for u in units]
    m_out = [jnp.maximum(log_inter[u], row_max[u]) for u in units]
    s = [qk[u] * jnp.exp(log_d[u] - m_out[u]) for u in units]
    intra = [_dot(s[u].astype(BF16), v_bf[u]) for u in units]
    s_sum = [jnp.sum(s[u], axis=-1, keepdims=True) for u in units]
    w_inter = [jnp.exp(log_inter[u] - m_out[u]) for u in units]
    den = [s_sum[u] + w_inter[u] * qn[u] for u in units]
    scale = [1.0 / jnp.maximum(jnp.abs(den[u]), jnp.exp(-m_out[u])) for u in units]
    hout = [(intra[u] + w_inter[u] * inter[u]) * scale[u] for u in units]
    ms = [jnp.mean(hout[u] * hout[u], axis=-1, keepdims=True) for u in units]
    gate = [_sigmoid(og_ref[r, :, h * DV:(h + 1) * DV]) * hg_ref[:, h * DV:(h + 1) * DV] for r, h in rh]
    for u, (r, h) in enumerate(rh):
        o_ref[r, :, h * DV:(h + 1) * DV] = (hout[u] * lax.rsqrt(ms[u] + NORM_EPS) * gate[u]).astype(o_ref.dtype)

    for u in units:
        decay = jnp.exp(g[u] + m_prev[u] - m_new[u])
        ct_ref[u] = decay * ct_ref[u] + outer[u]
        n_ref[u] = decay * n_ref[u] + jnp.sum(kw[u], axis=0, keepdims=True)
        m_ref[u] = m_new[u]


def _mlstm_mixer(x, norm_gain, w_in_padded, b_gate, head_gain, w_out, layer, *, batch, seq):
    tokens, d = x.shape
    L, H, DK, DV = MLSTM_CHUNK, MLSTM_HEADS, MLSTM_DQK, MLSTM_DV
    R = MLSTM_ROWS if batch % MLSTM_ROWS == 0 else 1
    nc = seq // L
    main = 2 * H * DK + 2 * H * DV
    assert w_in_padded.shape[2] == main + MLSTM_GATE_PAD
    zin = _proj(x, w_in_padded, layer, seq=seq, gain=norm_gain).reshape(batch, seq, main + MLSTM_GATE_PAD)
    bias = jnp.pad(b_gate.reshape(1, 2 * H), ((0, 0), (0, LANES - 2 * H)))
    qk_w, v_w = H * DK, H * DV
    gated = pl.pallas_call(
        _mlstm_kernel,
        grid=(batch // R, nc),
        in_specs=[
            pl.BlockSpec((R, L, qk_w), lambda b, c: (b, c, 0)),
            pl.BlockSpec((R, L, qk_w), lambda b, c: (b, c, 1)),
            pl.BlockSpec((R, L, v_w), lambda b, c: (b, c, 2 * qk_w // v_w)),
            pl.BlockSpec((R, L, v_w), lambda b, c: (b, c, 2 * qk_w // v_w + 1)),
            pl.BlockSpec((R, L, LANES), lambda b, c: (b, c, main // LANES)),
            pl.BlockSpec((1, LANES), lambda b, c: (0, 0)),
            pl.BlockSpec((1, v_w), lambda b, c: (0, 0)),
        ],
        out_specs=pl.BlockSpec((R, L, v_w), lambda b, c: (b, c, 0)),
        out_shape=jax.ShapeDtypeStruct((batch, seq, v_w), BF16),
        scratch_shapes=[
            pltpu.VMEM((R * H, DK, DV), F32),
            pltpu.VMEM((R * H, 1, DK), F32),
            pltpu.VMEM((R * H, 1, 1), F32),
        ],
        compiler_params=_compiler_params(("parallel", "arbitrary")),
        name="mlstm_scan",
    )(zin, zin, zin, zin, zin, bias, head_gain.reshape(1, v_w))
    return _proj(gated.reshape(tokens, v_w), w_out, layer, seq=seq, residual=x)


def _qk_norm_rope(slabs, gain2, tables, lane):
    dh = ATTN_HEAD_DIM
    half = ROPE_DIM // 2
    low = lane % dh < half
    idx = range(len(slabs))
    r_i = lax.broadcasted_iota(jnp.int32, (LANES, LANES), 0)
    c_i = lax.broadcasted_iota(jnp.int32, (LANES, LANES), 1)
    same_head = ((r_i // dh) == (c_i // dh)).astype(BF16)
    sq = [t * t for t in slabs]
    sq_hi = [t.astype(BF16) for t in sq]
    sq_lo = [(sq[i] - sq_hi[i].astype(F32)).astype(BF16) for i in idx]
    ssq = [_dot(sq_hi[i], same_head) + _dot(sq_lo[i], same_head) for i in idx]
    tn = [slabs[i] * lax.rsqrt(ssq[i] * (1.0 / dh) + NORM_EPS) * gain2 for i in idx]
    up = [pltpu.roll(tn[i], LANES - half, 1) for i in idx]
    down = [pltpu.roll(tn[i], half, 1) for i in idx]
    return [tn[i] * tables[i][0] + jnp.where(low, up[i], down[i]) * tables[i][1] for i in idx]


def _swa_kernel(q_ref, kc_ref, kp_ref, vc_ref, vp_ref, pc_ref, pp_ref, freq_ref, qg_ref, kg_ref, sink_ref, o_ref):
    blk, dh, grp = ATTN_WINDOW, ATTN_HEAD_DIM, ATTN_GROUP
    half = ROPE_DIM // 2
    n_kv = kc_ref.shape[1] // dh
    has_prev = pl.program_id(1) > 0

    lane = lax.broadcasted_iota(jnp.int32, (blk, LANES), 1)
    in_rot = lane % dh < ROPE_DIM

    src = lax.broadcasted_iota(jnp.int32, (LANES, LANES), 0)
    dst = lax.broadcasted_iota(jnp.int32, (LANES, LANES), 1) % dh
    hit = (src < half) & ((dst == src) | (dst == src + half))
    spread_cos = hit.astype(BF16)
    spread_sin = jnp.where(hit, jnp.where(dst < half, -1.0, 1.0), 0.0).astype(BF16)
    pad_rows = jnp.zeros((LANES - half, blk), F32)

    def spread(values, matrix):
        by_token = jnp.concatenate([values, pad_rows], axis=0).T
        hi, mid, lo = _split3(by_token)
        return _dot(hi, matrix) + _dot(mid, matrix) + _dot(lo, matrix)

    def rope_tables(pos_ref):
        ang = freq_ref[...] * pos_ref[...].astype(F32)
        cos_t = spread(jnp.cos(ang), spread_cos) + jnp.where(in_rot, 0.0, 1.0)
        return cos_t, spread(jnp.sin(ang), spread_sin)

    cos_c, sin_c = rope_tables(pc_ref)
    cos_p, sin_p = rope_tables(pp_ref)

    qi = lax.broadcasted_iota(jnp.int32, (blk, 2 * blk), 0) + blk
    kj = lax.broadcasted_iota(jnp.int32, (blk, 2 * blk), 1)
    first_key = jnp.where(has_prev, 0, blk)
    ok = (qi >= kj) & (qi - kj < ATTN_WINDOW) & (kj >= first_key)
    scale = dh ** -0.5
    assert LANES == 2 * dh and scale == 0.125
    first = lane < dh
    first2 = lax.broadcasted_iota(jnp.int32, (2 * blk, LANES), 1) < dh
    key_row = lax.broadcasted_iota(jnp.int32, (2 * blk, LANES), 0)
    half_ones = ((lax.broadcasted_iota(jnp.int32, (4 * blk, LANES), 0) < 2 * blk)
                 == (lax.broadcasted_iota(jnp.int32, (4 * blk, LANES), 1) < dh)).astype(BF16)
    slabs_per_kv = grp * dh // LANES

    k_dup, v_split = [], []
    kv_slabs = n_kv * dh // LANES
    k_in = [kp_ref[:, c * LANES:(c + 1) * LANES] for c in range(kv_slabs)]
    k_in += [kc_ref[:, c * LANES:(c + 1) * LANES] for c in range(kv_slabs)]
    k_roped = _qk_norm_rope(k_in, kg_ref[...], [(cos_p, sin_p)] * kv_slabs + [(cos_c, sin_c)] * kv_slabs, lane)
    q_slabs = q_ref.shape[1] // LANES
    q_roped = _qk_norm_rope([q_ref[:, c * LANES:(c + 1) * LANES] for c in range(q_slabs)], qg_ref[...],
                            [(cos_c, sin_c)] * q_slabs, lane)
    for c in range(kv_slabs):
        sl = slice(c * LANES, (c + 1) * LANES)
        kcat = jnp.concatenate([k_roped[c], k_roped[kv_slabs + c]], axis=0)
        vcat = jnp.concatenate([vp_ref[:, sl], vc_ref[:, sl]], axis=0)
        vcat = jnp.where(key_row == 0, 0.0, vcat)
        k_rot = pltpu.roll(kcat, dh, 1)
        v_rot = pltpu.roll(vcat, dh, 1)
        k_dup.append(jnp.where(first2, kcat, k_rot).astype(BF16))
        k_dup.append(jnp.where(first2, k_rot, kcat).astype(BF16))
        v_split.append(jnp.concatenate([jnp.where(first2, vcat, 0.0), jnp.where(first2, 0.0, v_rot)],
                                       axis=0).astype(BF16))
        v_split.append(jnp.concatenate([jnp.where(first2, v_rot, 0.0), jnp.where(first2, 0.0, vcat)],
                                       axis=0).astype(BF16))

    s_all = []
    for kv in range(n_kv):
        parts = []
        for c in range(kv * slabs_per_kv, (kv + 1) * slabs_per_kv):
            q2 = q_roped[c] * scale
            parts += [jnp.where(first, q2, 0.0), jnp.where(first, 0.0, q2)]
        q_st = jnp.concatenate(parts, axis=0).astype(BF16)
        s_all.append(_dot_nt(q_st, k_dup[kv]))

    heads = range(n_kv * grp)
    s = [jnp.where(ok, s_all[h // grp][(h % grp) * blk:(h % grp + 1) * blk], sink_ref[h:h + 1, :]) for h in heads]
    m = [jnp.max(s[h], axis=-1, keepdims=True) for h in heads]
    probs = [jnp.exp(s[h] - m[h]).astype(BF16) for h in heads]
    slabs = range(q_slabs)
    p_cat = [jnp.concatenate([probs[2 * c], probs[2 * c + 1]], axis=1) for c in slabs]
    outs = [_dot(p_cat[c], v_split[c // slabs_per_kv]) for c in slabs]
    dens = [_dot(p_cat[c], half_ones) for c in slabs]
    for c in slabs:
        o_ref[:, c * LANES:(c + 1) * LANES] = (outs[c] / dens[c]).astype(o_ref.dtype)


def _swa_mixer(x, positions, norm_gain, w_qkv, q_gain, k_gain, sinks, w_o, layer, *, batch, seq):
    tokens, d = x.shape
    blk, dh = ATTN_WINDOW, ATTN_HEAD_DIM
    nb = seq // blk
    n_q = sinks.shape[0]
    q_w = n_q * dh
    kv_w = (w_qkv.shape[2] - q_w) // 2
    qkv = _proj(x, w_qkv, layer, seq=seq, gain=norm_gain)
    assert blk == LANES
    pos = positions.reshape(tokens // blk, 1, blk)
    inv_freq = ROPE_THETA ** (-jnp.arange(0, ROPE_DIM, 2, dtype=F32) / ROPE_DIM)
    freq = inv_freq.reshape(ROPE_DIM // 2, 1)
    tile2 = lambda g: jnp.tile(g, LANES // dh).reshape(1, LANES)
    sink_rows = jnp.concatenate([sinks.reshape(n_q, 1), jnp.full((n_q, 2 * blk - 1), -jnp.inf, F32)], axis=1)
    cur = lambda b, i: b * nb + i
    prev = lambda b, i: b * nb + jnp.maximum(i - 1, 0)
    k_blk, v_blk = q_w // kv_w, q_w // kv_w + 1
    small = lambda b, i: (0, 0)
    attn = pl.pallas_call(
        _swa_kernel,
        grid=(batch, nb),
        in_specs=[
            pl.BlockSpec((blk, q_w), lambda b, i: (cur(b, i), 0)),
            pl.BlockSpec((blk, kv_w), lambda b, i: (cur(b, i), k_blk)),
            pl.BlockSpec((blk, kv_w), lambda b, i: (prev(b, i), k_blk)),
            pl.BlockSpec((blk, kv_w), lambda b, i: (cur(b, i), v_blk)),
            pl.BlockSpec((blk, kv_w), lambda b, i: (prev(b, i), v_blk)),
            pl.BlockSpec((None, 1, blk), lambda b, i: (cur(b, i), 0, 0)),
            pl.BlockSpec((None, 1, blk), lambda b, i: (prev(b, i), 0, 0)),
            pl.BlockSpec((ROPE_DIM // 2, 1), small),
            pl.BlockSpec((1, LANES), small),
            pl.BlockSpec((1, LANES), small),
            pl.BlockSpec((n_q, 2 * blk), small),
        ],
        out_specs=pl.BlockSpec((blk, q_w), lambda b, i: (cur(b, i), 0)),
        out_shape=jax.ShapeDtypeStruct((tokens, q_w), BF16),
        compiler_params=_compiler_params(("parallel", "arbitrary")),
        name="swa",
    )(qkv, qkv, qkv, qkv, qkv, pos, pos, freq, tile2(q_gain), tile2(k_gain), sink_rows)
    return _proj(attn, w_o, layer, seq=seq, residual=x)


def _rwkv_proj_kernel(x_ref, xb_ref, gain_ref, mix_ref, wl_ref, w_ref, o_ref, lo_ref, xm_ref, *, per_proj, seq):
    j = pl.program_id(1)
    tm, d = x_ref.shape
    pad = RWKV_LORA_PAD

    @pl.when(j == 0)
    def _():
        gain = gain_ref[...]
        rc = min(RWKV_PROLOGUE_ROWS, tm)
        row = lax.broadcasted_iota(jnp.int32, (rc, d), 0)
        starts_sequence = (pl.program_id(0) * tm) % seq == 0
        for c in range(tm // rc):
            rows = slice(c * rc, (c + 1) * rc)
            hn = _rmsnorm_rows(x_ref[rows, :], gain)
            if c == 0:
                before = jnp.where(starts_sequence, 0.0, xb_ref[SUBLANES - 1:SUBLANES, :])
            else:
                before = x_ref[c * rc - 1:c * rc, :]
            dx = jnp.where(row == 0, _rmsnorm_rows(before, gain), pltpu.roll(hn, 1, 0)) - hn

            def mixed(i):
                return (hn + dx * mix_ref[i:i + 1, :]).astype(BF16)

            xm_ref[0, rows, :] = mixed(0)
            xm_ref[1, rows, :] = mixed(2)
            xm_ref[2, rows, :] = mixed(3)
            lo_ref[rows, 0:pad] = _dot(mixed(1), wl_ref[:, 0:pad])
            lo_ref[rows, pad:2 * pad] = _dot(mixed(4), wl_ref[:, pad:2 * pad])
            lo_ref[rows, 2 * pad:] = _dot(mixed(5), wl_ref[:, 2 * pad:])

    o_ref[...] = _dot(xm_ref[j // per_proj], w_ref[...])


def _rwkv_scan_kernel(r_ref, k_ref, v_ref, lo_ref, w0_ref, wlb_ref, a0_ref, alb_ref, glb_ref,
                      kk_ref, ka_ref, rk_ref, lnw_ref, lnb_ref, o_ref, st_ref):
    C, N = RWKV_CHUNK, RWKV_HEAD
    pad = RWKV_LORA_PAD
    n_pairs = r_ref.shape[1] // LANES

    @pl.when(pl.program_id(1) == 0)
    def _():
        st_ref[...] = jnp.zeros_like(st_ref)

    lane = lax.broadcasted_iota(jnp.int32, (C, LANES), 1)
    first = lane < N
    row_c = lax.broadcasted_iota(jnp.int32, (C, C), 0)
    col_c = lax.broadcasted_iota(jnp.int32, (C, C), 1)
    tril_bf = (row_c >= col_c).astype(BF16)
    row2 = lax.broadcasted_iota(jnp.int32, (2 * C, 2 * C), 0)
    col2 = lax.broadcasted_iota(jnp.int32, (2 * C, 2 * C), 1)
    same_head = (row2 // C) == (col2 // C)
    eye2 = row2 == col2
    row_s = lax.broadcasted_iota(jnp.int32, (C, 2 * C), 0)
    col_s = lax.broadcasted_iota(jnp.int32, (C, 2 * C), 1) % C
    strict_lower = row_s > col_s
    lower = row_s >= col_s
    eye_side = row_s == col_s

    def head_sum(t):
        s0 = jnp.sum(jnp.where(first, t, 0.0), axis=-1, keepdims=True)
        s1 = jnp.sum(jnp.where(first, 0.0, t), axis=-1, keepdims=True)
        return jnp.where(first, s0, s1)

    def stack(t):
        return jnp.concatenate([jnp.where(first, t, 0.0), jnp.where(first, 0.0, t)], axis=0)

    lw1 = jnp.tanh(lo_ref[:, 0:pad]).astype(BF16)
    la1 = lo_ref[:, pad:2 * pad].astype(BF16)
    lg1 = _sigmoid(lo_ref[:, 2 * pad:]).astype(BF16)

    r_all = r_ref[...]
    k_all = k_ref[...]
    zw = w0_ref[...] + _dot(lw1, wlb_ref[...])
    log_decay = -math.exp(-0.5) * _sigmoid(zw)
    a_all = _sigmoid(a0_ref[...] + _dot(la1, alb_ref[...]))
    g_all = _dot(lg1, glb_ref[...])
    kk_all = k_all * kk_ref[...]
    k2_all = k_all * (1.0 + (a_all - 1.0) * ka_ref[...])
    cum_all = _cumsum_time(tril_bf, log_decay)
    total_all = cum_all[C - 1:C, :]
    p_incl = jnp.exp(cum_all)
    p_excl = jnp.exp(cum_all - log_decay)
    inv_p = jnp.exp(-cum_all)
    to_end = jnp.exp(total_all - cum_all)
    rt_all = r_all * p_incl
    rk2_all = r_all * k2_all * rk_ref[...]

    h2 = 2 * C

    def block_diag(side):
        return jnp.where(same_head, jnp.concatenate([side, side], axis=0), 0.0).astype(BF16)

    def run_pairs(pairs):
        sls = {p: slice(p * LANES, (p + 1) * LANES) for p in pairs}
        lhs, rhs, v_sts, ends = {}, {}, {}, {}
        for p in pairs:
            sl = sls[p]
            kk = kk_all[:, sl]
            kk = kk * lax.rsqrt(jnp.maximum(head_sum(kk * kk), 1e-24))
            kka = kk * a_all[:, sl]
            a_t = -kk * p_excl[:, sl]
            b_t = kka * inv_p[:, sl]
            k_t = k2_all[:, sl] * inv_p[:, sl]
            b_e = kka * to_end[:, sl]
            k_e = k2_all[:, sl] * to_end[:, sl]
            lhs[p] = jnp.concatenate([a_t, rt_all[:, sl]], axis=0).astype(BF16)
            rhs[p] = jnp.concatenate([stack(b_t), stack(k_t)], axis=0).astype(BF16)
            ends[p] = jnp.concatenate([b_e, k_e], axis=0).astype(BF16)
            v_sts[p] = stack(v_ref[:, sl]).astype(BF16)

        grams = {p: _dot_nt(lhs[p], rhs[p]) for p in pairs}
        a_ak = {p: jnp.where(strict_lower, grams[p][:C, h2:], 0.0).astype(BF16) for p in pairs}
        a_rbk = {p: jnp.concatenate([jnp.where(lower, grams[p][C:, :h2], 0.0),
                                     jnp.where(lower, grams[p][C:, h2:], 0.0)], axis=1).astype(BF16)
                 for p in pairs}

        pw = {p: jnp.where(strict_lower, grams[p][:C, :h2], 0.0) for p in pairs}
        pw_bd = {p: block_diag(pw[p]) for p in pairs}
        t_side = {p: jnp.where(eye_side, 1.0, 0.0) + pw[p] for p in pairs}
        for _ in range(C.bit_length() - 2):
            pw = {p: _dot(pw[p].astype(BF16), pw_bd[p]) for p in pairs}
            pw_bd = {p: block_diag(pw[p]) for p in pairs}
            t_side = {p: t_side[p] + _dot(t_side[p].astype(BF16), pw_bd[p]) for p in pairs}

        m0 = {p: st_ref[p] for p in pairs}
        m0_bf = {p: m0[p].astype(BF16) for p in pairs}
        lm = {p: _dot(lhs[p], m0_bf[p]) for p in pairs}
        x_rhs = {p: lm[p][:C] + _dot(a_ak[p], v_sts[p]) for p in pairs}
        u = {p: _dot(t_side[p].astype(BF16), stack(x_rhs[p]).astype(BF16)) for p in pairs}
        uv_st = {p: jnp.concatenate([stack(u[p]).astype(BF16), v_sts[p]], axis=0) for p in pairs}
        y_all = {p: lm[p][C:] + _dot(a_rbk[p], uv_st[p]) for p in pairs}

        for p in pairs:
            total_col = jnp.sum(jnp.where(eye2, total_all[:, sls[p]], 0.0), axis=-1, keepdims=True)
            uv = jnp.concatenate([u[p], v_ref[:, sls[p]]], axis=0).astype(BF16)
            st_ref[p] = jnp.exp(total_col) * m0[p] + jnp.where(same_head, _dot_tn(ends[p], uv), 0.0)

        for p in pairs:
            sl = sls[p]
            y = y_all[p]
            mu = head_sum(y) * (1.0 / N)
            yc = y - mu
            var = head_sum(yc * yc) * (1.0 / N)
            yn = yc * lax.rsqrt(var + RWKV_LN_EPS) * lnw_ref[:, sl] + lnb_ref[:, sl]
            bonus = head_sum(rk2_all[:, sl])
            o_ref[:, sl] = ((yn + bonus * v_ref[:, sl]) * g_all[:, sl]).astype(o_ref.dtype)

    group = min(RWKV_PAIR_GROUP, n_pairs)
    for first_pair in range(0, n_pairs, group):
        run_pairs(range(first_pair, first_pair + group))


def _rwkv_mixer(x, norm_gain, mix, w_rkv, w0, w_la, w_lb, a0, a_la, a_lb, g_la, g_lb, k_k, k_a, r_k,
                ln_w, ln_b, w_o, layer, *, batch, seq):
    tokens, d = x.shape
    C = RWKV_CHUNK
    pad = RWKV_LORA_PAD
    tm = _row_tile(seq, 1024)
    tn = _col_tile(d, 512)
    per_proj = d // tn
    n_tiles = tokens // tm
    lora_w = w_la.shape[1], a_la.shape[1], g_la.shape[1]
    lora_cols = 2 * pad + lora_w[2]
    assert lora_w[0] <= pad and lora_w[1] <= pad and 2 * C == LANES
    w_lora = jnp.concatenate([_pad_cols(w_la, pad), _pad_cols(a_la, pad), g_la], axis=1).astype(BF16)
    proj, lora = pl.pallas_call(
        functools.partial(_rwkv_proj_kernel, per_proj=per_proj, seq=seq),
        grid=(n_tiles, 3 * per_proj),
        in_specs=[
            pl.BlockSpec((tm, d), lambda i, j: (i, 0)),
            pl.BlockSpec((SUBLANES, d), lambda i, j: (jnp.maximum(i * (tm // SUBLANES) - 1, 0), 0)),
            pl.BlockSpec((1, d), lambda i, j: (0, 0)),
            pl.BlockSpec((6, d), lambda i, j: (0, 0)),
            pl.BlockSpec((d, lora_cols), lambda i, j: (0, 0)),
            pl.BlockSpec((None, d, tn), lambda i, j: (j // per_proj, 0, j % per_proj)),
        ],
        out_specs=[
            pl.BlockSpec((tm, tn), lambda i, j: (i, j)),
            pl.BlockSpec((tm, lora_cols), lambda i, j: (i, 0)),
        ],
        out_shape=[
            jax.ShapeDtypeStruct((tokens, 3 * d), F32),
            jax.ShapeDtypeStruct((tokens, lora_cols), F32),
        ],
        scratch_shapes=[pltpu.VMEM((3, tm, d), BF16)],
        compiler_params=_compiler_params(("parallel", "arbitrary")),
        name="rwkv_proj",
    )(x, x, norm_gain.reshape(1, d), mix, w_lora, w_rkv.astype(BF16))

    nc = seq // C
    row1 = lambda t: t.reshape(1, d)
    pad_rows = lambda w: jnp.pad(w, ((0, pad - w.shape[0]), (0, 0))).astype(BF16)
    chunk = lambda blk: (lambda b, c: (b * nc + c, blk))
    const = lambda b, c: (0, 0)
    vec = pl.BlockSpec((1, d), const)
    mixed = pl.pallas_call(
        _rwkv_scan_kernel,
        grid=(batch, nc),
        in_specs=[
            pl.BlockSpec((C, d), chunk(0)),
            pl.BlockSpec((C, d), chunk(1)),
            pl.BlockSpec((C, d), chunk(2)),
            pl.BlockSpec((C, lora_cols), chunk(0)),
            vec,
            pl.BlockSpec((pad, d), const),
            vec,
            pl.BlockSpec((pad, d), const),
            pl.BlockSpec((lora_w[2], d), const),
            vec, vec, vec, vec, vec,
        ],
        out_specs=pl.BlockSpec((C, d), chunk(0)),
        out_shape=jax.ShapeDtypeStruct((tokens, d), BF16),
        scratch_shapes=[pltpu.VMEM((d // LANES, LANES, LANES), F32)],
        compiler_params=_compiler_params(("parallel", "arbitrary")),
        name="rwkv_scan",
    )(proj, proj, proj, lora, row1(w0), pad_rows(w_lb), row1(a0), pad_rows(a_lb), g_lb.astype(BF16),
      row1(k_k), row1(k_a), row1(r_k), row1(ln_w), row1(ln_b))
    return _proj(mixed, w_o, layer, seq=seq, residual=x)


def kernel(x, positions, ffn1_norm, ffn1_w_gu, ffn1_w_down, mixer_norm, ffn2_norm, ffn2_w_gu, ffn2_w_down,
           mlstm_w_in, mlstm_b_gate, mlstm_head_gain, mlstm_w_out,
           attn_w_qkv, attn_q_gain, attn_k_gain, attn_sinks, attn_w_o,
           rwkv_mix, rwkv_w_rkv, rwkv_w0, rwkv_w_lora_a, rwkv_w_lora_b, rwkv_a0, rwkv_a_lora_a, rwkv_a_lora_b,
           rwkv_g_lora_a, rwkv_g_lora_b, rwkv_k_k, rwkv_k_a, rwkv_r_k, rwkv_ln_w, rwkv_ln_b, rwkv_w_o):
    batch, seq, d = x.shape
    depth = ffn1_norm.shape[0]
    h = x.reshape(batch * seq, d)
    ffn1_gu, ffn1_down = ffn1_w_gu.astype(BF16), ffn1_w_down.astype(BF16)
    ffn2_gu, ffn2_down = ffn2_w_gu.astype(BF16), ffn2_w_down.astype(BF16)
    mlstm_in = jnp.pad(mlstm_w_in, ((0, 0), (0, 0), (0, MLSTM_GATE_PAD - 2 * MLSTM_HEADS))).astype(BF16)
    mlstm_out = mlstm_w_out.astype(BF16)
    attn_qkv, attn_o = attn_w_qkv.astype(BF16), attn_w_o.astype(BF16)
    rwkv_o = rwkv_w_o.astype(BF16)
    for i in range(depth):
        h = _ffn(h, ffn1_norm[i], ffn1_gu, ffn1_down, i, seq=seq)
        kind, j = i % 3, i // 3
        if kind == 0:
            h = _mlstm_mixer(h, mixer_norm[i], mlstm_in, mlstm_b_gate[j], mlstm_head_gain[j], mlstm_out, j,
                             batch=batch, seq=seq)
        elif kind == 1:
            h = _swa_mixer(h, positions, mixer_norm[i], attn_qkv, attn_q_gain[j], attn_k_gain[j],
                           attn_sinks[j], attn_o, j, batch=batch, seq=seq)
        else:
            h = _rwkv_mixer(h, mixer_norm[i], rwkv_mix[j], rwkv_w_rkv[j], rwkv_w0[j], rwkv_w_lora_a[j],
                            rwkv_w_lora_b[j], rwkv_a0[j], rwkv_a_lora_a[j], rwkv_a_lora_b[j], rwkv_g_lora_a[j],
                            rwkv_g_lora_b[j], rwkv_k_k[j], rwkv_k_a[j], rwkv_r_k[j], rwkv_ln_w[j], rwkv_ln_b[j],
                            rwkv_o, j, batch=batch, seq=seq)
        h = _ffn(h, ffn2_norm[i], ffn2_gu, ffn2_down, i, seq=seq)
    return h.reshape(batch, seq, d)
```
